```python
import jax, jax.numpy as jnp
from jax import lax
import numpy as np

D_MODEL = 2048
BATCH = 4
SEQ = 4096
DEPTH = 4
DEC_BATCH = 2
DEC_SEQ = 4096
PAST_LEN = 128

N_MIXERS = 3
N_LAYERS_A = (DEPTH + 2) // 3
N_LAYERS_B = (DEPTH + 1) // 3
N_LAYERS_C = DEPTH // 3
BLOCK = 128
HEAD_DIM = 128
EPS = 1e-6
NEG_INF = -1e30
GMLP_HALF = 2 * D_MODEL
GMLP_GROUPS = 8
GMLP_GDIM = GMLP_HALF // GMLP_GROUPS
N_HEADS = D_MODEL // HEAD_DIM
N_KV_HEADS = 4
KV_GROUP = N_HEADS // N_KV_HEADS
WINDOW = 128
N_BUCKETS = 32
MAX_DISTANCE = 128
D_RNN = ((4 * D_MODEL // 3 + 255) // 256) * 256
RNN_BLOCKS = 16
RNN_BDIM = D_RNN // RNN_BLOCKS
CONV_WIDTH = 4
CONV_LEFT = 2
LRU_C = 8.0
MEM_LEN = 256
MEM_HEADS = 4
MEM_WIDTH = MEM_HEADS * HEAD_DIM
D_FF = ((8 * D_MODEL // 3 + 255) // 256) * 256
N_NORMS = 7

kernel_name = 'hybrid_bidir_encoder'


def rmsnorm(x, g):
    xf = x.astype(jnp.float32)
    y = xf * lax.rsqrt(jnp.mean(xf * xf, axis=-1, keepdims=True) + EPS)
    return (y * g.astype(jnp.float32)).astype(x.dtype)


def layernorm(x, g, b):
    xf = x.astype(jnp.float32)
    mu = jnp.mean(xf, axis=-1, keepdims=True)
    xc = xf - mu
    var = jnp.mean(xc * xc, axis=-1, keepdims=True)
    return (xc * lax.rsqrt(var + EPS) * g.astype(jnp.float32) + b.astype(jnp.float32)).astype(x.dtype)


def t5_bucket(rel):
    half = N_BUCKETS // 2
    exact = half // 2
    n = np.abs(rel)
    large = exact + (np.log(np.maximum(n, 1) / exact) / np.log(MAX_DISTANCE / exact) * (half - exact)).astype(np.int32)
    large = np.minimum(large, half - 1)
    return (np.where(rel > 0, half, 0) + np.where(n < exact, n, large)).astype(np.int32)


def gmlp_mixer(h, w_in, ln_g, ln_b, w_s, b_s, w_out):
    B, S, _ = h.shape
    z = jax.nn.gelu(h @ w_in)
    u, v = jnp.split(z, 2, axis=-1)
    v = layernorm(v, ln_g, ln_b)
    v = v.reshape(B, S // BLOCK, BLOCK, GMLP_GROUPS, GMLP_GDIM)
    v = jnp.einsum('gpq,bnqgc->bnpgc', w_s, v) + b_s.T[None, None, :, :, None]
    return (u * v.reshape(B, S, GMLP_HALF)) @ w_out


def window_attention(h, w_qkv, sink, w_o, rel_table):
    B, S, _ = h.shape
    NB = S // BLOCK
    q, k, v = jnp.split(h @ w_qkv, [N_HEADS * HEAD_DIM, (N_HEADS + N_KV_HEADS) * HEAD_DIM], axis=-1)
    q = q.reshape(B, NB, BLOCK, N_KV_HEADS, KV_GROUP, HEAD_DIM)

    def band(t):
        t = t.reshape(B, S, N_KV_HEADS, HEAD_DIM)
        t = jnp.pad(t, ((0, 0), (BLOCK, BLOCK), (0, 0), (0, 0))).reshape(B, NB + 2, BLOCK, N_KV_HEADS, HEAD_DIM)
        return jnp.concatenate([t[:, :-2], t[:, 1:-1], t[:, 2:]], axis=2)

    kb, vb = band(k), band(v)
    s = jnp.einsum('bnqkgd,bnjkd->bnkgqj', q, kb).astype(jnp.float32) * (HEAD_DIM ** -0.5)
    rel = np.arange(3 * BLOCK)[None, :] - BLOCK - np.arange(BLOCK)[:, None]
    bias = rel_table[t5_bucket(rel)].astype(jnp.float32)
    bias = jnp.transpose(bias, (2, 0, 1)).reshape(N_KV_HEADS, KV_GROUP, BLOCK, 3 * BLOCK)
    kpos = np.arange(NB)[:, None, None] * BLOCK - BLOCK + np.arange(3 * BLOCK)[None, None, :]
    qpos = np.arange(NB)[:, None, None] * BLOCK + np.arange(BLOCK)[None, :, None]
    valid = (kpos >= 0) & (kpos < S) & (np.abs(kpos - qpos) <= WINDOW)
    s = jnp.where(valid[None, :, None, None], s + bias[None, None], NEG_INF)
    sk = sink.astype(jnp.float32).reshape(1, 1, N_KV_HEADS, KV_GROUP, 1, 1)
    m = jnp.maximum(jnp.max(s, axis=-1, keepdims=True), sk)
    p = jnp.exp(s - m)
    p = p / (jnp.sum(p, axis=-1, keepdims=True) + jnp.exp(sk - m))
    o = jnp.einsum('bnkgqj,bnjkd->bnqkgd', p.astype(h.dtype), vb).reshape(B, S, N_HEADS * HEAD_DIM)
    return o @ w_o


def _lin_combine(left, right):
    a1, b1 = left
    a2, b2 = right
    return a1 * a2, a2 * b1 + b2


def rglru_scan(x, w_a, b_a, w_x, b_x, lam, reverse):
    B, S, _ = x.shape
    xb = x.reshape(B, S, RNN_BLOCKS, RNN_BDIM)
    r = jax.nn.sigmoid((jnp.einsum('bshi,hij->bshj', xb, w_a).reshape(B, S, D_RNN) + b_a).astype(jnp.float32))
    gi = jax.nn.sigmoid((jnp.einsum('bshi,hij->bshj', xb, w_x).reshape(B, S, D_RNN) + b_x).astype(jnp.float32))
    log_a = -LRU_C * r * jax.nn.softplus(-lam.astype(jnp.float32))
    a = jnp.exp(log_a)
    u = jnp.sqrt(-jnp.expm1(2.0 * log_a)) * (gi * x.astype(jnp.float32))
    _, hs = lax.associative_scan(_lin_combine, (a, u), axis=1, reverse=reverse)
    return hs


def rglru_block(h, w_in, conv_w, conv_b, w_a, b_a, w_x, b_x, lam, w_out):
    S = h.shape[1]
    gate_br, rnn_br = jnp.split(h @ w_in, 2, axis=-1)
    xp = jnp.pad(rnn_br, ((0, 0), (CONV_LEFT, CONV_WIDTH - 1 - CONV_LEFT), (0, 0)))
    xc = conv_b + xp[:, 0:S] * conv_w[0]
    for k in range(1, CONV_WIDTH):
        xc = xc + xp[:, k:k + S] * conv_w[k]
    y = (rglru_scan(xc, w_a[0], b_a[0], w_x[0], b_x[0], lam[0], False)
         + rglru_scan(xc, w_a[1], b_a[1], w_x[1], b_x[1], lam[1], True))
    return (jax.nn.gelu(gate_br) * y.astype(h.dtype)) @ w_out


def memory_attention(h, mem, w_q, w_kv, w_o):
    B, S, _ = h.shape
    M = mem.shape[1]
    q = (h @ w_q).reshape(B, S, MEM_HEADS, HEAD_DIM)
    k, v = jnp.split(mem @ w_kv, 2, axis=-1)
    k = k.reshape(B, M, MEM_HEADS, HEAD_DIM)
    v = v.reshape(B, M, MEM_HEADS, HEAD_DIM)
    s = jnp.einsum('bshd,bmhd->bhsm', q, k).astype(jnp.float32) * (HEAD_DIM ** -0.5)
    p = jax.nn.softmax(s, axis=-1)
    o = jnp.einsum('bhsm,bmhd->bshd', p.astype(h.dtype), v).reshape(B, S, MEM_WIDTH)
    return o @ w_o


def swiglu(h, w_gate_up, w_down):
    g, u = jnp.split(h @ w_gate_up, 2, axis=-1)
    return (jax.nn.silu(g) * u) @ w_down


def trunk(x, mem, p):
    for i in range(DEPTH):
        kind, j = i % N_MIXERS, i // N_MIXERS
        g = p['norms'][i]
        h = rmsnorm(x, g[0])
        if kind == 0:
            h = gmlp_mixer(h, p['a_w_in'][j], p['a_ln_g'][j], p['a_ln_b'][j], p['a_w_s'][j], p['a_b_s'][j], p['a_w_out'][j])
        elif kind == 1:
            h = window_attention(h, p['b_w_qkv'][j], p['b_sink'][j], p['b_w_o'][j], p['rel_bias'])
        else:
            h = rglru_block(h, p['c_w_in'][j], p['c_conv_w'][j], p['c_conv_b'][j], p['c_w_a'][j], p['c_b_a'][j],
                            p['c_w_x'][j], p['c_b_x'][j], p['c_lam'][j], p['c_w_out'][j])
        x = x + rmsnorm(h, g[1])
        h = memory_attention(rmsnorm(x, g[2]), rmsnorm(mem, g[4]), p['m_w_q'][i], p['m_w_kv'][i], p['m_w_o'][i])
        x = x + rmsnorm(h, g[3])
        h = swiglu(rmsnorm(x, g[5]), p['f_w_gate_up'][i], p['f_w_down'][i])
        x = x + rmsnorm(h, g[6])
    return x


def _normal(k, shape, scale):
    return jax.random.normal(k, shape, jnp.float32) * scale


def setup_inputs(seed: int = 0) -> dict:
    key = jax.random.key(seed)
    ks = list(jax.random.split(key, 32))
    D = D_MODEL
    QKV = (N_HEADS + 2 * N_KV_HEADS) * HEAD_DIM
    a8 = jax.random.uniform(ks[27], (N_LAYERS_C, 2, D_RNN), jnp.float32, 0.9, 0.999)
    a_base = a8 ** (1.0 / LRU_C)
    c_lam = jnp.log(a_base) - jnp.log1p(-a_base)
    return {
        'x_prompt': _normal(ks[0], (BATCH, SEQ, D), 1.0),
        'x_sample': _normal(ks[1], (DEC_BATCH, DEC_SEQ, D), 1.0),
        'mem_prompt': _normal(ks[2], (BATCH, MEM_LEN, D), 1.0),
        'mem_sample': _normal(ks[3], (DEC_BATCH, MEM_LEN, D), 1.0),
        'norms': 1.0 + _normal(ks[4], (DEPTH, N_NORMS, D), 0.05),
        'rel_bias': _normal(ks[5], (N_BUCKETS, N_HEADS), 0.1),
        'a_w_in': _normal(ks[6], (N_LAYERS_A, D, 2 * GMLP_HALF), D ** -0.5),
        'a_ln_g': 1.0 + _normal(ks[7], (N_LAYERS_A, GMLP_HALF), 0.05),
        'a_ln_b': _normal(ks[8], (N_LAYERS_A, GMLP_HALF), 0.02),
        'a_w_s': _normal(ks[9], (N_LAYERS_A, GMLP_GROUPS, BLOCK, BLOCK), BLOCK ** -0.5),
        'a_b_s': 1.0 + _normal(ks[10], (N_LAYERS_A, GMLP_GROUPS, BLOCK), 0.02),
        'a_w_out': _normal(ks[11], (N_LAYERS_A, GMLP_HALF, D), GMLP_HALF ** -0.5),
        'b_w_qkv': _normal(ks[12], (N_LAYERS_B, D, QKV), D ** -0.5),
        'b_sink': _normal(ks[13], (N_LAYERS_B, N_HEADS), 0.5),
        'b_w_o': _normal(ks[14], (N_LAYERS_B, N_HEADS * HEAD_DIM, D), (N_HEADS * HEAD_DIM) ** -0.5),
        'c_w_in': _normal(ks[15], (N_LAYERS_C, D, 2 * D_RNN), D ** -0.5),
        'c_conv_w': _normal(ks[16], (N_LAYERS_C, CONV_WIDTH, D_RNN), CONV_WIDTH ** -0.5),
        'c_conv_b': _normal(ks[17], (N_LAYERS_C, D_RNN), 0.02),
        'c_w_a': _normal(ks[18], (N_LAYERS_C, 2, RNN_BLOCKS, RNN_BDIM, RNN_BDIM), RNN_BDIM ** -0.5),
        'c_b_a': _normal(ks[19], (N_LAYERS_C, 2, D_RNN), 0.02),
        'c_w_x': _normal(ks[20], (N_LAYERS_C, 2, RNN_BLOCKS, RNN_BDIM, RNN_BDIM), RNN_BDIM ** -0.5),
        'c_b_x': _normal(ks[21], (N_LAYERS_C, 2, D_RNN), 0.02),
        'c_lam': c_lam,
        'c_w_out': _normal(ks[22], (N_LAYERS_C, D_RNN, D), D_RNN ** -0.5),
        'm_w_q': _normal(ks[23], (DEPTH, D, MEM_WIDTH), D ** -0.5),
        'm_w_kv': _normal(ks[24], (DEPTH, D, 2 * MEM_WIDTH), D ** -0.5),
        'm_w_o': _normal(ks[25], (DEPTH, MEM_WIDTH, D), MEM_WIDTH ** -0.5),
        'f_w_gate_up': _normal(ks[26], (DEPTH, D, 2 * D_FF), D ** -0.5),
        'f_w_down': _normal(ks[28], (DEPTH, D_FF, D), D_FF ** -0.5),
    }


def reference(x_prompt, x_sample, mem_prompt, mem_sample, norms, rel_bias,
              a_w_in, a_ln_g, a_ln_b, a_w_s, a_b_s, a_w_out,
              b_w_qkv, b_sink, b_w_o,
              c_w_in, c_conv_w, c_conv_b, c_w_a, c_b_a, c_w_x, c_b_x, c_lam, c_w_out,
              m_w_q, m_w_kv, m_w_o, f_w_gate_up, f_w_down):
    p = {
        'norms': norms, 'rel_bias': rel_bias,
        'a_w_in': a_w_in, 'a_ln_g': a_ln_g, 'a_ln_b': a_ln_b, 'a_w_s': a_w_s, 'a_b_s': a_b_s, 'a_w_out': a_w_out,
        'b_w_qkv': b_w_qkv, 'b_sink': b_sink, 'b_w_o': b_w_o,
        'c_w_in': c_w_in, 'c_conv_w': c_conv_w, 'c_conv_b': c_conv_b, 'c_w_a': c_w_a, 'c_b_a': c_b_a,
        'c_w_x': c_w_x, 'c_b_x': c_b_x, 'c_lam': c_lam, 'c_w_out': c_w_out,
        'm_w_q': m_w_q, 'm_w_kv': m_w_kv, 'm_w_o': m_w_o,
        'f_w_gate_up': f_w_gate_up, 'f_w_down': f_w_down,
    }
    y_prompt = trunk(x_prompt, mem_prompt, p)
    y_sample = trunk(x_sample, mem_sample, p)
    return (y_prompt, y_sample)
```

```python
import functools

import numpy as np
import jax
import jax.numpy as jnp
from jax import lax
from jax.experimental import pallas as pl
from jax.experimental.pallas import tpu as pltpu

D_MODEL = 2048
SEQ = 4096
DEPTH = 4
N_MIXERS = 3
BLOCK = 128
HEAD_DIM = 128
EPS = 1e-6
NEG_INF = -1e30
GMLP_HALF = 2 * D_MODEL
GMLP_GROUPS = 8
GMLP_GDIM = GMLP_HALF // GMLP_GROUPS
N_HEADS = D_MODEL // HEAD_DIM
N_KV_HEADS = 4
KV_GROUP = N_HEADS // N_KV_HEADS
WINDOW = 128
N_BUCKETS = 32
MAX_DISTANCE = 128
D_RNN = 2816
RNN_BLOCKS = 16
RNN_BDIM = D_RNN // RNN_BLOCKS
CONV_WIDTH = 4
CONV_LEFT = 2
LRU_C = 8.0
MEM_LEN = 256
MEM_HEADS = 4
MEM_WIDTH = MEM_HEADS * HEAD_DIM
D_FF = 5632

LANES = 128
SUBLANES = 8
VMEM_LIMIT = 56 * 1024 * 1024

F32 = jnp.float32
BF16 = jnp.bfloat16


def _params(sem):
    return pltpu.CompilerParams(dimension_semantics=sem, vmem_limit_bytes=VMEM_LIMIT)


def _rms(x, g):
    ms = jnp.mean(x * x, axis=-1, keepdims=True)
    return x * lax.rsqrt(ms + EPS) * g


def _gelu(x):
    c = np.sqrt(2.0 / np.pi).astype(np.float32)
    return 0.5 * x * (1.0 + jnp.tanh(c * (x + 0.044715 * (x * x * x))))


def _sigmoid(x):
    return 1.0 / (1.0 + jnp.exp(-x))


def _dot(a, b):
    return jnp.dot(a, b, preferred_element_type=F32)


def _norm_mm_kernel(x_ref, g_ref, w_ref, o_ref, h_scr, *, n_gelu, n_tiles, tile_major):
    j = pl.program_id(1)

    @pl.when(j == 0)
    def _():
        h_scr[...] = _rms(x_ref[...], g_ref[...]).astype(BF16)

    def emit(act):
        z = _dot(h_scr[...], w_ref[...])
        if act:
            z = _gelu(z)
        if tile_major:
            o_ref[0] = z.astype(o_ref.dtype)
        else:
            o_ref[...] = z.astype(o_ref.dtype)

    if n_gelu == 0:
        emit(False)
    elif n_gelu == n_tiles:
        emit(True)
    else:
        pl.when(j < n_gelu)(lambda: emit(True))
        pl.when(j >= n_gelu)(lambda: emit(False))


def _norm_matmul(x, g, w, *, tm, tn, n_gelu=0, tile_major=False):
    t, d = x.shape
    n = w.shape[1]
    n_tiles = n // tn
    if tile_major:
        out_shape = jax.ShapeDtypeStruct((n_tiles, t, tn), BF16)
        out_spec = pl.BlockSpec((1, tm, tn), lambda i, j: (j, i, 0))
    else:
        out_shape = jax.ShapeDtypeStruct((t, n), BF16)
        out_spec = pl.BlockSpec((tm, tn), lambda i, j: (i, j))
    return pl.pallas_call(
        functools.partial(_norm_mm_kernel, n_gelu=n_gelu, n_tiles=n_tiles, tile_major=tile_major),
        grid=(t // tm, n_tiles),
        in_specs=[
            pl.BlockSpec((tm, d), lambda i, j: (i, 0)),
            pl.BlockSpec((1, d), lambda i, j: (0, 0)),
            pl.BlockSpec((d, tn), lambda i, j: (0, j)),
        ],
        out_specs=out_spec,
        out_shape=out_shape,
        scratch_shapes=[pltpu.VMEM((tm, d), BF16)],
        compiler_params=_params(("parallel", "arbitrary")),
        name="norm_matmul",
    )(x, g, w)


def _mm_norm_res_kernel(a_ref, w_ref, x_ref, g_ref, o_ref):
    y = _dot(a_ref[...], w_ref[...])
    o_ref[...] = x_ref[...] + _rms(y, g_ref[...])


def _matmul_norm_res(a, w, x, g, *, tm):
    t, k = a.shape
    d = w.shape[1]
    return pl.pallas_call(
        _mm_norm_res_kernel,
        grid=(t // tm,),
        in_specs=[
            pl.BlockSpec((tm, k), lambda i: (i, 0)),
            pl.BlockSpec((k, d), lambda i: (0, 0)),
            pl.BlockSpec((tm, d), lambda i: (i, 0)),
            pl.BlockSpec((1, d), lambda i: (0, 0)),
        ],
        out_specs=pl.BlockSpec((tm, d), lambda i: (i, 0)),
        out_shape=jax.ShapeDtypeStruct((t, d), F32),
        compiler_params=_params(("parallel",)),
        name="matmul_norm_res",
    )(a, w, x, g)


def _gmlp_out_kernel(u_ref, v_ref, lng_ref, lnb_ref, ws_ref, bs_ref, wo_ref, x_ref, g_ref,
                     o_ref, mu_scr, rs_scr, acc_scr, *, tm):
    grp = pl.program_id(1)

    @pl.when(grp == 0)
    def _():
        s = jnp.zeros((tm, 1), F32)
        for c in range(GMLP_GROUPS):
            s = s + jnp.sum(v_ref[c].astype(F32), axis=-1, keepdims=True)
        mu = s * (1.0 / GMLP_HALF)
        q = jnp.zeros((tm, 1), F32)
        for c in range(GMLP_GROUPS):
            dv = v_ref[c].astype(F32) - mu
            q = q + jnp.sum(dv * dv, axis=-1, keepdims=True)
        mu_scr[...] = mu
        rs_scr[...] = lax.rsqrt(q * (1.0 / GMLP_HALF) + EPS)

    vg = v_ref[grp].astype(F32)
    vn = ((vg - mu_scr[...]) * rs_scr[...] * lng_ref[0] + lnb_ref[0]).astype(BF16)
    ws = ws_ref[0]
    bs = bs_ref[0]
    parts = []
    for n in range(tm // BLOCK):
        rows = slice(n * BLOCK, (n + 1) * BLOCK)
        mixed = _dot(ws, vn[rows, :]) + bs
        parts.append((u_ref[0, rows, :].astype(F32) * mixed).astype(BF16))
    p = _dot(jnp.concatenate(parts, axis=0), wo_ref[...])

    @pl.when(grp == 0)
    def _():
        acc_scr[...] = p

    @pl.when(grp > 0)
    def _():
        acc_scr[...] += p

    @pl.when(grp == GMLP_GROUPS - 1)
    def _():
        o_ref[...] = x_ref[...] + _rms(acc_scr[...], g_ref[...])


def _gmlp_out(z, ln_g, ln_b, w_s, b_s, w_out, x, g, *, tm):
    t, d = x.shape
    return pl.pallas_call(
        functools.partial(_gmlp_out_kernel, tm=tm),
        grid=(t // tm, GMLP_GROUPS),
        in_specs=[
            pl.BlockSpec((1, tm, GMLP_GDIM), lambda i, c: (c, i, 0)),
            pl.BlockSpec((GMLP_GROUPS, tm, GMLP_GDIM), lambda i, c: (1, i, 0)),
            pl.BlockSpec((1, 1, GMLP_GDIM), lambda i, c: (c, 0, 0)),
            pl.BlockSpec((1, 1, GMLP_GDIM), lambda i, c: (c, 0, 0)),
            pl.BlockSpec((1, BLOCK, BLOCK), lambda i, c: (c, 0, 0)),
            pl.BlockSpec((1, BLOCK, 1), lambda i, c: (c, 0, 0)),
            pl.BlockSpec((GMLP_GDIM, d), lambda i, c: (c, 0)),
            pl.BlockSpec((tm, d), lambda i, c: (i, 0)),
            pl.BlockSpec((1, d), lambda i, c: (0, 0)),
        ],
        out_specs=pl.BlockSpec((tm, d), lambda i, c: (i, 0)),
        out_shape=jax.ShapeDtypeStruct((t, d), F32),
        scratch_shapes=[
            pltpu.VMEM((tm, 1), F32),
            pltpu.VMEM((tm, 1), F32),
            pltpu.VMEM((tm, d), F32),
        ],
        compiler_params=_params(("parallel", "arbitrary")),
        name="gmlp_out",
    )(z, z, ln_g, ln_b, w_s, b_s, w_out, x, g)


def _win_attn_kernel(sink_ref, q_ref, kp_ref, kc_ref, kn_ref, vp_ref, vc_ref, vn_ref, bias_ref,
                     o_ref, *, r_blocks, n_steps):
    i = pl.program_id(1)
    kall = jnp.concatenate([kp_ref[0], kc_ref[0], kn_ref[0]], axis=0)
    vall = jnp.concatenate([vp_ref[0], vc_ref[0], vn_ref[0]], axis=0)
    col = lax.broadcasted_iota(jnp.int32, (1, 3 * BLOCK), 1)
    scale = HEAD_DIM ** -0.5
    for r in range(r_blocks):
        rows = slice(r * BLOCK, (r + 1) * BLOCK)
        oob = None
        if r == 0:
            oob = (col < BLOCK) & (i == 0)
        if r == r_blocks - 1:
            hi = (col >= 2 * BLOCK) & (i == n_steps - 1)
            oob = hi if oob is None else (oob | hi)
        for kh in range(N_KV_HEADS):
            ksl = slice(kh * HEAD_DIM, (kh + 1) * HEAD_DIM)
            kb = kall[r * BLOCK:(r + 3) * BLOCK, ksl]
            vb = vall[r * BLOCK:(r + 3) * BLOCK, ksl]
            heads = [kh * KV_GROUP + gq for gq in range(KV_GROUP)]
            qs = jnp.concatenate(
                [q_ref[0, rows, h * HEAD_DIM:(h + 1) * HEAD_DIM] for h in heads], axis=0)
            s = lax.dot_general(qs, kb, (((1,), (1,)), ((), ())), preferred_element_type=F32)
            s = s * scale + bias_ref[kh]
            if oob is not None:
                s = jnp.where(oob, NEG_INF, s)
            sk = jnp.concatenate(
                [jnp.full((BLOCK, 1), sink_ref[h], F32) for h in heads], axis=0)
            m = jnp.maximum(jnp.max(s, axis=-1, keepdims=True), sk)
            p = jnp.exp(s - m)
            denom = jnp.sum(p, axis=-1, keepdims=True) + jnp.exp(sk - m)
            o = _dot(p.astype(BF16), vb) / denom
            for gq, h in enumerate(heads):
                o_ref[0, rows, h * HEAD_DIM:(h + 1) * HEAD_DIM] = (
                    o[gq * BLOCK:(gq + 1) * BLOCK].astype(o_ref.dtype))


def _win_attention(qkv, sink, bias, *, nseq, r_blocks):
    tq = r_blocks * BLOCK
    n_steps = SEQ // tq
    nb = SEQ // BLOCK
    qkv3 = qkv.reshape(nseq, SEQ, qkv.shape[-1])
    kcol = N_HEADS * HEAD_DIM // (N_KV_HEADS * HEAD_DIM)
    vcol = kcol + 1
    kvw = N_KV_HEADS * HEAD_DIM

    def prev_map(c):
        return lambda b, i: (b, jnp.maximum(i * r_blocks - 1, 0), c)

    def cur_map(c):
        return lambda b, i: (b, i, c)

    def next_map(c):
        return lambda b, i: (b, jnp.minimum((i + 1) * r_blocks, nb - 1), c)

    out = pl.pallas_call(
        functools.partial(_win_attn_kernel, r_blocks=r_blocks, n_steps=n_steps),
        grid=(nseq, n_steps),
        in_specs=[
            pl.BlockSpec(memory_space=pltpu.SMEM),
            pl.BlockSpec((1, tq, N_HEADS * HEAD_DIM), lambda b, i: (b, i, 0)),
            pl.BlockSpec((1, BLOCK, kvw), prev_map(kcol)),
            pl.BlockSpec((1, tq, kvw), cur_map(kcol)),
            pl.BlockSpec((1, BLOCK, kvw), next_map(kcol)),
            pl.BlockSpec((1, BLOCK, kvw), prev_map(vcol)),
            pl.BlockSpec((1, tq, kvw), cur_map(vcol)),
            pl.BlockSpec((1, BLOCK, kvw), next_map(vcol)),
            pl.BlockSpec((N_KV_HEADS, KV_GROUP * BLOCK, 3 * BLOCK), lambda b, i: (0, 0, 0)),
        ],
        out_specs=pl.BlockSpec((1, tq, N_HEADS * HEAD_DIM), lambda b, i: (b, i, 0)),
        out_shape=jax.ShapeDtypeStruct((nseq, SEQ, N_HEADS * HEAD_DIM), BF16),
        compiler_params=_params(("parallel", "parallel")),
        name="win_attention",
    )(sink, qkv3, qkv3, qkv3, qkv3, qkv3, qkv3, qkv3, bias)
    return out.reshape(nseq * SEQ, N_HEADS * HEAD_DIM)


def _t5_bucket(rel):
    half = N_BUCKETS // 2
    exact = half // 2
    n = np.abs(rel)
    large = exact + (np.log(np.maximum(n, 1) / exact) / np.log(MAX_DISTANCE / exact)
                     * (half - exact)).astype(np.int32)
    large = np.minimum(large, half - 1)
    return (np.where(rel > 0, half, 0) + np.where(n < exact, n, large)).astype(np.int32)


def _band_bias(rel_table):
    rel = np.arange(3 * BLOCK)[None, :] - BLOCK - np.arange(BLOCK)[:, None]
    bias = rel_table[_t5_bucket(rel)].astype(F32)
    bias = jnp.where((np.abs(rel) <= WINDOW)[:, :, None], bias, NEG_INF)
    bias = jnp.transpose(bias, (2, 0, 1))
    return bias.reshape(N_KV_HEADS, KV_GROUP * BLOCK, 3 * BLOCK)


def _kv_kernel(mem_ref, g_ref, w_ref, o_ref):
    h = _rms(mem_ref[0], g_ref[0]).astype(BF16)
    o_ref[0, 0] = _dot(h, w_ref[0]).astype(o_ref.dtype)


def _kv_proj(mem, g, w_kv):
    nseq = mem.shape[0]
    return pl.pallas_call(
        _kv_kernel,
        grid=(DEPTH, nseq),
        in_specs=[
            pl.BlockSpec((1, MEM_LEN, D_MODEL), lambda l, b: (b, 0, 0)),
            pl.BlockSpec((1, 1, D_MODEL), lambda l, b: (l, 0, 0)),
            pl.BlockSpec((1, D_MODEL, 2 * MEM_WIDTH), lambda l, b: (l, 0, 0)),
        ],
        out_specs=pl.BlockSpec((1, 1, MEM_LEN, 2 * MEM_WIDTH), lambda l, b: (l, b, 0, 0)),
        out_shape=jax.ShapeDtypeStruct((DEPTH, nseq, MEM_LEN, 2 * MEM_WIDTH), BF16),
        compiler_params=_params(("parallel", "parallel")),
        name="kv_proj",
    )(mem, g, w_kv)


def _mem_attn_kernel(x_ref, gpre_ref, gpost_ref, wq_ref, kv_ref, wo_ref, o_ref):
    x = x_ref[...]
    h = _rms(x, gpre_ref[...]).astype(BF16)
    q = (_dot(h, wq_ref[...]) * (HEAD_DIM ** -0.5)).astype(BF16)
    outs = []
    for hd in range(MEM_HEADS):
        sl = slice(hd * HEAD_DIM, (hd + 1) * HEAD_DIM)
        k = kv_ref[0, :, sl]
        v = kv_ref[0, :, MEM_WIDTH + hd * HEAD_DIM:MEM_WIDTH + (hd + 1) * HEAD_DIM]
        s = lax.dot_general(q[:, sl], k, (((1,), (1,)), ((), ())), preferred_element_type=F32)
        m = jnp.max(s, axis=-1, keepdims=True)
        p = jnp.exp(s - m)
        denom = jnp.sum(p, axis=-1, keepdims=True)
        outs.append((_dot(p.astype(BF16), v) / denom).astype(BF16))
    y = _dot(jnp.concatenate(outs, axis=-1), wo_ref[...])
    o_ref[...] = x + _rms(y, gpost_ref[...])


def _mem_attention(x, gpre, gpost, w_q, kv, w_o, *, tm):
    t, d = x.shape
    per_seq = SEQ // tm
    return pl.pallas_call(
        _mem_attn_kernel,
        grid=(t // tm,),
        in_specs=[
            pl.BlockSpec((tm, d), lambda i: (i, 0)),
            pl.BlockSpec((1, d), lambda i: (0, 0)),
            pl.BlockSpec((1, d), lambda i: (0, 0)),
            pl.BlockSpec((d, MEM_WIDTH), lambda i: (0, 0)),
            pl.BlockSpec((1, MEM_LEN, 2 * MEM_WIDTH), lambda i: (i // per_seq, 0, 0)),
            pl.BlockSpec((MEM_WIDTH, d), lambda i: (0, 0)),
        ],
        out_specs=pl.BlockSpec((tm, d), lambda i: (i, 0)),
        out_shape=jax.ShapeDtypeStruct((t, d), F32),
        compiler_params=_params(("parallel",)),
        name="mem_attention",
    )(x, gpre, gpost, w_q, kv, w_o)


def _ffn_kernel(x_ref, gpre_ref, gpost_ref, wg_ref, wu_ref, wd_ref, o_ref, h_scr, acc_scr, *, n_chunks):
    j = pl.program_id(1)

    @pl.when(j == 0)
    def _():
        h_scr[...] = _rms(x_ref[...], gpre_ref[...]).astype(BF16)

    h = h_scr[...]
    gate = _dot(h, wg_ref[...])
    up = _dot(h, wu_ref[...])
    a = (gate * _sigmoid(gate) * up).astype(BF16)
    p = _dot(a, wd_ref[...])

    @pl.when(j == 0)
    def _():
        acc_scr[...] = p

    @pl.when(j > 0)
    def _():
        acc_scr[...] += p

    @pl.when(j == n_chunks - 1)
    def _():
        o_ref[...] = x_ref[...] + _rms(acc_scr[...], gpost_ref[...])


def _ffn(x, gpre, gpost, w_gate_up, w_down, *, tm, tn):
    t, d = x.shape
    n_chunks = D_FF // tn
    return pl.pallas_call(
        functools.partial(_ffn_kernel, n_chunks=n_chunks),
        grid=(t // tm, n_chunks),
        in_specs=[
            pl.BlockSpec((tm, d), lambda i, j: (i, 0)),
            pl.BlockSpec((1, d), lambda i, j: (0, 0)),
            pl.BlockSpec((1, d), lambda i, j: (0, 0)),
            pl.BlockSpec((d, tn), lambda i, j: (0, j)),
            pl.BlockSpec((d, tn), lambda i, j: (0, j + n_chunks)),
            pl.BlockSpec((tn, d), lambda i, j: (j, 0)),
        ],
        out_specs=pl.BlockSpec((tm, d), lambda i, j: (i, 0)),
        out_shape=jax.ShapeDtypeStruct((t, d), F32),
        scratch_shapes=[pltpu.VMEM((tm, d), BF16), pltpu.VMEM((tm, d), F32)],
        compiler_params=_params(("parallel", "arbitrary")),
        name="ffn",
    )(x, gpre, gpost, w_gate_up, w_gate_up, w_down)


N_CT = D_RNN // LANES


def _gate_windows():
    lo_hi = []
    for j in range(N_CT):
        h0 = (LANES * j) // RNN_BDIM
        h1 = (LANES * j + LANES - 1) // RNN_BDIM
        lo = (RNN_BDIM * h0) // LANES * LANES
        hi = -(-(RNN_BDIM * (h1 + 1)) // LANES) * LANES
        lo_hi.append((lo, hi))
    width = max(hi - lo for lo, hi in lo_hi)
    starts = tuple(min(lo, D_RNN - width) for lo, _ in lo_hi)
    return starts, width


GATE_STARTS, GATE_WIDTH = _gate_windows()


def _gate_weights(w_a, w_x):
    eye = jnp.eye(RNN_BLOCKS, dtype=BF16)

    def dense(w):
        wd = w.astype(BF16)[:, :, None, :] * eye[:, None, :, None]
        return wd.reshape(D_RNN, D_RNN)

    da, dx = dense(w_a), dense(w_x)
    tiles = []
    for j, st in enumerate(GATE_STARTS):
        cols = slice(LANES * j, LANES * (j + 1))
        rows = slice(st, st + GATE_WIDTH)
        tiles.append(jnp.concatenate([da[rows, cols], dx[rows, cols]], axis=1))
    return jnp.stack(tiles)


def _scan_kernel(*refs, ts, n_tiles, nseq, reverse):
    if reverse:
        (xp_ref, xc_ref, xn_ref, cw_ref, cb_ref, wp_ref, ba_ref, bx_ref, lam_ref,
         hsf_ref, gate_ref, o_ref, conv_scr, a_scr, u_scr, h_scr) = refs
    else:
        (xp_ref, xc_ref, xn_ref, cw_ref, cb_ref, wp_ref, ba_ref, bx_ref, lam_ref,
         o_ref, conv_scr, a_scr, u_scr, h_scr) = refs
    i = pl.program_id(0)
    tile = (n_tiles - 1 - i) if reverse else i

    @pl.when(i == 0)
    def _():
        h_scr[...] = jnp.zeros_like(h_scr)
        a_scr[...] = jnp.zeros_like(a_scr)
        u_scr[...] = jnp.zeros_like(u_scr)

    cw = cw_ref[...]
    cb = cb_ref[...]
    for b in range(nseq):
        prev = jnp.where(tile == 0, 0.0, xp_ref[b].astype(F32))
        nxt = jnp.where(tile == n_tiles - 1, 0.0, xn_ref[b].astype(F32))
        ext = jnp.concatenate([prev, xc_ref[b].astype(F32), nxt], axis=0)
        acc = cb
        for k in range(CONV_WIDTH):
            off = SUBLANES - CONV_LEFT + k
            acc = acc + ext[off:off + ts] * cw[k:k + 1]
        conv_scr[b * ts:(b + 1) * ts, :] = acc

    for j in range(N_CT):
        st = GATE_STARTS[j]
        sl = slice(LANES * j, LANES * (j + 1))
        pre = _dot(conv_scr[:, st:st + GATE_WIDTH].astype(BF16), wp_ref[j])
        r = _sigmoid(pre[:, :LANES] + ba_ref[:, sl])
        gi = _sigmoid(pre[:, LANES:] + bx_ref[:, sl])
        z = -lam_ref[:, sl]
        softplus = jnp.maximum(z, 0.0) + jnp.log(1.0 + jnp.exp(-jnp.abs(z)))
        log_a = -LRU_C * r * softplus
        a = jnp.exp(log_a)
        u = jnp.sqrt(1.0 - a * a) * (gi * conv_scr[:, sl])
        for b in range(nseq):
            a_scr[j, pl.ds(b, ts, stride=SUBLANES), :] = a[b * ts:(b + 1) * ts]
            u_scr[j, pl.ds(b, ts, stride=SUBLANES), :] = u[b * ts:(b + 1) * ts]

    def step(k, h):
        t = (ts - 1 - k) if reverse else k
        row = pl.multiple_of(t * SUBLANES, SUBLANES)
        h = a_scr[:, pl.ds(row, SUBLANES), :] * h + u_scr[:, pl.ds(row, SUBLANES), :]
        u_scr[:, pl.ds(row, SUBLANES), :] = h
        return h

    h_scr[...] = lax.fori_loop(0, ts, step, h_scr[...], unroll=2)

    for b in range(nseq):
        for j in range(N_CT):
            sl = slice(LANES * j, LANES * (j + 1))
            y = u_scr[j, pl.ds(b, ts, stride=SUBLANES), :]
            if reverse:
                y = gate_ref[b, :, sl].astype(F32) * (y + hsf_ref[b, :, sl].astype(F32))
            o_ref[b, :, sl] = y.astype(o_ref.dtype)


def _rglru_scan(y_in, conv_w, conv_b, wp, b_a, b_x, lam, hs_fwd, *, ts, nseq, reverse):
    n_tiles = SEQ // ts
    halo = ts // SUBLANES

    def tmap(i):
        return (n_tiles - 1 - i) if reverse else i

    in_specs = [
        pl.BlockSpec((nseq, SUBLANES, D_RNN), lambda i: (0, jnp.maximum(tmap(i) * halo - 1, 0), 1)),
        pl.BlockSpec((nseq, ts, D_RNN), lambda i: (0, tmap(i), 1)),
        pl.BlockSpec((nseq, SUBLANES, D_RNN),
                     lambda i: (0, jnp.minimum((tmap(i) + 1) * halo, SEQ // SUBLANES - 1), 1)),
        pl.BlockSpec((CONV_WIDTH, D_RNN), lambda i: (0, 0)),
        pl.BlockSpec((1, D_RNN), lambda i: (0, 0)),
        pl.BlockSpec((N_CT, GATE_WIDTH, 2 * LANES), lambda i: (0, 0, 0)),
        pl.BlockSpec((1, D_RNN), lambda i: (0, 0)),
        pl.BlockSpec((1, D_RNN), lambda i: (0, 0)),
        pl.BlockSpec((1, D_RNN), lambda i: (0, 0)),
    ]
    args = [y_in, y_in, y_in, conv_w, conv_b, wp, b_a, b_x, lam]
    if reverse:
        in_specs += [
            pl.BlockSpec((nseq, ts, D_RNN), lambda i: (0, tmap(i), 0)),
            pl.BlockSpec((nseq, ts, D_RNN), lambda i: (0, tmap(i), 0)),
        ]
        args += [hs_fwd, y_in]
    return pl.pallas_call(
        functools.partial(_scan_kernel, ts=ts, n_tiles=n_tiles, nseq=nseq, reverse=reverse),
        grid=(n_tiles,),
        in_specs=in_specs,
        out_specs=pl.BlockSpec((nseq, ts, D_RNN), lambda i: (0, tmap(i), 0)),
        out_shape=jax.ShapeDtypeStruct((nseq, SEQ, D_RNN), BF16),
        scratch_shapes=[
            pltpu.VMEM((nseq * ts, D_RNN), F32),
            pltpu.VMEM((N_CT, ts * SUBLANES, LANES), F32),
            pltpu.VMEM((N_CT, ts * SUBLANES, LANES), F32),
            pltpu.VMEM((N_CT, SUBLANES, LANES), F32),
        ],
        compiler_params=_params(("arbitrary",)),
        name="rglru_scan_bwd" if reverse else "rglru_scan_fwd",
    )(*args)


def kernel(x_prompt, x_sample, mem_prompt, mem_sample, norms, rel_bias, a_w_in, a_ln_g, a_ln_b, a_w_s, a_b_s, a_w_out, b_w_qkv, b_sink, b_w_o, c_w_in, c_conv_w, c_conv_b, c_w_a, c_b_a, c_w_x, c_b_x, c_lam, c_w_out, m_w_q, m_w_kv, m_w_o, f_w_gate_up, f_w_down):
    n_prompt = x_prompt.shape[0]
    nseq = n_prompt + x_sample.shape[0]
    assert nseq <= SUBLANES
    t = nseq * SEQ
    x = jnp.concatenate([x_prompt, x_sample], axis=0).reshape(t, D_MODEL)
    mem = jnp.concatenate([mem_prompt, mem_sample], axis=0)

    def gain(i, k):
        return norms[i, k].reshape(1, D_MODEL)

    kv_all = _kv_proj(mem, norms[:, 4].reshape(DEPTH, 1, D_MODEL), m_w_kv.astype(BF16))

    for i in range(DEPTH):
        kind, j = i % N_MIXERS, i // N_MIXERS
        if kind == 0:
            z = _norm_matmul(x, gain(i, 0), a_w_in[j].astype(BF16), tm=1024, tn=GMLP_GDIM,
                             n_gelu=2 * GMLP_GROUPS, tile_major=True)
            x = _gmlp_out(z, a_ln_g[j].reshape(GMLP_GROUPS, 1, GMLP_GDIM),
                          a_ln_b[j].reshape(GMLP_GROUPS, 1, GMLP_GDIM),
                          a_w_s[j].astype(BF16), a_b_s[j].reshape(GMLP_GROUPS, BLOCK, 1),
                          a_w_out[j].astype(BF16), x, gain(i, 1), tm=512)
        elif kind == 1:
            qkv = _norm_matmul(x, gain(i, 0), b_w_qkv[j].astype(BF16), tm=1024, tn=1024)
            o = _win_attention(qkv, b_sink[j], _band_bias(rel_bias), nseq=nseq, r_blocks=4)
            x = _matmul_norm_res(o, b_w_o[j].astype(BF16), x, gain(i, 1), tm=512)
        else:
            y_in = _norm_matmul(x, gain(i, 0), c_w_in[j].astype(BF16), tm=1024, tn=D_RNN // 2,
                                n_gelu=2)
            y_in = y_in.reshape(nseq, SEQ, 2 * D_RNN)
            scan_args = (c_conv_w[j], c_conv_b[j].reshape(1, D_RNN))
            hs = None
            for d in range(2):
                hs = _rglru_scan(y_in, *scan_args, _gate_weights(c_w_a[j, d], c_w_x[j, d]),
                                 c_b_a[j, d].reshape(1, D_RNN), c_b_x[j, d].reshape(1, D_RNN),
                                 c_lam[j, d].reshape(1, D_RNN), hs, ts=64, nseq=nseq,
                                 reverse=(d == 1))
            x = _matmul_norm_res(hs.reshape(t, D_RNN), c_w_out[j].astype(BF16), x, gain(i, 1), tm=512)
        x = _mem_attention(x, gain(i, 2), gain(i, 3), m_w_q[i].astype(BF16), kv_all[i],
                           m_w_o[i].astype(BF16), tm=512)
        x = _ffn(x, gain(i, 5), gain(i, 6), f_w_gate_up[i].astype(BF16), f_w_down[i].astype(BF16),
                 tm=512, tn=512)

    y = x.reshape(nseq, SEQ, D_MODEL)
    return (y[:n_prompt], y[n_prompt:])
```

```python
import functools

import numpy as np
import jax
import jax.numpy as jnp
from jax import lax
from jax.experimental import pallas as pl
from jax.experimental.pallas import tpu as pltpu

D_MODEL = 2048
SEQ = 4096
DEPTH = 4
N_MIXERS = 3
BLOCK = 128
HEAD_DIM = 128
EPS = 1e-6
NEG_INF = -1e30
GMLP_HALF = 2 * D_MODEL
GMLP_GROUPS = 8
GMLP_GDIM = GMLP_HALF // GMLP_GROUPS
N_HEADS = D_MODEL // HEAD_DIM
N_KV_HEADS = 4
KV_GROUP = N_HEADS // N_KV_HEADS
WINDOW = 128
N_BUCKETS = 32
MAX_DISTANCE = 128
D_RNN = 2816
RNN_BLOCKS = 16
RNN_BDIM = D_RNN // RNN_BLOCKS
CONV_WIDTH = 4
CONV_LEFT = 2
LRU_C = 8.0
MEM_LEN = 256
MEM_HEADS = 4
MEM_WIDTH = MEM_HEADS * HEAD_DIM
D_FF = 5632

LANES = 128
SUBLANES = 8
VMEM_LIMIT = 56 * 1024 * 1024

F32 = jnp.float32
BF16 = jnp.bfloat16


def _params(sem):
    return pltpu.CompilerParams(dimension_semantics=sem, vmem_limit_bytes=VMEM_LIMIT)


def _rms(x, g):
    ms = jnp.mean(x * x, axis=-1, keepdims=True)
    return x * lax.rsqrt(ms + EPS) * g


def _gelu(x):
    c = np.sqrt(2.0 / np.pi).astype(np.float32)
    return 0.5 * x * (1.0 + jnp.tanh(c * (x + 0.044715 * (x * x * x))))


def _sigmoid(x):
    return 1.0 / (1.0 + jnp.exp(-x))


def _dot(a, b):
    return jnp.dot(a, b, preferred_element_type=F32)


def _norm_mm_kernel(x_ref, g_ref, w_ref, o_ref, h_scr, *, n_gelu, n_tiles, tile_major):
    j = pl.program_id(1)

    @pl.when(j == 0)
    def _():
        h_scr[...] = _rms(x_ref[...], g_ref[...]).astype(BF16)

    def emit(act):
        z = _dot(h_scr[...], w_ref[...])
        if act:
            z = _gelu(z)
        if tile_major:
            o_ref[0] = z.astype(o_ref.dtype)
        else:
            o_ref[...] = z.astype(o_ref.dtype)

    if n_gelu == 0:
        emit(False)
    elif n_gelu == n_tiles:
        emit(True)
    else:
        pl.when(j < n_gelu)(lambda: emit(True))
        pl.when(j >= n_gelu)(lambda: emit(False))


def _norm_matmul(x, g, w, layer, *, tm, tn, n_gelu=0, tile_major=False):
    t, d = x.shape
    n = w.shape[2]
    n_tiles = n // tn
    if tile_major:
        out_shape = jax.ShapeDtypeStruct((n_tiles, t, tn), BF16)
        out_spec = pl.BlockSpec((1, tm, tn), lambda i, j: (j, i, 0))
    else:
        out_shape = jax.ShapeDtypeStruct((t, n), BF16)
        out_spec = pl.BlockSpec((tm, tn), lambda i, j: (i, j))
    return pl.pallas_call(
        functools.partial(_norm_mm_kernel, n_gelu=n_gelu, n_tiles=n_tiles, tile_major=tile_major),
        grid=(t // tm, n_tiles),
        in_specs=[
            pl.BlockSpec((tm, d), lambda i, j: (i, 0)),
            pl.BlockSpec((1, d), lambda i, j: (0, 0)),
            pl.BlockSpec((None, d, tn), lambda i, j: (layer, 0, j)),
        ],
        out_specs=out_spec,
        out_shape=out_shape,
        scratch_shapes=[pltpu.VMEM((tm, d), BF16)],
        compiler_params=_params(("parallel", "arbitrary")),
        name="norm_matmul",
    )(x, g, w)


def _mm_norm_res_kernel(a_ref, w_ref, x_ref, g_ref, o_ref):
    y = _dot(a_ref[...], w_ref[...])
    o_ref[...] = x_ref[...] + _rms(y, g_ref[...])


def _matmul_norm_res(a, w, layer, x, g, *, tm):
    t, k = a.shape
    d = w.shape[2]
    return pl.pallas_call(
        _mm_norm_res_kernel,
        grid=(t // tm,),
        in_specs=[
            pl.BlockSpec((tm, k), lambda i: (i, 0)),
            pl.BlockSpec((None, k, d), lambda i: (layer, 0, 0), pipeline_mode=pl.Buffered(1)),
            pl.BlockSpec((tm, d), lambda i: (i, 0)),
            pl.BlockSpec((1, d), lambda i: (0, 0)),
        ],
        out_specs=pl.BlockSpec((tm, d), lambda i: (i, 0)),
        out_shape=jax.ShapeDtypeStruct((t, d), F32),
        compiler_params=_params(("parallel",)),
        name="matmul_norm_res",
    )(a, w, x, g)


def _gmlp_out_kernel(u_ref, v_ref, lng_ref, lnb_ref, ws_ref, bs_ref, wo_ref, x_ref, g_ref,
                     o_ref, mu_scr, rs_scr, *, tm):
    grp = pl.program_id(1)

    @pl.when(grp == 0)
    def _():
        s = jnp.zeros((tm, 1), F32)
        for c in range(GMLP_GROUPS):
            s = s + jnp.sum(v_ref[c].astype(F32), axis=-1, keepdims=True)
        mu = s * (1.0 / GMLP_HALF)
        q = jnp.zeros((tm, 1), F32)
        for c in range(GMLP_GROUPS):
            dv = v_ref[c].astype(F32) - mu
            q = q + jnp.sum(dv * dv, axis=-1, keepdims=True)
        mu_scr[...] = mu
        rs_scr[...] = lax.rsqrt(q * (1.0 / GMLP_HALF) + EPS)
        o_ref[...] = jnp.zeros_like(o_ref)

    vg = v_ref[grp].astype(F32)
    vn = ((vg - mu_scr[...]) * rs_scr[...] * lng_ref[0] + lnb_ref[0]).astype(BF16)
    ws = ws_ref[0]
    bs = bs_ref[0]
    parts = []
    for n in range(tm // BLOCK):
        rows = slice(n * BLOCK, (n + 1) * BLOCK)
        mixed = _dot(ws, vn[rows, :]) + bs
        parts.append((u_ref[0, rows, :].astype(F32) * mixed).astype(BF16))
    o_ref[...] += _dot(jnp.concatenate(parts, axis=0), wo_ref[...])

    @pl.when(grp == GMLP_GROUPS - 1)
    def _():
        o_ref[...] = x_ref[...] + _rms(o_ref[...], g_ref[...])


def _gmlp_out(z, ln_g, ln_b, w_s, b_s, w_out, layer, x, g, *, tm):
    t, d = x.shape
    return pl.pallas_call(
        functools.partial(_gmlp_out_kernel, tm=tm),
        grid=(t // tm, GMLP_GROUPS),
        in_specs=[
            pl.BlockSpec((1, tm, GMLP_GDIM), lambda i, c: (c, i, 0)),
            pl.BlockSpec((GMLP_GROUPS, tm, GMLP_GDIM), lambda i, c: (1, i, 0)),
            pl.BlockSpec((1, 1, GMLP_GDIM), lambda i, c: (c, 0, 0)),
            pl.BlockSpec((1, 1, GMLP_GDIM), lambda i, c: (c, 0, 0)),
            pl.BlockSpec((1, BLOCK, BLOCK), lambda i, c: (c, 0, 0)),
            pl.BlockSpec((1, BLOCK, 1), lambda i, c: (c, 0, 0)),
            pl.BlockSpec((None, GMLP_GDIM, d), lambda i, c: (layer, c, 0)),
            pl.BlockSpec((tm, d), lambda i, c: (i, 0)),
            pl.BlockSpec((1, d), lambda i, c: (0, 0)),
        ],
        out_specs=pl.BlockSpec((tm, d), lambda i, c: (i, 0)),
        out_shape=jax.ShapeDtypeStruct((t, d), F32),
        scratch_shapes=[
            pltpu.VMEM((tm, 1), F32),
            pltpu.VMEM((tm, 1), F32),
        ],
        compiler_params=_params(("parallel", "arbitrary")),
        name="gmlp_out",
    )(z, z, ln_g, ln_b, w_s, b_s, w_out, x, g)


def _win_attn_kernel(sink_ref, q_ref, kp_ref, kc_ref, kn_ref, vp_ref, vc_ref, vn_ref, bias_ref,
                     o_ref, *, r_blocks, n_steps):
    i = pl.program_id(1)
    kall = jnp.concatenate([kp_ref[0], kc_ref[0], kn_ref[0]], axis=0)
    vall = jnp.concatenate([vp_ref[0], vc_ref[0], vn_ref[0]], axis=0)
    col = lax.broadcasted_iota(jnp.int32, (1, 3 * BLOCK), 1)
    sk = jnp.concatenate(
        [jnp.full((BLOCK, 1), sink_ref[h], F32) for h in range(N_HEADS)], axis=0)
    hrows = KV_GROUP * BLOCK
    for r in range(r_blocks):
        rows = slice(r * BLOCK, (r + 1) * BLOCK)
        oob = None
        if r == 0:
            oob = (col < BLOCK) & (i == 0)
        if r == r_blocks - 1:
            hi = (col >= 2 * BLOCK) & (i == n_steps - 1)
            oob = hi if oob is None else (oob | hi)
        parts = []
        for kh in range(N_KV_HEADS):
            kb = kall[r * BLOCK:(r + 3) * BLOCK, kh * HEAD_DIM:(kh + 1) * HEAD_DIM]
            qs = jnp.concatenate(
                [q_ref[0, rows, h * HEAD_DIM:(h + 1) * HEAD_DIM]
                 for h in range(kh * KV_GROUP, (kh + 1) * KV_GROUP)], axis=0)
            parts.append(
                lax.dot_general(qs, kb, (((1,), (1,)), ((), ())), preferred_element_type=F32))
        s = jnp.concatenate(parts, axis=0) + bias_ref[...]
        if oob is not None:
            s = jnp.where(oob, NEG_INF, s)
        m = jnp.maximum(jnp.max(s, axis=-1, keepdims=True), sk)
        p = jnp.exp2(s - m)
        inv = 1.0 / (jnp.sum(p, axis=-1, keepdims=True) + jnp.exp2(sk - m))
        pb = p.astype(BF16)
        for kh in range(N_KV_HEADS):
            vb = vall[r * BLOCK:(r + 3) * BLOCK, kh * HEAD_DIM:(kh + 1) * HEAD_DIM]
            hsl = slice(kh * hrows, (kh + 1) * hrows)
            o = _dot(pb[hsl], vb) * inv[hsl]
            for gq in range(KV_GROUP):
                h = kh * KV_GROUP + gq
                o_ref[0, rows, h * HEAD_DIM:(h + 1) * HEAD_DIM] = (
                    o[gq * BLOCK:(gq + 1) * BLOCK].astype(o_ref.dtype))


def _win_attention(qkv, sink, bias, *, nseq, r_blocks):
    tq = r_blocks * BLOCK
    n_steps = SEQ // tq
    nb = SEQ // BLOCK
    qkv3 = qkv.reshape(nseq, SEQ, qkv.shape[-1])
    kcol = N_HEADS * HEAD_DIM // (N_KV_HEADS * HEAD_DIM)
    vcol = kcol + 1
    kvw = N_KV_HEADS * HEAD_DIM

    def prev_map(c):
        return lambda b, i: (b, jnp.maximum(i * r_blocks - 1, 0), c)

    def cur_map(c):
        return lambda b, i: (b, i, c)

    def next_map(c):
        return lambda b, i: (b, jnp.minimum((i + 1) * r_blocks, nb - 1), c)

    out = pl.pallas_call(
        functools.partial(_win_attn_kernel, r_blocks=r_blocks, n_steps=n_steps),
        grid=(nseq, n_steps),
        in_specs=[
            pl.BlockSpec(memory_space=pltpu.SMEM),
            pl.BlockSpec((1, tq, N_HEADS * HEAD_DIM), lambda b, i: (b, i, 0)),
            pl.BlockSpec((1, BLOCK, kvw), prev_map(kcol)),
            pl.BlockSpec((1, tq, kvw), cur_map(kcol)),
            pl.BlockSpec((1, BLOCK, kvw), next_map(kcol)),
            pl.BlockSpec((1, BLOCK, kvw), prev_map(vcol)),
            pl.BlockSpec((1, tq, kvw), cur_map(vcol)),
            pl.BlockSpec((1, BLOCK, kvw), next_map(vcol)),
            pl.BlockSpec((N_HEADS * BLOCK, 3 * BLOCK), lambda b, i: (0, 0)),
        ],
        out_specs=pl.BlockSpec((1, tq, N_HEADS * HEAD_DIM), lambda b, i: (b, i, 0)),
        out_shape=jax.ShapeDtypeStruct((nseq, SEQ, N_HEADS * HEAD_DIM), BF16),
        compiler_params=_params(("parallel", "parallel")),
        name="win_attention",
    )(sink, qkv3, qkv3, qkv3, qkv3, qkv3, qkv3, qkv3, bias)
    return out.reshape(nseq * SEQ, N_HEADS * HEAD_DIM)


def _t5_bucket(rel):
    half = N_BUCKETS // 2
    exact = half // 2
    n = np.abs(rel)
    large = exact + (np.log(np.maximum(n, 1) / exact) / np.log(MAX_DISTANCE / exact)
                     * (half - exact)).astype(np.int32)
    large = np.minimum(large, half - 1)
    return (np.where(rel > 0, half, 0) + np.where(n < exact, n, large)).astype(np.int32)


LOG2E = float(np.log2(np.e))


def _band_bias(rel_table):
    rel = np.arange(3 * BLOCK)[None, :] - BLOCK - np.arange(BLOCK)[:, None]
    onehot = (_t5_bucket(rel).reshape(1, -1) == np.arange(N_BUCKETS)[:, None]).astype(np.float32)
    bias = jnp.einsum("bh,bn->hn", rel_table.astype(F32) * LOG2E, onehot,
                      precision=lax.Precision.HIGHEST)
    bias = jnp.where((np.abs(rel) <= WINDOW).reshape(1, -1), bias, NEG_INF)
    return bias.reshape(N_HEADS * BLOCK, 3 * BLOCK)


def _kv_kernel(mem_ref, g_ref, w_ref, o_ref):
    h = _rms(mem_ref[0], g_ref[0]).astype(BF16)
    o_ref[0, 0] = _dot(h, w_ref[0]).astype(o_ref.dtype)


def _kv_proj(mem, g, w_kv):
    nseq = mem.shape[0]
    return pl.pallas_call(
        _kv_kernel,
        grid=(DEPTH, nseq),
        in_specs=[
            pl.BlockSpec((1, MEM_LEN, D_MODEL), lambda l, b: (b, 0, 0)),
            pl.BlockSpec((1, 1, D_MODEL), lambda l, b: (l, 0, 0)),
            pl.BlockSpec((1, D_MODEL, 2 * MEM_WIDTH), lambda l, b: (l, 0, 0)),
        ],
        out_specs=pl.BlockSpec((1, 1, MEM_LEN, 2 * MEM_WIDTH), lambda l, b: (l, b, 0, 0)),
        out_shape=jax.ShapeDtypeStruct((DEPTH, nseq, MEM_LEN, 2 * MEM_WIDTH), BF16),
        compiler_params=_params(("parallel", "parallel")),
        name="kv_proj",
    )(mem, g, w_kv)


def _mem_attn_kernel(x_ref, gpre_ref, gpost_ref, wq_ref, kv_ref, wo_ref, o_ref):
    x = x_ref[...]
    h = _rms(x, gpre_ref[...]).astype(BF16)
    q = (_dot(h, wq_ref[...]) * (HEAD_DIM ** -0.5)).astype(BF16)
    outs = []
    for hd in range(MEM_HEADS):
        sl = slice(hd * HEAD_DIM, (hd + 1) * HEAD_DIM)
        k = kv_ref[0, :, sl]
        v = kv_ref[0, :, MEM_WIDTH + hd * HEAD_DIM:MEM_WIDTH + (hd + 1) * HEAD_DIM]
        s = lax.dot_general(q[:, sl], k, (((1,), (1,)), ((), ())), preferred_element_type=F32)
        m = jnp.max(s, axis=-1, keepdims=True)
        p = jnp.exp(s - m)
        denom = jnp.sum(p, axis=-1, keepdims=True)
        outs.append((_dot(p.astype(BF16), v) / denom).astype(BF16))
    y = _dot(jnp.concatenate(outs, axis=-1), wo_ref[...])
    o_ref[...] = x + _rms(y, gpost_ref[...])


def _mem_attention(x, gpre, gpost, w_q, kv, w_o, layer, *, tm):
    t, d = x.shape
    per_seq = SEQ // tm
    return pl.pallas_call(
        _mem_attn_kernel,
        grid=(t // tm,),
        in_specs=[
            pl.BlockSpec((tm, d), lambda i: (i, 0)),
            pl.BlockSpec((1, d), lambda i: (0, 0)),
            pl.BlockSpec((1, d), lambda i: (0, 0)),
            pl.BlockSpec((None, d, MEM_WIDTH), lambda i: (layer, 0, 0)),
            pl.BlockSpec((None, 1, MEM_LEN, 2 * MEM_WIDTH), lambda i: (layer, i // per_seq, 0, 0)),
            pl.BlockSpec((None, MEM_WIDTH, d), lambda i: (layer, 0, 0)),
        ],
        out_specs=pl.BlockSpec((tm, d), lambda i: (i, 0)),
        out_shape=jax.ShapeDtypeStruct((t, d), F32),
        compiler_params=_params(("parallel",)),
        name="mem_attention",
    )(x, gpre, gpost, w_q, kv, w_o)


def _ffn_kernel(x_ref, gpre_ref, gpost_ref, wg_ref, wu_ref, wd_ref, o_ref, h_scr, *, n_chunks):
    j = pl.program_id(1)

    @pl.when(j == 0)
    def _():
        h_scr[...] = _rms(x_ref[...], gpre_ref[...]).astype(BF16)
        o_ref[...] = jnp.zeros_like(o_ref)

    h = h_scr[...]
    gate = _dot(h, wg_ref[...])
    up = _dot(h, wu_ref[...])
    a = (gate * _sigmoid(gate) * up).astype(BF16)
    o_ref[...] += _dot(a, wd_ref[...])

    @pl.when(j == n_chunks - 1)
    def _():
        o_ref[...] = x_ref[...] + _rms(o_ref[...], gpost_ref[...])


def _ffn(x, gpre, gpost, w_gate_up, w_down, layer, *, tm, tn):
    t, d = x.shape
    n_chunks = D_FF // tn
    return pl.pallas_call(
        functools.partial(_ffn_kernel, n_chunks=n_chunks),
        grid=(t // tm, n_chunks),
        in_specs=[
            pl.BlockSpec((tm, d), lambda i, j: (i, 0), pipeline_mode=pl.Buffered(1)),
            pl.BlockSpec((1, d), lambda i, j: (0, 0)),
            pl.BlockSpec((1, d), lambda i, j: (0, 0)),
            pl.BlockSpec((None, d, tn), lambda i, j: (layer, 0, j)),
            pl.BlockSpec((None, d, tn), lambda i, j: (layer, 0, j + n_chunks)),
            pl.BlockSpec((None, tn, d), lambda i, j: (layer, j, 0)),
        ],
        out_specs=pl.BlockSpec((tm, d), lambda i, j: (i, 0)),
        out_shape=jax.ShapeDtypeStruct((t, d), F32),
        scratch_shapes=[pltpu.VMEM((tm, d), BF16)],
        compiler_params=_params(("parallel", "arbitrary")),
        name="ffn",
    )(x, gpre, gpost, w_gate_up, w_gate_up, w_down)


N_CT = D_RNN // LANES


def _gate_windows():
    lo_hi = []
    for j in range(N_CT):
        h0 = (LANES * j) // RNN_BDIM
        h1 = (LANES * j + LANES - 1) // RNN_BDIM
        lo = (RNN_BDIM * h0) // LANES * LANES
        hi = -(-(RNN_BDIM * (h1 + 1)) // LANES) * LANES
        lo_hi.append((lo, hi))
    width = max(hi - lo for lo, hi in lo_hi)
    starts = tuple(min(lo, D_RNN - width) for lo, _ in lo_hi)
    return starts, width


GATE_STARTS, GATE_WIDTH = _gate_windows()


def _gate_weights(w_a, w_x):
    def tile(w, j):
        st = GATE_STARTS[j]
        pieces = []
        for h in range((LANES * j) // RNN_BDIM, (LANES * j + LANES - 1) // RNN_BDIM + 1):
            c0 = max(LANES * j, RNN_BDIM * h) - RNN_BDIM * h
            c1 = min(LANES * (j + 1), RNN_BDIM * (h + 1)) - RNN_BDIM * h
            top = RNN_BDIM * h - st
            pieces.append(jnp.pad(w[h, :, c0:c1], ((top, GATE_WIDTH - RNN_BDIM - top), (0, 0))))
        return jnp.concatenate(pieces, axis=1)

    wa, wx = w_a.astype(BF16), w_x.astype(BF16)
    return jnp.stack([jnp.concatenate([tile(wa, j), tile(wx, j)], axis=1) for j in range(N_CT)])


def _scan_kernel(*refs, ts, n_tiles, nseq, reverse):
    if reverse:
        (xp_ref, xc_ref, xn_ref, cw_ref, cb_ref, wp_ref, ba_ref, bx_ref, lam_ref,
         hsf_ref, gate_ref, o_ref, conv_scr, convb_scr, a_scr, u_scr, h_scr) = refs
    else:
        (xp_ref, xc_ref, xn_ref, cw_ref, cb_ref, wp_ref, ba_ref, bx_ref, lam_ref,
         o_ref, conv_scr, convb_scr, a_scr, u_scr, h_scr) = refs
    i = pl.program_id(0)
    tile = (n_tiles - 1 - i) if reverse else i

    @pl.when(i == 0)
    def _():
        h_scr[...] = jnp.zeros_like(h_scr)
        a_scr[...] = jnp.zeros_like(a_scr)
        u_scr[...] = jnp.zeros_like(u_scr)

    cw = cw_ref[...]
    cb = cb_ref[...]
    for b in range(nseq):
        prev = jnp.where(tile == 0, 0.0, xp_ref[b].astype(F32))
        nxt = jnp.where(tile == n_tiles - 1, 0.0, xn_ref[b].astype(F32))
        ext = jnp.concatenate([prev, xc_ref[b].astype(F32), nxt], axis=0)
        acc = cb
        for k in range(CONV_WIDTH):
            off = SUBLANES - CONV_LEFT + k
            acc = acc + ext[off:off + ts] * cw[k:k + 1]
        conv_scr[b * ts:(b + 1) * ts, :] = acc
        convb_scr[b * ts:(b + 1) * ts, :] = acc.astype(BF16)

    for j in range(N_CT):
        st = GATE_STARTS[j]
        sl = slice(LANES * j, LANES * (j + 1))
        pre = _dot(convb_scr[:, st:st + GATE_WIDTH], wp_ref[j])
        r = _sigmoid(pre[:, :LANES] + ba_ref[:, sl])
        gi = _sigmoid(pre[:, LANES:] + bx_ref[:, sl])
        z = -lam_ref[:, sl]
        softplus = jnp.maximum(z, 0.0) + jnp.log(1.0 + jnp.exp(-jnp.abs(z)))
        log_a = -LRU_C * r * softplus
        a = jnp.exp(log_a)
        u = jnp.sqrt(1.0 - a * a) * (gi * conv_scr[:, sl])
        for b in range(nseq):
            a_scr[j, pl.ds(b, ts, stride=SUBLANES), :] = a[b * ts:(b + 1) * ts]
            u_scr[j, pl.ds(b, ts, stride=SUBLANES), :] = u[b * ts:(b + 1) * ts]

    def step(k, h):
        t = (ts - 1 - k) if reverse else k
        row = pl.multiple_of(t * SUBLANES, SUBLANES)
        h = a_scr[:, pl.ds(row, SUBLANES), :] * h + u_scr[:, pl.ds(row, SUBLANES), :]
        u_scr[:, pl.ds(row, SUBLANES), :] = h
        return h

    h_scr[...] = lax.fori_loop(0, ts, step, h_scr[...], unroll=2)

    for b in range(nseq):
        for j in range(N_CT):
            sl = slice(LANES * j, LANES * (j + 1))
            y = u_scr[j, pl.ds(b, ts, stride=SUBLANES), :]
            if reverse:
                y = gate_ref[b, :, sl].astype(F32) * (y + hsf_ref[b, :, sl].astype(F32))
            o_ref[b, :, sl] = y.astype(o_ref.dtype)


def _rglru_scan(y_in, conv_w, conv_b, wp, b_a, b_x, lam, hs_fwd, *, ts, nseq, reverse):
    n_tiles = SEQ // ts
    halo = ts // SUBLANES

    def tmap(i):
        return (n_tiles - 1 - i) if reverse else i

    in_specs = [
        pl.BlockSpec((nseq, SUBLANES, D_RNN), lambda i: (0, jnp.maximum(tmap(i) * halo - 1, 0), 1)),
        pl.BlockSpec((nseq, ts, D_RNN), lambda i: (0, tmap(i), 1)),
        pl.BlockSpec((nseq, SUBLANES, D_RNN),
                     lambda i: (0, jnp.minimum((tmap(i) + 1) * halo, SEQ // SUBLANES - 1), 1)),
        pl.BlockSpec((CONV_WIDTH, D_RNN), lambda i: (0, 0)),
        pl.BlockSpec((1, D_RNN), lambda i: (0, 0)),
        pl.BlockSpec((N_CT, GATE_WIDTH, 2 * LANES), lambda i: (0, 0, 0)),
        pl.BlockSpec((1, D_RNN), lambda i: (0, 0)),
        pl.BlockSpec((1, D_RNN), lambda i: (0, 0)),
        pl.BlockSpec((1, D_RNN), lambda i: (0, 0)),
    ]
    args = [y_in, y_in, y_in, conv_w, conv_b, wp, b_a, b_x, lam]
    if reverse:
        in_specs += [
            pl.BlockSpec((nseq, ts, D_RNN), lambda i: (0, tmap(i), 0)),
            pl.BlockSpec((nseq, ts, D_RNN), lambda i: (0, tmap(i), 0)),
        ]
        args += [hs_fwd, y_in]
    return pl.pallas_call(
        functools.partial(_scan_kernel, ts=ts, n_tiles=n_tiles, nseq=nseq, reverse=reverse),
        grid=(n_tiles,),
        in_specs=in_specs,
        out_specs=pl.BlockSpec((nseq, ts, D_RNN), lambda i: (0, tmap(i), 0)),
        out_shape=jax.ShapeDtypeStruct((nseq, SEQ, D_RNN), BF16),
        scratch_shapes=[
            pltpu.VMEM((nseq * ts, D_RNN), F32),
            pltpu.VMEM((nseq * ts, D_RNN), BF16),
            pltpu.VMEM((N_CT, ts * SUBLANES, LANES), F32),
            pltpu.VMEM((N_CT, ts * SUBLANES, LANES), F32),
            pltpu.VMEM((N_CT, SUBLANES, LANES), F32),
        ],
        compiler_params=_params(("arbitrary",)),
        name="rglru_scan_bwd" if reverse else "rglru_scan_fwd",
    )(*args)


def kernel(x_prompt, x_sample, mem_prompt, mem_sample, norms, rel_bias, a_w_in, a_ln_g, a_ln_b, a_w_s, a_b_s, a_w_out, b_w_qkv, b_sink, b_w_o, c_w_in, c_conv_w, c_conv_b, c_w_a, c_b_a, c_w_x, c_b_x, c_lam, c_w_out, m_w_q, m_w_kv, m_w_o, f_w_gate_up, f_w_down):
    n_prompt = x_prompt.shape[0]
    nseq = n_prompt + x_sample.shape[0]
    assert nseq <= SUBLANES
    t = nseq * SEQ
    x = jnp.concatenate([x_prompt, x_sample], axis=0).reshape(t, D_MODEL)
    mem = jnp.concatenate([mem_prompt, mem_sample], axis=0)

    def gain(i, k):
        return norms[i, k].reshape(1, D_MODEL)

    a_w_in_b, a_w_out_b = a_w_in.astype(BF16), a_w_out.astype(BF16)
    q_cols = N_HEADS * HEAD_DIM
    b_w_qkv_b = jnp.concatenate(
        [b_w_qkv[:, :, :q_cols] * (HEAD_DIM ** -0.5 * LOG2E), b_w_qkv[:, :, q_cols:]],
        axis=-1).astype(BF16)
    b_w_o_b = b_w_o.astype(BF16)
    c_w_in_b, c_w_out_b = c_w_in.astype(BF16), c_w_out.astype(BF16)
    m_w_q_b, m_w_o_b = m_w_q.astype(BF16), m_w_o.astype(BF16)
    f_w_gate_up_b, f_w_down_b = f_w_gate_up.astype(BF16), f_w_down.astype(BF16)

    kv_all = _kv_proj(mem, norms[:, 4].reshape(DEPTH, 1, D_MODEL), m_w_kv.astype(BF16))

    for i in range(DEPTH):
        kind, j = i % N_MIXERS, i // N_MIXERS
        if kind == 0:
            z = _norm_matmul(x, gain(i, 0), a_w_in_b, j, tm=1024, tn=GMLP_GDIM,
                             n_gelu=2 * GMLP_GROUPS, tile_major=True)
            x = _gmlp_out(z, a_ln_g[j].reshape(GMLP_GROUPS, 1, GMLP_GDIM),
                          a_ln_b[j].reshape(GMLP_GROUPS, 1, GMLP_GDIM),
                          a_w_s[j].astype(BF16), a_b_s[j].reshape(GMLP_GROUPS, BLOCK, 1),
                          a_w_out_b, j, x, gain(i, 1), tm=512)
        elif kind == 1:
            qkv = _norm_matmul(x, gain(i, 0), b_w_qkv_b, j, tm=1024, tn=1024)
            o = _win_attention(qkv, b_sink[j] * LOG2E, _band_bias(rel_bias), nseq=nseq, r_blocks=4)
            x = _matmul_norm_res(o, b_w_o_b, j, x, gain(i, 1), tm=512)
        else:
            y_in = _norm_matmul(x, gain(i, 0), c_w_in_b, j, tm=1024, tn=D_RNN // 2, n_gelu=2)
            y_in = y_in.reshape(nseq, SEQ, 2 * D_RNN)
            scan_args = (c_conv_w[j], c_conv_b[j].reshape(1, D_RNN))
            hs = None
            for d in range(2):
                hs = _rglru_scan(y_in, *scan_args, _gate_weights(c_w_a[j, d], c_w_x[j, d]),
                                 c_b_a[j, d].reshape(1, D_RNN), c_b_x[j, d].reshape(1, D_RNN),
                                 c_lam[j, d].reshape(1, D_RNN), hs, ts=64, nseq=nseq,
                                 reverse=(d == 1))
            x = _matmul_norm_res(hs.reshape(t, D_RNN), c_w_out_b, j, x, gain(i, 1), tm=512)
        x = _mem_attention(x, gain(i, 2), gain(i, 3), m_w_q_b, kv_all, m_w_o_b, i, tm=512)
        x = _ffn(x, gain(i, 5), gain(i, 6), f_w_gate_up_b, f_w_down_b, i, tm=1024, tn=512)

    y = x.reshape(nseq, SEQ, D_MODEL)
    return (y[:n_prompt], y[n_prompt:])
```

```python
import functools

import numpy as np
import jax
import jax.numpy as jnp
from jax import lax
from jax.experimental import pallas as pl
from jax.experimental.pallas import tpu as pltpu

D_MODEL = 2048
SEQ = 4096
DEPTH = 4
N_MIXERS = 3
BLOCK = 128
HEAD_DIM = 128
EPS = 1e-6
NEG_INF = -1e30
GMLP_HALF = 2 * D_MODEL
GMLP_GROUPS = 8
GMLP_GDIM = GMLP_HALF // GMLP_GROUPS
N_HEADS = D_MODEL // HEAD_DIM
N_KV_HEADS = 4
KV_GROUP = N_HEADS // N_KV_HEADS
WINDOW = 128
N_BUCKETS = 32
MAX_DISTANCE = 128
D_RNN = 2816
RNN_BLOCKS = 16
RNN_BDIM = D_RNN // RNN_BLOCKS
CONV_WIDTH = 4
CONV_LEFT = 2
LRU_C = 8.0
MEM_LEN = 256
MEM_HEADS = 4
MEM_WIDTH = MEM_HEADS * HEAD_DIM
D_FF = 5632

LANES = 128
SUBLANES = 8
VMEM_LIMIT = 56 * 1024 * 1024

F32 = jnp.float32
BF16 = jnp.bfloat16
LOG2E = float(np.log2(np.e))


def _params(sem):
    return pltpu.CompilerParams(dimension_semantics=sem, vmem_limit_bytes=VMEM_LIMIT)


def _rms(x, g):
    ms = jnp.mean(x * x, axis=-1, keepdims=True)
    return x * lax.rsqrt(ms + EPS) * g


def _gelu(x):
    c = np.sqrt(2.0 / np.pi).astype(np.float32)
    return 0.5 * x * (1.0 + jnp.tanh(c * (x + 0.044715 * (x * x * x))))


def _sigmoid(x):
    return 1.0 / (1.0 + jnp.exp(-x))


def _dot(a, b):
    return jnp.dot(a, b, preferred_element_type=F32)


def _norm_mm_kernel(x_ref, g_ref, w_ref, o_ref, h_scr, *, n_gelu):
    j = pl.program_id(1)

    @pl.when(j == 0)
    def _():
        h_scr[...] = _rms(x_ref[...], g_ref[...]).astype(BF16)

    def emit(act):
        z = _dot(h_scr[...], w_ref[...])
        o_ref[...] = (_gelu(z) if act else z).astype(o_ref.dtype)

    if n_gelu == 0:
        emit(False)
    else:
        pl.when(j < n_gelu)(lambda: emit(True))
        pl.when(j >= n_gelu)(lambda: emit(False))


def _norm_matmul(x, g, w, layer, *, tm, tn, n_gelu=0):
    t, d = x.shape
    n = w.shape[2]
    return pl.pallas_call(
        functools.partial(_norm_mm_kernel, n_gelu=n_gelu),
        grid=(t // tm, n // tn),
        in_specs=[
            pl.BlockSpec((tm, d), lambda i, j: (i, 0)),
            pl.BlockSpec((1, d), lambda i, j: (0, 0)),
            pl.BlockSpec((None, d, tn), lambda i, j: (layer, 0, j)),
        ],
        out_specs=pl.BlockSpec((tm, tn), lambda i, j: (i, j)),
        out_shape=jax.ShapeDtypeStruct((t, n), BF16),
        scratch_shapes=[pltpu.VMEM((tm, d), BF16)],
        compiler_params=_params(("parallel", "arbitrary")),
        name="norm_matmul",
    )(x, g, w)


def _part_specs(parts, tm, tile_of):
    specs, ranges, lo = [], [], 0
    mode = pl.Buffered(1) if len(parts) > 1 else None
    for p in parts:
        n = p.shape[0] // tm
        specs.append(pl.BlockSpec(
            (tm, p.shape[1]),
            lambda *idx, lo=lo, n=n: (jnp.clip(tile_of(*idx) - lo, 0, n - 1), 0),
            pipeline_mode=mode))
        ranges.append((lo, lo + n))
        lo += n
    return specs, tuple(ranges)


def _for_owner(i, ranges, refs, cond, fn):
    for (lo, hi), ref in zip(ranges, refs):
        pl.when(cond & (i >= lo) & (i < hi))(functools.partial(fn, ref))


MXU_COLS = 256


def _gmlp_in_kernel(*refs, ranges, n_col, tn):
    n_parts = len(ranges)
    x_refs = refs[:n_parts]
    g_ref, w_ref, z_ref, mu_ref, rs_ref, h_scr, s1_scr, s2_scr = refs[n_parts:]
    i, j = pl.program_id(0), pl.program_id(1)
    tm = h_scr.shape[0]

    def prologue(x_ref):
        h_scr[...] = _rms(x_ref[...], g_ref[...]).astype(BF16)
        s1_scr[...] = jnp.zeros_like(s1_scr)
        s2_scr[...] = jnp.zeros_like(s2_scr)

    _for_owner(i, ranges, x_refs, j == 0, prologue)

    p1 = jnp.zeros((tm, LANES), F32)
    p2 = jnp.zeros((tm, LANES), F32)
    per_tile = GMLP_GDIM // MXU_COLS
    for c in range(tn // MXU_COLS):
        cols = slice(c * MXU_COLS, (c + 1) * MXU_COLS)
        zc = _gelu(_dot(h_scr[...], w_ref[:, cols]))
        lo = (c % per_tile) * MXU_COLS
        z_ref[c // per_tile, :, lo:lo + MXU_COLS] = zc.astype(z_ref.dtype)
        for k in range(MXU_COLS // LANES):
            zk = zc[:, k * LANES:(k + 1) * LANES]
            p1 = p1 + zk
            p2 = p2 + zk * zk

    first_v = j == n_col // 2
    s1 = jnp.where(first_v, p1, s1_scr[...] + p1)
    s2 = jnp.where(first_v, p2, s2_scr[...] + p2)
    s1_scr[...] = s1
    s2_scr[...] = s2

    @pl.when(j == n_col - 1)
    def _():
        mu = jnp.sum(s1, axis=-1, keepdims=True) * (1.0 / GMLP_HALF)
        msq = jnp.sum(s2, axis=-1, keepdims=True) * (1.0 / GMLP_HALF)
        mu_ref[...] = mu
        rs_ref[...] = lax.rsqrt(jnp.maximum(msq - mu * mu, 0.0) + EPS)


def _gmlp_in(x_parts, g, w, layer, *, tm, tn):
    t = sum(p.shape[0] for p in x_parts)
    d = x_parts[0].shape[1]
    n_col = 2 * GMLP_HALF // tn
    sub = tn // GMLP_GDIM
    x_specs, ranges = _part_specs(x_parts, tm, lambda i, j: i)
    return pl.pallas_call(
        functools.partial(_gmlp_in_kernel, ranges=ranges, n_col=n_col, tn=tn),
        grid=(t // tm, n_col),
        in_specs=x_specs + [
            pl.BlockSpec((1, d), lambda i, j: (0, 0)),
            pl.BlockSpec((None, d, tn), lambda i, j: (layer, 0, j)),
        ],
        out_specs=[
            pl.BlockSpec((sub, tm, GMLP_GDIM), lambda i, j: (j, i, 0)),
            pl.BlockSpec((tm, 1), lambda i, j: (i, 0)),
            pl.BlockSpec((tm, 1), lambda i, j: (i, 0)),
        ],
        out_shape=[
            jax.ShapeDtypeStruct((2 * GMLP_GROUPS, t, GMLP_GDIM), BF16),
            jax.ShapeDtypeStruct((t, 1), F32),
            jax.ShapeDtypeStruct((t, 1), F32),
        ],
        scratch_shapes=[
            pltpu.VMEM((tm, d), BF16),
            pltpu.VMEM((tm, LANES), F32),
            pltpu.VMEM((tm, LANES), F32),
        ],
        compiler_params=_params(("parallel", "arbitrary")),
        name="gmlp_in",
    )(*x_parts, g, w)


def _mm_norm_res_kernel(a_ref, w_ref, x_ref, g_ref, o_ref):
    y = _dot(a_ref[...], w_ref[...])
    o_ref[...] = x_ref[...] + _rms(y, g_ref[...])


def _matmul_norm_res(a, w, layer, x, g, *, tm):
    t, k = a.shape
    d = w.shape[2]
    return pl.pallas_call(
        _mm_norm_res_kernel,
        grid=(t // tm,),
        in_specs=[
            pl.BlockSpec((tm, k), lambda i: (i, 0)),
            pl.BlockSpec((None, k, d), lambda i: (layer, 0, 0), pipeline_mode=pl.Buffered(1)),
            pl.BlockSpec((tm, d), lambda i: (i, 0)),
            pl.BlockSpec((1, d), lambda i: (0, 0)),
        ],
        out_specs=pl.BlockSpec((tm, d), lambda i: (i, 0)),
        out_shape=jax.ShapeDtypeStruct((t, d), F32),
        compiler_params=_params(("parallel",)),
        name="matmul_norm_res",
    )(a, w, x, g)


def _gmlp_out_kernel(*refs, ranges, tm):
    n_parts = len(ranges)
    (u_ref, v_ref, mu_ref, rs_ref, lng_ref, lnb_ref, ws_ref, bs_ref, wo_ref, g_ref) = refs[:10]
    x_refs = refs[10:10 + n_parts]
    o_ref = refs[10 + n_parts]
    i, grp = pl.program_id(0), pl.program_id(1)

    @pl.when(grp == 0)
    def _():
        o_ref[...] = jnp.zeros_like(o_ref)

    vn = ((v_ref[0].astype(F32) - mu_ref[...]) * rs_ref[...] * lng_ref[0] + lnb_ref[0]).astype(BF16)
    ws = ws_ref[0]
    bs = bs_ref[0]
    parts = []
    for n in range(tm // BLOCK):
        rows = slice(n * BLOCK, (n + 1) * BLOCK)
        mixed = _dot(ws, vn[rows, :]) + bs
        parts.append((u_ref[0, rows, :].astype(F32) * mixed).astype(BF16))
    o_ref[...] += _dot(jnp.concatenate(parts, axis=0), wo_ref[...])

    def epilogue(x_ref):
        o_ref[...] = x_ref[...] + _rms(o_ref[...], g_ref[...])

    _for_owner(i, ranges, x_refs, grp == GMLP_GROUPS - 1, epilogue)


def _gmlp_out(z, mu, rstd, ln_g, ln_b, w_s, b_s, w_out, layer, x_parts, g, *, tm):
    t = sum(p.shape[0] for p in x_parts)
    d = x_parts[0].shape[1]
    x_specs, ranges = _part_specs(x_parts, tm, lambda i, c: i)
    return pl.pallas_call(
        functools.partial(_gmlp_out_kernel, ranges=ranges, tm=tm),
        grid=(t // tm, GMLP_GROUPS),
        in_specs=[
            pl.BlockSpec((1, tm, GMLP_GDIM), lambda i, c: (c, i, 0)),
            pl.BlockSpec((1, tm, GMLP_GDIM), lambda i, c: (GMLP_GROUPS + c, i, 0)),
            pl.BlockSpec((tm, 1), lambda i, c: (i, 0)),
            pl.BlockSpec((tm, 1), lambda i, c: (i, 0)),
            pl.BlockSpec((1, 1, GMLP_GDIM), lambda i, c: (c, 0, 0)),
            pl.BlockSpec((1, 1, GMLP_GDIM), lambda i, c: (c, 0, 0)),
            pl.BlockSpec((1, BLOCK, BLOCK), lambda i, c: (c, 0, 0)),
            pl.BlockSpec((1, BLOCK, 1), lambda i, c: (c, 0, 0)),
            pl.BlockSpec((None, GMLP_GDIM, d), lambda i, c: (layer, c, 0)),
            pl.BlockSpec((1, d), lambda i, c: (0, 0)),
        ] + x_specs,
        out_specs=pl.BlockSpec((tm, d), lambda i, c: (i, 0)),
        out_shape=jax.ShapeDtypeStruct((t, d), F32),
        compiler_params=_params(("parallel", "arbitrary")),
        name="gmlp_out",
    )(z, z, mu, rstd, ln_g, ln_b, w_s, b_s, w_out, g, *x_parts)


def _win_attn_kernel(sink_ref, q_ref, kp_ref, kc_ref, kn_ref, vp_ref, vc_ref, vn_ref, bias_ref,
                     o_ref, *, r_blocks, n_steps):
    i = pl.program_id(1)
    kall = jnp.concatenate([kp_ref[0], kc_ref[0], kn_ref[0]], axis=0)
    vall = jnp.concatenate([vp_ref[0], vc_ref[0], vn_ref[0]], axis=0)
    col = lax.broadcasted_iota(jnp.int32, (1, 3 * BLOCK), 1)
    sk = jnp.concatenate(
        [jnp.full((BLOCK, 1), sink_ref[h], F32) for h in range(N_HEADS)], axis=0)
    hrows = KV_GROUP * BLOCK
    for r in range(r_blocks):
        rows = slice(r * BLOCK, (r + 1) * BLOCK)
        oob = None
        if r == 0:
            oob = (col < BLOCK) & (i == 0)
        if r == r_blocks - 1:
            hi = (col >= 2 * BLOCK) & (i == n_steps - 1)
            oob = hi if oob is None else (oob | hi)
        parts = []
        for kh in range(N_KV_HEADS):
            kb = kall[r * BLOCK:(r + 3) * BLOCK, kh * HEAD_DIM:(kh + 1) * HEAD_DIM]
            qs = jnp.concatenate(
                [q_ref[0, rows, h * HEAD_DIM:(h + 1) * HEAD_DIM]
                 for h in range(kh * KV_GROUP, (kh + 1) * KV_GROUP)], axis=0)
            parts.append(
                lax.dot_general(qs, kb, (((1,), (1,)), ((), ())), preferred_element_type=F32))
        s = jnp.concatenate(parts, axis=0) + bias_ref[...]
        if oob is not None:
            s = jnp.where(oob, NEG_INF, s)
        m = jnp.maximum(jnp.max(s, axis=-1, keepdims=True), sk)
        p = jnp.exp2(s - m)
        inv = 1.0 / (jnp.sum(p, axis=-1, keepdims=True) + jnp.exp2(sk - m))
        pb = p.astype(BF16)
        for kh in range(N_KV_HEADS):
            vb = vall[r * BLOCK:(r + 3) * BLOCK, kh * HEAD_DIM:(kh + 1) * HEAD_DIM]
            hsl = slice(kh * hrows, (kh + 1) * hrows)
            o = _dot(pb[hsl], vb) * inv[hsl]
            for gq in range(KV_GROUP):
                h = kh * KV_GROUP + gq
                o_ref[0, rows, h * HEAD_DIM:(h + 1) * HEAD_DIM] = (
                    o[gq * BLOCK:(gq + 1) * BLOCK].astype(o_ref.dtype))


def _win_attention(qkv, sink, bias, *, nseq, r_blocks):
    tq = r_blocks * BLOCK
    n_steps = SEQ // tq
    nb = SEQ // BLOCK
    qkv3 = qkv.reshape(nseq, SEQ, qkv.shape[-1])
    kcol = N_HEADS * HEAD_DIM // (N_KV_HEADS * HEAD_DIM)
    vcol = kcol + 1
    kvw = N_KV_HEADS * HEAD_DIM

    def prev_map(c):
        return lambda b, i: (b, jnp.maximum(i * r_blocks - 1, 0), c)

    def cur_map(c):
        return lambda b, i: (b, i, c)

    def next_map(c):
        return lambda b, i: (b, jnp.minimum((i + 1) * r_blocks, nb - 1), c)

    out = pl.pallas_call(
        functools.partial(_win_attn_kernel, r_blocks=r_blocks, n_steps=n_steps),
        grid=(nseq, n_steps),
        in_specs=[
            pl.BlockSpec(memory_space=pltpu.SMEM),
            pl.BlockSpec((1, tq, N_HEADS * HEAD_DIM), lambda b, i: (b, i, 0)),
            pl.BlockSpec((1, BLOCK, kvw), prev_map(kcol)),
            pl.BlockSpec((1, tq, kvw), cur_map(kcol)),
            pl.BlockSpec((1, BLOCK, kvw), next_map(kcol)),
            pl.BlockSpec((1, BLOCK, kvw), prev_map(vcol)),
            pl.BlockSpec((1, tq, kvw), cur_map(vcol)),
            pl.BlockSpec((1, BLOCK, kvw), next_map(vcol)),
            pl.BlockSpec((N_HEADS * BLOCK, 3 * BLOCK), lambda b, i: (0, 0)),
        ],
        out_specs=pl.BlockSpec((1, tq, N_HEADS * HEAD_DIM), lambda b, i: (b, i, 0)),
        out_shape=jax.ShapeDtypeStruct((nseq, SEQ, N_HEADS * HEAD_DIM), BF16),
        compiler_params=_params(("parallel", "parallel")),
        name="win_attention",
    )(sink, qkv3, qkv3, qkv3, qkv3, qkv3, qkv3, qkv3, bias)
    return out.reshape(nseq * SEQ, N_HEADS * HEAD_DIM)


def _t5_bucket(rel):
    half = N_BUCKETS // 2
    exact = half // 2
    n = np.abs(rel)
    large = exact + (np.log(np.maximum(n, 1) / exact) / np.log(MAX_DISTANCE / exact)
                     * (half - exact)).astype(np.int32)
    large = np.minimum(large, half - 1)
    return (np.where(rel > 0, half, 0) + np.where(n < exact, n, large)).astype(np.int32)


def _band_bias(rel_table):
    rel = np.arange(3 * BLOCK)[None, :] - BLOCK - np.arange(BLOCK)[:, None]
    onehot = (_t5_bucket(rel).reshape(1, -1) == np.arange(N_BUCKETS)[:, None]).astype(np.float32)
    bias = jnp.einsum("bh,bn->hn", rel_table.astype(F32) * LOG2E, onehot,
                      precision=lax.Precision.HIGHEST)
    bias = jnp.where((np.abs(rel) <= WINDOW).reshape(1, -1), bias, NEG_INF)
    return bias.reshape(N_HEADS * BLOCK, 3 * BLOCK)


def _kv_kernel(mem_ref, g_ref, w_ref, o_ref):
    h = _rms(mem_ref[0], g_ref[0]).astype(BF16)
    o_ref[0, 0] = _dot(h, w_ref[0]).astype(o_ref.dtype)


def _kv_proj(mem, g, w_kv):
    nseq = mem.shape[0]
    return pl.pallas_call(
        _kv_kernel,
        grid=(DEPTH, nseq),
        in_specs=[
            pl.BlockSpec((1, MEM_LEN, D_MODEL), lambda l, b: (b, 0, 0)),
            pl.BlockSpec((1, 1, D_MODEL), lambda l, b: (l, 0, 0)),
            pl.BlockSpec((1, D_MODEL, 2 * MEM_WIDTH), lambda l, b: (l, 0, 0)),
        ],
        out_specs=pl.BlockSpec((1, 1, MEM_LEN, 2 * MEM_WIDTH), lambda l, b: (l, b, 0, 0)),
        out_shape=jax.ShapeDtypeStruct((DEPTH, nseq, MEM_LEN, 2 * MEM_WIDTH), BF16),
        compiler_params=_params(("parallel", "parallel")),
        name="kv_proj",
    )(mem, g, w_kv)


def _mem_attn_kernel(x_ref, gpre_ref, gpost_ref, wq_ref, kv_ref, wo_ref, o_ref):
    x = x_ref[...]
    h = _rms(x, gpre_ref[...]).astype(BF16)
    q = (_dot(h, wq_ref[...]) * (HEAD_DIM ** -0.5)).astype(BF16)
    outs = []
    for hd in range(MEM_HEADS):
        sl = slice(hd * HEAD_DIM, (hd + 1) * HEAD_DIM)
        k = kv_ref[0, :, sl]
        v = kv_ref[0, :, MEM_WIDTH + hd * HEAD_DIM:MEM_WIDTH + (hd + 1) * HEAD_DIM]
        s = lax.dot_general(q[:, sl], k, (((1,), (1,)), ((), ())), preferred_element_type=F32)
        m = jnp.max(s, axis=-1, keepdims=True)
        p = jnp.exp(s - m)
        denom = jnp.sum(p, axis=-1, keepdims=True)
        outs.append((_dot(p.astype(BF16), v) / denom).astype(BF16))
    y = _dot(jnp.concatenate(outs, axis=-1), wo_ref[...])
    o_ref[...] = x + _rms(y, gpost_ref[...])


def _mem_attention(x, gpre, gpost, w_q, kv, w_o, layer, *, tm):
    t, d = x.shape
    per_seq = SEQ // tm
    return pl.pallas_call(
        _mem_attn_kernel,
        grid=(t // tm,),
        in_specs=[
            pl.BlockSpec((tm, d), lambda i: (i, 0)),
            pl.BlockSpec((1, d), lambda i: (0, 0)),
            pl.BlockSpec((1, d), lambda i: (0, 0)),
            pl.BlockSpec((None, d, MEM_WIDTH), lambda i: (layer, 0, 0)),
            pl.BlockSpec((None, 1, MEM_LEN, 2 * MEM_WIDTH), lambda i: (layer, i // per_seq, 0, 0)),
            pl.BlockSpec((None, MEM_WIDTH, d), lambda i: (layer, 0, 0)),
        ],
        out_specs=pl.BlockSpec((tm, d), lambda i: (i, 0)),
        out_shape=jax.ShapeDtypeStruct((t, d), F32),
        compiler_params=_params(("parallel",)),
        name="mem_attention",
    )(x, gpre, gpost, w_q, kv, w_o)


def _ffn_kernel(x_ref, gpre_ref, gpost_ref, wg_ref, wu_ref, wd_ref, o_ref, h_scr, *, n_chunks):
    j = pl.program_id(1)

    @pl.when(j == 0)
    def _():
        h_scr[...] = _rms(x_ref[...], gpre_ref[...]).astype(BF16)
        o_ref[...] = jnp.zeros_like(o_ref)

    h = h_scr[...]
    gate = _dot(h, wg_ref[...])
    up = _dot(h, wu_ref[...])
    a = (gate * _sigmoid(gate) * up).astype(BF16)
    o_ref[...] += _dot(a, wd_ref[...])

    @pl.when(j == n_chunks - 1)
    def _():
        o_ref[...] = x_ref[...] + _rms(o_ref[...], gpost_ref[...])


def _ffn(x, gpre, gpost, w_gate_up, w_down, layer, *, tm, tn, row0=0, rows=None):
    d = x.shape[1]
    t = x.shape[0] if rows is None else rows
    tile0 = row0 // tm
    n_chunks = D_FF // tn
    return pl.pallas_call(
        functools.partial(_ffn_kernel, n_chunks=n_chunks),
        grid=(t // tm, n_chunks),
        in_specs=[
            pl.BlockSpec((tm, d), lambda i, j: (i + tile0, 0), pipeline_mode=pl.Buffered(1)),
            pl.BlockSpec((1, d), lambda i, j: (0, 0)),
            pl.BlockSpec((1, d), lambda i, j: (0, 0)),
            pl.BlockSpec((None, d, tn), lambda i, j: (layer, 0, j)),
            pl.BlockSpec((None, d, tn), lambda i, j: (layer, 0, j + n_chunks)),
            pl.BlockSpec((None, tn, d), lambda i, j: (layer, j, 0)),
        ],
        out_specs=pl.BlockSpec((tm, d), lambda i, j: (i, 0)),
        out_shape=jax.ShapeDtypeStruct((t, d), F32),
        scratch_shapes=[pltpu.VMEM((tm, d), BF16)],
        compiler_params=_params(("parallel", "arbitrary")),
        name="ffn",
    )(x, gpre, gpost, w_gate_up, w_gate_up, w_down)


N_CT = D_RNN // LANES
HALO = 2 * SUBLANES


def _gate_windows():
    lo_hi = []
    for j in range(N_CT):
        h0 = (LANES * j) // RNN_BDIM
        h1 = (LANES * j + LANES - 1) // RNN_BDIM
        lo = (RNN_BDIM * h0) // LANES * LANES
        hi = -(-(RNN_BDIM * (h1 + 1)) // LANES) * LANES
        lo_hi.append((lo, hi))
    width = max(hi - lo for lo, hi in lo_hi)
    starts = tuple(min(lo, D_RNN - width) for lo, _ in lo_hi)
    return starts, width


GATE_STARTS, GATE_WIDTH = _gate_windows()


def _gate_weights(w_a, w_x):
    def tile(w, j):
        st = GATE_STARTS[j]
        pieces = []
        for h in range((LANES * j) // RNN_BDIM, (LANES * j + LANES - 1) // RNN_BDIM + 1):
            c0 = max(LANES * j, RNN_BDIM * h) - RNN_BDIM * h
            c1 = min(LANES * (j + 1), RNN_BDIM * (h + 1)) - RNN_BDIM * h
            top = RNN_BDIM * h - st
            pieces.append(jnp.pad(w[h, :, c0:c1], ((top, GATE_WIDTH - RNN_BDIM - top), (0, 0))))
        return jnp.concatenate(pieces, axis=1)

    wa, wx = (w_a * -LOG2E).astype(BF16), (w_x * -LOG2E).astype(BF16)
    return jnp.stack([jnp.concatenate([tile(wa, j), tile(wx, j)], axis=1) for j in range(N_CT)])


def _scan_kernel(*refs, ts, n_tiles, nseq, reverse):
    if reverse:
        (xp_ref, xc_ref, xn_ref, cw_ref, cb_ref, wp_ref, ba_ref, bx_ref, lam_ref,
         hsf_ref, gate_ref, o_ref, conv_scr, convb_scr, a_scr, u_scr, h_scr) = refs
    else:
        (xp_ref, xc_ref, xn_ref, cw_ref, cb_ref, wp_ref, ba_ref, bx_ref, lam_ref,
         o_ref, conv_scr, convb_scr, a_scr, u_scr, h_scr) = refs
    i = pl.program_id(0)
    tile = (n_tiles - 1 - i) if reverse else i

    @pl.when(i == 0)
    def _():
        h_scr[...] = jnp.zeros_like(h_scr)
        a_scr[...] = jnp.zeros_like(a_scr)
        u_scr[...] = jnp.zeros_like(u_scr)

    cw = cw_ref[...]
    cb = cb_ref[...]
    shifts = [k - CONV_LEFT for k in range(CONV_WIDTH) if k != CONV_LEFT]
    rr = lax.broadcasted_iota(jnp.int32, (ts, ts + 2 * HALO), 0)
    cc = lax.broadcasted_iota(jnp.int32, (ts, ts + 2 * HALO), 1)
    sel = jnp.concatenate(
        [jnp.where(cc == rr + HALO + s, 1.0, 0.0).astype(BF16) for s in shifts], axis=0)
    for b in range(nseq):
        prev = xp_ref[b]
        nxt = xn_ref[b]
        prev = jnp.where(tile == 0, jnp.zeros_like(prev), prev)
        nxt = jnp.where(tile == n_tiles - 1, jnp.zeros_like(nxt), nxt)
        cur = xc_ref[b]
        shifted = _dot(sel, jnp.concatenate([prev, cur, nxt], axis=0))
        acc = cb + cur.astype(F32) * cw[CONV_LEFT:CONV_LEFT + 1]
        for n, s in enumerate(shifts):
            k = s + CONV_LEFT
            acc = acc + shifted[n * ts:(n + 1) * ts] * cw[k:k + 1]
        conv_scr[b * ts:(b + 1) * ts, :] = acc
        convb_scr[b * ts:(b + 1) * ts, :] = acc.astype(BF16)

    for j in range(N_CT):
        st = GATE_STARTS[j]
        sl = slice(LANES * j, LANES * (j + 1))
        pre = _dot(convb_scr[:, st:st + GATE_WIDTH], wp_ref[j])
        r = 1.0 / (1.0 + jnp.exp2(pre[:, :LANES] + ba_ref[:, sl]))
        gi = 1.0 / (1.0 + jnp.exp2(pre[:, LANES:] + bx_ref[:, sl]))
        z = -lam_ref[:, sl]
        softplus = jnp.maximum(z, 0.0) + jnp.log(1.0 + jnp.exp(-jnp.abs(z)))
        a = jnp.exp2(r * ((-LRU_C * LOG2E) * softplus))
        u = jnp.sqrt(1.0 - a * a) * (gi * conv_scr[:, sl])
        for b in range(nseq):
            a_scr[j, pl.ds(b, ts, stride=SUBLANES), :] = a[b * ts:(b + 1) * ts]
            u_scr[j, pl.ds(b, ts, stride=SUBLANES), :] = u[b * ts:(b + 1) * ts]

    def step(k, h):
        t = (ts - 1 - k) if reverse else k
        row = pl.multiple_of(t * SUBLANES, SUBLANES)
        h = a_scr[:, pl.ds(row, SUBLANES), :] * h + u_scr[:, pl.ds(row, SUBLANES), :]
        u_scr[:, pl.ds(row, SUBLANES), :] = h
        return h

    h_scr[...] = lax.fori_loop(0, ts, step, h_scr[...], unroll=2)

    for b in range(nseq):
        for j in range(N_CT):
            sl = slice(LANES * j, LANES * (j + 1))
            y = u_scr[j, pl.ds(b, ts, stride=SUBLANES), :]
            if reverse:
                y = gate_ref[b, :, sl].astype(F32) * (y + hsf_ref[b, :, sl].astype(F32))
            o_ref[b, :, sl] = y.astype(o_ref.dtype)


def _rglru_scan(y_in, conv_w, conv_b, wp, b_a, b_x, lam, hs_fwd, *, ts, nseq, reverse):
    n_tiles = SEQ // ts
    per_tile = ts // HALO

    def tmap(i):
        return (n_tiles - 1 - i) if reverse else i

    in_specs = [
        pl.BlockSpec((nseq, HALO, D_RNN), lambda i: (0, jnp.maximum(tmap(i) * per_tile - 1, 0), 1)),
        pl.BlockSpec((nseq, ts, D_RNN), lambda i: (0, tmap(i), 1)),
        pl.BlockSpec((nseq, HALO, D_RNN),
                     lambda i: (0, jnp.minimum((tmap(i) + 1) * per_tile, SEQ // HALO - 1), 1)),
        pl.BlockSpec((CONV_WIDTH, D_RNN), lambda i: (0, 0)),
        pl.BlockSpec((1, D_RNN), lambda i: (0, 0)),
        pl.BlockSpec((N_CT, GATE_WIDTH, 2 * LANES), lambda i: (0, 0, 0)),
        pl.BlockSpec((1, D_RNN), lambda i: (0, 0)),
        pl.BlockSpec((1, D_RNN), lambda i: (0, 0)),
        pl.BlockSpec((1, D_RNN), lambda i: (0, 0)),
    ]
    args = [y_in, y_in, y_in, conv_w, conv_b, wp, b_a, b_x, lam]
    if reverse:
        in_specs += [
            pl.BlockSpec((nseq, ts, D_RNN), lambda i: (0, tmap(i), 0)),
            pl.BlockSpec((nseq, ts, D_RNN), lambda i: (0, tmap(i), 0)),
        ]
        args += [hs_fwd, y_in]
    return pl.pallas_call(
        functools.partial(_scan_kernel, ts=ts, n_tiles=n_tiles, nseq=nseq, reverse=reverse),
        grid=(n_tiles,),
        in_specs=in_specs,
        out_specs=pl.BlockSpec((nseq, ts, D_RNN), lambda i: (0, tmap(i), 0)),
        out_shape=jax.ShapeDtypeStruct((nseq, SEQ, D_RNN), BF16),
        scratch_shapes=[
            pltpu.VMEM((nseq * ts, D_RNN), F32),
            pltpu.VMEM((nseq * ts, D_RNN), BF16),
            pltpu.VMEM((N_CT, ts * SUBLANES, LANES), F32),
            pltpu.VMEM((N_CT, ts * SUBLANES, LANES), F32),
            pltpu.VMEM((N_CT, SUBLANES, LANES), F32),
        ],
        compiler_params=_params(("arbitrary",)),
        name="rglru_scan_bwd" if reverse else "rglru_scan_fwd",
    )(*args)


def kernel(x_prompt, x_sample, mem_prompt, mem_sample, norms, rel_bias, a_w_in, a_ln_g, a_ln_b, a_w_s, a_b_s, a_w_out, b_w_qkv, b_sink, b_w_o, c_w_in, c_conv_w, c_conv_b, c_w_a, c_b_a, c_w_x, c_b_x, c_lam, c_w_out, m_w_q, m_w_kv, m_w_o, f_w_gate_up, f_w_down):
    n_prompt = x_prompt.shape[0]
    nseq = n_prompt + x_sample.shape[0]
    assert nseq <= SUBLANES
    t = nseq * SEQ
    t_prompt = n_prompt * SEQ
    x = None
    x_parts = [x_prompt.reshape(t_prompt, D_MODEL), x_sample.reshape(t - t_prompt, D_MODEL)]
    mem = jnp.concatenate([mem_prompt, mem_sample], axis=0)

    def gain(i, k):
        return norms[i, k].reshape(1, D_MODEL)

    a_w_in_b, a_w_out_b = a_w_in.astype(BF16), a_w_out.astype(BF16)
    q_cols = N_HEADS * HEAD_DIM
    b_w_qkv_b = jnp.concatenate(
        [b_w_qkv[:, :, :q_cols] * (HEAD_DIM ** -0.5 * LOG2E), b_w_qkv[:, :, q_cols:]],
        axis=-1).astype(BF16)
    b_w_o_b = b_w_o.astype(BF16)
    c_w_in_b, c_w_out_b = c_w_in.astype(BF16), c_w_out.astype(BF16)
    m_w_q_b, m_w_o_b = m_w_q.astype(BF16), m_w_o.astype(BF16)
    f_w_gate_up_b, f_w_down_b = f_w_gate_up.astype(BF16), f_w_down.astype(BF16)

    kv_all = _kv_proj(mem, norms[:, 4].reshape(DEPTH, 1, D_MODEL), m_w_kv.astype(BF16))

    for i in range(DEPTH):
        kind, j = i % N_MIXERS, i // N_MIXERS
        if kind == 0:
            parts = x_parts if x is None else [x]
            z, mu, rstd = _gmlp_in(parts, gain(i, 0), a_w_in_b, j, tm=1024, tn=2 * GMLP_GDIM)
            x = _gmlp_out(z, mu, rstd, a_ln_g[j].reshape(GMLP_GROUPS, 1, GMLP_GDIM),
                          a_ln_b[j].reshape(GMLP_GROUPS, 1, GMLP_GDIM),
                          a_w_s[j].astype(BF16), a_b_s[j].reshape(GMLP_GROUPS, BLOCK, 1),
                          a_w_out_b, j, parts, gain(i, 1), tm=512)
        elif kind == 1:
            qkv = _norm_matmul(x, gain(i, 0), b_w_qkv_b, j, tm=1024, tn=1024)
            o = _win_attention(qkv, b_sink[j] * LOG2E, _band_bias(rel_bias), nseq=nseq, r_blocks=4)
            x = _matmul_norm_res(o, b_w_o_b, j, x, gain(i, 1), tm=512)
        else:
            y_in = _norm_matmul(x, gain(i, 0), c_w_in_b, j, tm=1024, tn=D_RNN // 2, n_gelu=2)
            y_in = y_in.reshape(nseq, SEQ, 2 * D_RNN)
            scan_args = (c_conv_w[j], c_conv_b[j].reshape(1, D_RNN))
            hs = None
            for d in range(2):
                hs = _rglru_scan(y_in, *scan_args, _gate_weights(c_w_a[j, d], c_w_x[j, d]),
                                 (c_b_a[j, d] * -LOG2E).reshape(1, D_RNN),
                                 (c_b_x[j, d] * -LOG2E).reshape(1, D_RNN),
                                 c_lam[j, d].reshape(1, D_RNN), hs, ts=64, nseq=nseq,
                                 reverse=(d == 1))
            x = _matmul_norm_res(hs.reshape(t, D_RNN), c_w_out_b, j, x, gain(i, 1), tm=512)
        x = _mem_attention(x, gain(i, 2), gain(i, 3), m_w_q_b, kv_all, m_w_o_b, i, tm=512)
        ffn = functools.partial(_ffn, x, gain(i, 5), gain(i, 6), f_w_gate_up_b, f_w_down_b, i,
                                tm=1024, tn=512)
        if i < DEPTH - 1:
            x = ffn()

    y_prompt = ffn(row0=0, rows=t_prompt).reshape(n_prompt, SEQ, D_MODEL)
    y_sample = ffn(row0=t_prompt, rows=t - t_prompt).reshape(nseq - n_prompt, SEQ, D_MODEL)
    return (y_prompt, y_sample)
```

```python
import functools

import numpy as np
import jax
import jax.numpy as jnp
from jax import lax
from jax.experimental import pallas as pl
from jax.experimental.pallas import tpu as pltpu

D_MODEL = 2048
SEQ = 4096
DEPTH = 4
N_MIXERS = 3
BLOCK = 128
HEAD_DIM = 128
EPS = 1e-6
NEG_INF = -1e30
GMLP_HALF = 2 * D_MODEL
GMLP_GROUPS = 8
GMLP_GDIM = GMLP_HALF // GMLP_GROUPS
N_HEADS = D_MODEL // HEAD_DIM
N_KV_HEADS = 4
KV_GROUP = N_HEADS // N_KV_HEADS
WINDOW = 128
N_BUCKETS = 32
MAX_DISTANCE = 128
D_RNN = 2816
RNN_BLOCKS = 16
RNN_BDIM = D_RNN // RNN_BLOCKS
CONV_WIDTH = 4
CONV_LEFT = 2
LRU_C = 8.0
MEM_LEN = 256
MEM_HEADS = 4
MEM_WIDTH = MEM_HEADS * HEAD_DIM
D_FF = 5632

LANES = 128
SUBLANES = 8
VMEM_LIMIT = 56 * 1024 * 1024

F32 = jnp.float32
BF16 = jnp.bfloat16
LOG2E = float(np.log2(np.e))


def _params(sem):
    return pltpu.CompilerParams(dimension_semantics=sem, vmem_limit_bytes=VMEM_LIMIT)


def _rms(x, g):
    ms = jnp.mean(x * x, axis=-1, keepdims=True)
    return x * lax.rsqrt(ms + EPS) * g


def _gelu(x):
    c = np.sqrt(2.0 / np.pi).astype(np.float32)
    return 0.5 * x * (1.0 + jnp.tanh(c * (x + 0.044715 * (x * x * x))))


def _sigmoid(x):
    return 1.0 / (1.0 + jnp.exp(-x))


def _dot(a, b):
    return jnp.dot(a, b, preferred_element_type=F32)


def _norm_mm_kernel(x_ref, g_ref, w_ref, o_ref, h_scr, *, n_gelu):
    j = pl.program_id(1)

    @pl.when(j == 0)
    def _():
        h_scr[...] = _rms(x_ref[...], g_ref[...]).astype(BF16)

    def emit(act):
        z = _dot(h_scr[...], w_ref[...])
        o_ref[...] = (_gelu(z) if act else z).astype(o_ref.dtype)

    if n_gelu == 0:
        emit(False)
    else:
        pl.when(j < n_gelu)(lambda: emit(True))
        pl.when(j >= n_gelu)(lambda: emit(False))


def _norm_matmul(x, g, w, layer, *, tm, tn, n_gelu=0):
    t, d = x.shape
    n = w.shape[2]
    return pl.pallas_call(
        functools.partial(_norm_mm_kernel, n_gelu=n_gelu),
        grid=(t // tm, n // tn),
        in_specs=[
            pl.BlockSpec((tm, d), lambda i, j: (i, 0)),
            pl.BlockSpec((1, d), lambda i, j: (0, 0)),
            pl.BlockSpec((None, d, tn), lambda i, j: (layer, 0, j)),
        ],
        out_specs=pl.BlockSpec((tm, tn), lambda i, j: (i, j)),
        out_shape=jax.ShapeDtypeStruct((t, n), BF16),
        scratch_shapes=[pltpu.VMEM((tm, d), BF16)],
        compiler_params=_params(("parallel", "arbitrary")),
        name="norm_matmul",
    )(x, g, w)


def _part_specs(parts, tm, tile_of):
    specs, ranges, lo = [], [], 0
    for p in parts:
        n = p.shape[0] // tm
        specs.append(pl.BlockSpec(
            (tm, p.shape[1]),
            lambda *idx, lo=lo, n=n: (jnp.clip(tile_of(*idx) - lo, 0, n - 1), 0)))
        ranges.append((lo, lo + n))
        lo += n
    return specs, tuple(ranges)


def _for_owner(i, ranges, refs, cond, fn):
    for (lo, hi), ref in zip(ranges, refs):
        pl.when(cond & (i >= lo) & (i < hi))(functools.partial(fn, ref))


MXU_COLS = 256


def _gmlp_in_kernel(*refs, ranges, n_col, tn):
    n_parts = len(ranges)
    x_refs = refs[:n_parts]
    g_ref, w_ref, z_ref, mu_ref, rs_ref, h_scr, s1_scr, s2_scr = refs[n_parts:]
    i, j = pl.program_id(0), pl.program_id(1)
    tm = h_scr.shape[0]

    def prologue(x_ref):
        h_scr[...] = _rms(x_ref[...], g_ref[...]).astype(BF16)
        s1_scr[...] = jnp.zeros_like(s1_scr)
        s2_scr[...] = jnp.zeros_like(s2_scr)

    _for_owner(i, ranges, x_refs, j == 0, prologue)

    p1 = jnp.zeros((tm, LANES), F32)
    p2 = jnp.zeros((tm, LANES), F32)
    per_tile = GMLP_GDIM // MXU_COLS
    for c in range(tn // MXU_COLS):
        cols = slice(c * MXU_COLS, (c + 1) * MXU_COLS)
        zc = _gelu(_dot(h_scr[...], w_ref[:, cols]))
        lo = (c % per_tile) * MXU_COLS
        z_ref[c // per_tile, :, lo:lo + MXU_COLS] = zc.astype(z_ref.dtype)
        for k in range(MXU_COLS // LANES):
            zk = zc[:, k * LANES:(k + 1) * LANES]
            p1 = p1 + zk
            p2 = p2 + zk * zk

    first_v = j == n_col // 2
    s1 = jnp.where(first_v, p1, s1_scr[...] + p1)
    s2 = jnp.where(first_v, p2, s2_scr[...] + p2)
    s1_scr[...] = s1
    s2_scr[...] = s2

    @pl.when(j == n_col - 1)
    def _():
        mu = jnp.sum(s1, axis=-1, keepdims=True) * (1.0 / GMLP_HALF)
        msq = jnp.sum(s2, axis=-1, keepdims=True) * (1.0 / GMLP_HALF)
        mu_ref[...] = mu
        rs_ref[...] = lax.rsqrt(jnp.maximum(msq - mu * mu, 0.0) + EPS)


def _gmlp_in(x_parts, g, w, layer, *, tm, tn):
    t = sum(p.shape[0] for p in x_parts)
    d = x_parts[0].shape[1]
    n_col = 2 * GMLP_HALF // tn
    sub = tn // GMLP_GDIM
    x_specs, ranges = _part_specs(x_parts, tm, lambda i, j: i)
    return pl.pallas_call(
        functools.partial(_gmlp_in_kernel, ranges=ranges, n_col=n_col, tn=tn),
        grid=(t // tm, n_col),
        in_specs=x_specs + [
            pl.BlockSpec((1, d), lambda i, j: (0, 0)),
            pl.BlockSpec((None, d, tn), lambda i, j: (layer, 0, j)),
        ],
        out_specs=[
            pl.BlockSpec((sub, tm, GMLP_GDIM), lambda i, j: (j, i, 0)),
            pl.BlockSpec((tm, 1), lambda i, j: (i, 0)),
            pl.BlockSpec((tm, 1), lambda i, j: (i, 0)),
        ],
        out_shape=[
            jax.ShapeDtypeStruct((2 * GMLP_GROUPS, t, GMLP_GDIM), BF16),
            jax.ShapeDtypeStruct((t, 1), F32),
            jax.ShapeDtypeStruct((t, 1), F32),
        ],
        scratch_shapes=[
            pltpu.VMEM((tm, d), BF16),
            pltpu.VMEM((tm, LANES), F32),
            pltpu.VMEM((tm, LANES), F32),
        ],
        compiler_params=_params(("parallel", "arbitrary")),
        name="gmlp_in",
    )(*x_parts, g, w)


def _mm_norm_res_kernel(a_ref, w_ref, x_ref, g_ref, o_ref):
    y = _dot(a_ref[...], w_ref[...])
    o_ref[...] = x_ref[...] + _rms(y, g_ref[...])


def _matmul_norm_res(a, w, layer, x, g, *, tm):
    t, k = a.shape
    d = w.shape[2]
    return pl.pallas_call(
        _mm_norm_res_kernel,
        grid=(t // tm,),
        in_specs=[
            pl.BlockSpec((tm, k), lambda i: (i, 0)),
            pl.BlockSpec((None, k, d), lambda i: (layer, 0, 0), pipeline_mode=pl.Buffered(1)),
            pl.BlockSpec((tm, d), lambda i: (i, 0)),
            pl.BlockSpec((1, d), lambda i: (0, 0)),
        ],
        out_specs=pl.BlockSpec((tm, d), lambda i: (i, 0)),
        out_shape=jax.ShapeDtypeStruct((t, d), F32),
        compiler_params=_params(("parallel",)),
        name="matmul_norm_res",
    )(a, w, x, g)


PAIRS = GMLP_GROUPS // 2


def _gmlp_out_kernel(*refs, ranges, tm):
    n_parts = len(ranges)
    (u1_ref, v1_ref, mu1_ref, rs1_ref, lng1_ref, lnb1_ref, ws1_ref, bs1_ref,
     u2_ref, v2_ref, mu2_ref, rs2_ref, lng2_ref, lnb2_ref, ws2_ref, bs2_ref,
     wo_ref, g_ref) = refs[:18]
    x_refs = refs[18:18 + n_parts]
    o_ref, a0_scr, a1_scr = refs[18 + n_parts:]
    s = pl.program_id(0)
    q = jnp.maximum(s - 1, 0)
    tile, pair = q // PAIRS, q % PAIRS

    @pl.when(s == 0)
    def _():
        a0_scr[...] = jnp.zeros_like(a0_scr)
        a1_scr[...] = jnp.zeros_like(a1_scr)

    @pl.when(pair == 0)
    def _():
        o_ref[...] = jnp.zeros_like(o_ref)

    def prep(u_ref, v_ref, mu_ref, rs_ref, lng_ref, lnb_ref, ws_ref, bs_ref):
        vn = ((v_ref[0].astype(F32) - mu_ref[...]) * rs_ref[...] * lng_ref[0]
              + lnb_ref[0]).astype(BF16)
        ws = ws_ref[0]
        bs = bs_ref[0]
        parts = []
        for n in range(tm // BLOCK):
            rows = slice(n * BLOCK, (n + 1) * BLOCK)
            mixed = _dot(ws, vn[rows, :]) + bs
            parts.append((u_ref[0, rows, :].astype(F32) * mixed).astype(BF16))
        return jnp.concatenate(parts, axis=0)

    o_ref[...] += _dot(a0_scr[...], wo_ref[:GMLP_GDIM])
    a1_scr[...] = prep(u1_ref, v1_ref, mu1_ref, rs1_ref, lng1_ref, lnb1_ref, ws1_ref, bs1_ref)
    o_ref[...] += _dot(a1_scr[...], wo_ref[GMLP_GDIM:])
    a0_scr[...] = prep(u2_ref, v2_ref, mu2_ref, rs2_ref, lng2_ref, lnb2_ref, ws2_ref, bs2_ref)

    def epilogue(x_ref):
        o_ref[...] = x_ref[...] + _rms(o_ref[...], g_ref[...])

    _for_owner(tile, ranges, x_refs, (pair == PAIRS - 1) & (s > 0), epilogue)


def _gmlp_out(z, mu, rstd, ln_g, ln_b, w_s, b_s, w_out, layer, x_parts, g, *, tm):
    t = sum(p.shape[0] for p in x_parts)
    d = x_parts[0].shape[1]
    n_tiles = t // tm
    last = GMLP_GROUPS * n_tiles - 1

    def proj_tile(s):
        return jnp.maximum(s - 1, 0) // PAIRS

    def prep_specs(pair_of):
        def at(f):
            return lambda s: f(*pair_of(s))
        return [
            pl.BlockSpec((1, tm, GMLP_GDIM), at(lambda i, c: (c, i, 0))),
            pl.BlockSpec((1, tm, GMLP_GDIM), at(lambda i, c: (GMLP_GROUPS + c, i, 0))),
            pl.BlockSpec((tm, 1), at(lambda i, c: (i, 0))),
            pl.BlockSpec((tm, 1), at(lambda i, c: (i, 0))),
            pl.BlockSpec((1, 1, GMLP_GDIM), at(lambda i, c: (c, 0, 0))),
            pl.BlockSpec((1, 1, GMLP_GDIM), at(lambda i, c: (c, 0, 0))),
            pl.BlockSpec((1, BLOCK, BLOCK), at(lambda i, c: (c, 0, 0))),
            pl.BlockSpec((1, BLOCK, 1), at(lambda i, c: (c, 0, 0))),
        ]

    def split(gidx):
        gidx = jnp.clip(gidx, 0, last)
        return gidx // GMLP_GROUPS, gidx % GMLP_GROUPS

    x_specs, ranges = _part_specs(x_parts, tm, proj_tile)
    prep_args = (z, z, mu, rstd, ln_g, ln_b, w_s, b_s)
    return pl.pallas_call(
        functools.partial(_gmlp_out_kernel, ranges=ranges, tm=tm),
        grid=(PAIRS * n_tiles + 1,),
        in_specs=prep_specs(lambda s: split(2 * s - 1)) + prep_specs(lambda s: split(2 * s)) + [
            pl.BlockSpec((None, 2 * GMLP_GDIM, d),
                         lambda s: (layer, jnp.maximum(s - 1, 0) % PAIRS, 0)),
            pl.BlockSpec((1, d), lambda s: (0, 0)),
        ] + x_specs,
        out_specs=pl.BlockSpec((tm, d), lambda s: (proj_tile(s), 0)),
        out_shape=jax.ShapeDtypeStruct((t, d), F32),
        scratch_shapes=[
            pltpu.VMEM((tm, GMLP_GDIM), BF16),
            pltpu.VMEM((tm, GMLP_GDIM), BF16),
        ],
        compiler_params=_params(("arbitrary",)),
        name="gmlp_out",
    )(*prep_args, *prep_args, w_out, g, *x_parts)


def _win_attn_kernel(sink_ref, q_ref, kp_ref, kc_ref, kn_ref, vp_ref, vc_ref, vn_ref, bias_ref,
                     o_ref, *, r_blocks, n_steps):
    i = pl.program_id(1)
    kall = jnp.concatenate([kp_ref[0], kc_ref[0], kn_ref[0]], axis=0)
    vall = jnp.concatenate([vp_ref[0], vc_ref[0], vn_ref[0]], axis=0)
    col = lax.broadcasted_iota(jnp.int32, (1, 3 * BLOCK), 1)
    sk = jnp.concatenate(
        [jnp.full((BLOCK, 1), sink_ref[h], F32) for h in range(N_HEADS)], axis=0)
    hrows = KV_GROUP * BLOCK
    for r in range(r_blocks):
        rows = slice(r * BLOCK, (r + 1) * BLOCK)
        oob = None
        if r == 0:
            oob = (col < BLOCK) & (i == 0)
        if r == r_blocks - 1:
            hi = (col >= 2 * BLOCK) & (i == n_steps - 1)
            oob = hi if oob is None else (oob | hi)
        parts = []
        for kh in range(N_KV_HEADS):
            kb = kall[r * BLOCK:(r + 3) * BLOCK, kh * HEAD_DIM:(kh + 1) * HEAD_DIM]
            qs = jnp.concatenate(
                [q_ref[0, rows, h * HEAD_DIM:(h + 1) * HEAD_DIM]
                 for h in range(kh * KV_GROUP, (kh + 1) * KV_GROUP)], axis=0)
            parts.append(
                lax.dot_general(qs, kb, (((1,), (1,)), ((), ())), preferred_element_type=F32))
        s = jnp.concatenate(parts, axis=0) + bias_ref[...]
        if oob is not None:
            s = jnp.where(oob, NEG_INF, s)
        m = jnp.maximum(jnp.max(s, axis=-1, keepdims=True), sk)
        p = jnp.exp2(s - m)
        inv = 1.0 / (jnp.sum(p, axis=-1, keepdims=True) + jnp.exp2(sk - m))
        pb = p.astype(BF16)
        for kh in range(N_KV_HEADS):
            vb = vall[r * BLOCK:(r + 3) * BLOCK, kh * HEAD_DIM:(kh + 1) * HEAD_DIM]
            hsl = slice(kh * hrows, (kh + 1) * hrows)
            o = _dot(pb[hsl], vb) * inv[hsl]
            for gq in range(KV_GROUP):
                h = kh * KV_GROUP + gq
                o_ref[0, rows, h * HEAD_DIM:(h + 1) * HEAD_DIM] = (
                    o[gq * BLOCK:(gq + 1) * BLOCK].astype(o_ref.dtype))


def _win_attention(qkv, sink, bias, *, nseq, r_blocks):
    tq = r_blocks * BLOCK
    n_steps = SEQ // tq
    nb = SEQ // BLOCK
    qkv3 = qkv.reshape(nseq, SEQ, qkv.shape[-1])
    kcol = N_HEADS * HEAD_DIM // (N_KV_HEADS * HEAD_DIM)
    vcol = kcol + 1
    kvw = N_KV_HEADS * HEAD_DIM

    def prev_map(c):
        return lambda b, i: (b, jnp.maximum(i * r_blocks - 1, 0), c)

    def cur_map(c):
        return lambda b, i: (b, i, c)

    def next_map(c):
        return lambda b, i: (b, jnp.minimum((i + 1) * r_blocks, nb - 1), c)

    out = pl.pallas_call(
        functools.partial(_win_attn_kernel, r_blocks=r_blocks, n_steps=n_steps),
        grid=(nseq, n_steps),
        in_specs=[
            pl.BlockSpec(memory_space=pltpu.SMEM),
            pl.BlockSpec((1, tq, N_HEADS * HEAD_DIM), lambda b, i: (b, i, 0)),
            pl.BlockSpec((1, BLOCK, kvw), prev_map(kcol)),
            pl.BlockSpec((1, tq, kvw), cur_map(kcol)),
            pl.BlockSpec((1, BLOCK, kvw), next_map(kcol)),
            pl.BlockSpec((1, BLOCK, kvw), prev_map(vcol)),
            pl.BlockSpec((1, tq, kvw), cur_map(vcol)),
            pl.BlockSpec((1, BLOCK, kvw), next_map(vcol)),
            pl.BlockSpec((N_HEADS * BLOCK, 3 * BLOCK), lambda b, i: (0, 0)),
        ],
        out_specs=pl.BlockSpec((1, tq, N_HEADS * HEAD_DIM), lambda b, i: (b, i, 0)),
        out_shape=jax.ShapeDtypeStruct((nseq, SEQ, N_HEADS * HEAD_DIM), BF16),
        compiler_params=_params(("parallel", "parallel")),
        name="win_attention",
    )(sink, qkv3, qkv3, qkv3, qkv3, qkv3, qkv3, qkv3, bias)
    return out.reshape(nseq * SEQ, N_HEADS * HEAD_DIM)


def _t5_bucket(rel):
    half = N_BUCKETS // 2
    exact = half // 2
    n = np.abs(rel)
    large = exact + (np.log(np.maximum(n, 1) / exact) / np.log(MAX_DISTANCE / exact)
                     * (half - exact)).astype(np.int32)
    large = np.minimum(large, half - 1)
    return (np.where(rel > 0, half, 0) + np.where(n < exact, n, large)).astype(np.int32)


def _band_bias(rel_table):
    rel = np.arange(3 * BLOCK)[None, :] - BLOCK - np.arange(BLOCK)[:, None]
    onehot = (_t5_bucket(rel).reshape(1, -1) == np.arange(N_BUCKETS)[:, None]).astype(np.float32)
    bias = jnp.einsum("bh,bn->hn", rel_table.astype(F32) * LOG2E, onehot,
                      precision=lax.Precision.HIGHEST)
    bias = jnp.where((np.abs(rel) <= WINDOW).reshape(1, -1), bias, NEG_INF)
    return bias.reshape(N_HEADS * BLOCK, 3 * BLOCK)


def _kv_kernel(mem_ref, g_ref, w_ref, o_ref):
    h = _rms(mem_ref[0], g_ref[0]).astype(BF16)
    o_ref[0, 0] = _dot(h, w_ref[0]).astype(o_ref.dtype)


def _kv_proj(mem, g, w_kv):
    nseq = mem.shape[0]
    return pl.pallas_call(
        _kv_kernel,
        grid=(DEPTH, nseq),
        in_specs=[
            pl.BlockSpec((1, MEM_LEN, D_MODEL), lambda l, b: (b, 0, 0)),
            pl.BlockSpec((1, 1, D_MODEL), lambda l, b: (l, 0, 0)),
            pl.BlockSpec((1, D_MODEL, 2 * MEM_WIDTH), lambda l, b: (l, 0, 0)),
        ],
        out_specs=pl.BlockSpec((1, 1, MEM_LEN, 2 * MEM_WIDTH), lambda l, b: (l, b, 0, 0)),
        out_shape=jax.ShapeDtypeStruct((DEPTH, nseq, MEM_LEN, 2 * MEM_WIDTH), BF16),
        compiler_params=_params(("parallel", "parallel")),
        name="kv_proj",
    )(mem, g, w_kv)


def _mem_attn_kernel(x_ref, gpre_ref, gpost_ref, wq_ref, kv_ref, wo_ref, o_ref):
    x = x_ref[...]
    h = _rms(x, gpre_ref[...]).astype(BF16)
    q = (_dot(h, wq_ref[...]) * (HEAD_DIM ** -0.5)).astype(BF16)
    outs = []
    for hd in range(MEM_HEADS):
        sl = slice(hd * HEAD_DIM, (hd + 1) * HEAD_DIM)
        k = kv_ref[0, :, sl]
        v = kv_ref[0, :, MEM_WIDTH + hd * HEAD_DIM:MEM_WIDTH + (hd + 1) * HEAD_DIM]
        s = lax.dot_general(q[:, sl], k, (((1,), (1,)), ((), ())), preferred_element_type=F32)
        m = jnp.max(s, axis=-1, keepdims=True)
        p = jnp.exp(s - m)
        denom = jnp.sum(p, axis=-1, keepdims=True)
        outs.append((_dot(p.astype(BF16), v) / denom).astype(BF16))
    y = _dot(jnp.concatenate(outs, axis=-1), wo_ref[...])
    o_ref[...] = x + _rms(y, gpost_ref[...])


def _mem_attention(x, gpre, gpost, w_q, kv, w_o, layer, *, tm):
    t, d = x.shape
    per_seq = SEQ // tm
    return pl.pallas_call(
        _mem_attn_kernel,
        grid=(t // tm,),
        in_specs=[
            pl.BlockSpec((tm, d), lambda i: (i, 0)),
            pl.BlockSpec((1, d), lambda i: (0, 0)),
            pl.BlockSpec((1, d), lambda i: (0, 0)),
            pl.BlockSpec((None, d, MEM_WIDTH), lambda i: (layer, 0, 0)),
            pl.BlockSpec((None, 1, MEM_LEN, 2 * MEM_WIDTH), lambda i: (layer, i // per_seq, 0, 0)),
            pl.BlockSpec((None, MEM_WIDTH, d), lambda i: (layer, 0, 0)),
        ],
        out_specs=pl.BlockSpec((tm, d), lambda i: (i, 0)),
        out_shape=jax.ShapeDtypeStruct((t, d), F32),
        compiler_params=_params(("parallel",)),
        name="mem_attention",
    )(x, gpre, gpost, w_q, kv, w_o)


def _ffn_kernel(x_ref, gpre_ref, gpost_ref, wg_ref, wu_ref, wd_ref, o_ref, h_scr, *, n_chunks):
    j = pl.program_id(1)

    @pl.when(j == 0)
    def _():
        h_scr[...] = _rms(x_ref[...], gpre_ref[...]).astype(BF16)
        o_ref[...] = jnp.zeros_like(o_ref)

    h = h_scr[...]
    gate = _dot(h, wg_ref[...])
    up = _dot(h, wu_ref[...])
    a = (gate * _sigmoid(gate) * up).astype(BF16)
    o_ref[...] += _dot(a, wd_ref[...])

    @pl.when(j == n_chunks - 1)
    def _():
        o_ref[...] = x_ref[...] + _rms(o_ref[...], gpost_ref[...])


def _ffn(x, gpre, gpost, w_gate_up, w_down, layer, *, tm, tn, row0=0, rows=None):
    d = x.shape[1]
    t = x.shape[0] if rows is None else rows
    tile0 = row0 // tm
    n_chunks = D_FF // tn
    return pl.pallas_call(
        functools.partial(_ffn_kernel, n_chunks=n_chunks),
        grid=(t // tm, n_chunks),
        in_specs=[
            pl.BlockSpec((tm, d), lambda i, j: (i + tile0, 0), pipeline_mode=pl.Buffered(1)),
            pl.BlockSpec((1, d), lambda i, j: (0, 0)),
            pl.BlockSpec((1, d), lambda i, j: (0, 0)),
            pl.BlockSpec((None, d, tn), lambda i, j: (layer, 0, j)),
            pl.BlockSpec((None, d, tn), lambda i, j: (layer, 0, j + n_chunks)),
            pl.BlockSpec((None, tn, d), lambda i, j: (layer, j, 0)),
        ],
        out_specs=pl.BlockSpec((tm, d), lambda i, j: (i, 0)),
        out_shape=jax.ShapeDtypeStruct((t, d), F32),
        scratch_shapes=[pltpu.VMEM((tm, d), BF16)],
        compiler_params=_params(("parallel", "arbitrary")),
        name="ffn",
    )(x, gpre, gpost, w_gate_up, w_gate_up, w_down)


N_CT = D_RNN // LANES
HALO = 2 * SUBLANES


def _gate_windows():
    lo_hi = []
    for j in range(N_CT):
        h0 = (LANES * j) // RNN_BDIM
        h1 = (LANES * j + LANES - 1) // RNN_BDIM
        lo = (RNN_BDIM * h0) // LANES * LANES
        hi = -(-(RNN_BDIM * (h1 + 1)) // LANES) * LANES
        lo_hi.append((lo, hi))
    width = max(hi - lo for lo, hi in lo_hi)
    starts = tuple(min(lo, D_RNN - width) for lo, _ in lo_hi)
    return starts, width


GATE_STARTS, GATE_WIDTH = _gate_windows()


def _gate_weights(w_a, w_x):
    def tile(w, j):
        st = GATE_STARTS[j]
        pieces = []
        for h in range((LANES * j) // RNN_BDIM, (LANES * j + LANES - 1) // RNN_BDIM + 1):
            c0 = max(LANES * j, RNN_BDIM * h) - RNN_BDIM * h
            c1 = min(LANES * (j + 1), RNN_BDIM * (h + 1)) - RNN_BDIM * h
            top = RNN_BDIM * h - st
            pieces.append(jnp.pad(w[h, :, c0:c1], ((top, GATE_WIDTH - RNN_BDIM - top), (0, 0))))
        return jnp.concatenate(pieces, axis=1)

    wa, wx = (w_a * 0.5).astype(BF16), (w_x * 0.5).astype(BF16)
    return jnp.stack([jnp.concatenate([tile(wa, j), tile(wx, j)], axis=1) for j in range(N_CT)])


def _scan_kernel(*refs, ts, n_tiles, nseq, reverse):
    if reverse:
        (xp_ref, xc_ref, xn_ref, cw_ref, cb_ref, wp_ref, ba_ref, bx_ref, lam_ref,
         hsf_ref, gate_ref, o_ref, conv_scr, convb_scr, a_scr, u_scr, h_scr) = refs
    else:
        (xp_ref, xc_ref, xn_ref, cw_ref, cb_ref, wp_ref, ba_ref, bx_ref, lam_ref,
         o_ref, conv_scr, convb_scr, a_scr, u_scr, h_scr) = refs
    i = pl.program_id(0)
    tile = (n_tiles - 1 - i) if reverse else i

    @pl.when(i == 0)
    def _():
        h_scr[...] = jnp.zeros_like(h_scr)
        a_scr[...] = jnp.zeros_like(a_scr)
        u_scr[...] = jnp.zeros_like(u_scr)

    cw = cw_ref[...]
    cb = cb_ref[...]
    shifts = [k - CONV_LEFT for k in range(CONV_WIDTH) if k != CONV_LEFT]
    rr = lax.broadcasted_iota(jnp.int32, (ts, ts + 2 * HALO), 0)
    cc = lax.broadcasted_iota(jnp.int32, (ts, ts + 2 * HALO), 1)
    sel = jnp.concatenate(
        [jnp.where(cc == rr + HALO + s, 1.0, 0.0).astype(BF16) for s in shifts], axis=0)
    for b in range(nseq):
        prev = xp_ref[b]
        nxt = xn_ref[b]
        prev = jnp.where(tile == 0, jnp.zeros_like(prev), prev)
        nxt = jnp.where(tile == n_tiles - 1, jnp.zeros_like(nxt), nxt)
        cur = xc_ref[b]
        shifted = _dot(sel, jnp.concatenate([prev, cur, nxt], axis=0))
        acc = cb + cur.astype(F32) * cw[CONV_LEFT:CONV_LEFT + 1]
        for n, s in enumerate(shifts):
            k = s + CONV_LEFT
            acc = acc + shifted[n * ts:(n + 1) * ts] * cw[k:k + 1]
        conv_scr[b * ts:(b + 1) * ts, :] = acc
        convb_scr[b * ts:(b + 1) * ts, :] = acc.astype(BF16)

    for j in range(N_CT):
        st = GATE_STARTS[j]
        sl = slice(LANES * j, LANES * (j + 1))
        pre = _dot(convb_scr[:, st:st + GATE_WIDTH], wp_ref[j])
        tr = jnp.tanh(pre[:, :LANES] + ba_ref[:, sl])
        ti = jnp.tanh(pre[:, LANES:] + bx_ref[:, sl])
        z = -lam_ref[:, sl]
        softplus = jnp.maximum(z, 0.0) + jnp.log(1.0 + jnp.exp(-jnp.abs(z)))
        half = (-0.5 * LRU_C * LOG2E) * softplus
        a = jnp.exp2(half + half * tr)
        y = 1.0 - a * a
        root = jnp.where(y > 0.0, y * lax.rsqrt(y), 0.0)
        hx = 0.5 * conv_scr[:, sl]
        u = root * (hx + hx * ti)
        for b in range(nseq):
            a_scr[j, pl.ds(b, ts, stride=SUBLANES), :] = a[b * ts:(b + 1) * ts]
            u_scr[j, pl.ds(b, ts, stride=SUBLANES), :] = u[b * ts:(b + 1) * ts]

    def step(k, h):
        t = (ts - 1 - k) if reverse else k
        row = pl.multiple_of(t * SUBLANES, SUBLANES)
        h = a_scr[:, pl.ds(row, SUBLANES), :] * h + u_scr[:, pl.ds(row, SUBLANES), :]
        u_scr[:, pl.ds(row, SUBLANES), :] = h
        return h

    h_scr[...] = lax.fori_loop(0, ts, step, h_scr[...], unroll=8)

    for b in range(nseq):
        for j in range(N_CT):
            sl = slice(LANES * j, LANES * (j + 1))
            y = u_scr[j, pl.ds(b, ts, stride=SUBLANES), :]
            if reverse:
                y = gate_ref[b, :, sl].astype(F32) * (y + hsf_ref[b, :, sl].astype(F32))
            o_ref[b, :, sl] = y.astype(o_ref.dtype)


def _rglru_scan(y_in, conv_w, conv_b, wp, b_a, b_x, lam, hs_fwd, *, ts, nseq, reverse):
    n_tiles = SEQ // ts
    per_tile = ts // HALO

    def tmap(i):
        return (n_tiles - 1 - i) if reverse else i

    in_specs = [
        pl.BlockSpec((nseq, HALO, D_RNN), lambda i: (0, jnp.maximum(tmap(i) * per_tile - 1, 0), 1)),
        pl.BlockSpec((nseq, ts, D_RNN), lambda i: (0, tmap(i), 1)),
        pl.BlockSpec((nseq, HALO, D_RNN),
                     lambda i: (0, jnp.minimum((tmap(i) + 1) * per_tile, SEQ // HALO - 1), 1)),
        pl.BlockSpec((CONV_WIDTH, D_RNN), lambda i: (0, 0)),
        pl.BlockSpec((1, D_RNN), lambda i: (0, 0)),
        pl.BlockSpec((N_CT, GATE_WIDTH, 2 * LANES), lambda i: (0, 0, 0)),
        pl.BlockSpec((1, D_RNN), lambda i: (0, 0)),
        pl.BlockSpec((1, D_RNN), lambda i: (0, 0)),
        pl.BlockSpec((1, D_RNN), lambda i: (0, 0)),
    ]
    args = [y_in, y_in, y_in, conv_w, conv_b, wp, b_a, b_x, lam]
    if reverse:
        in_specs += [
            pl.BlockSpec((nseq, ts, D_RNN), lambda i: (0, tmap(i), 0)),
            pl.BlockSpec((nseq, ts, D_RNN), lambda i: (0, tmap(i), 0)),
        ]
        args += [hs_fwd, y_in]
    return pl.pallas_call(
        functools.partial(_scan_kernel, ts=ts, n_tiles=n_tiles, nseq=nseq, reverse=reverse),
        grid=(n_tiles,),
        in_specs=in_specs,
        out_specs=pl.BlockSpec((nseq, ts, D_RNN), lambda i: (0, tmap(i), 0)),
        out_shape=jax.ShapeDtypeStruct((nseq, SEQ, D_RNN), BF16),
        scratch_shapes=[
            pltpu.VMEM((nseq * ts, D_RNN), F32),
            pltpu.VMEM((nseq * ts, D_RNN), BF16),
            pltpu.VMEM((N_CT, ts * SUBLANES, LANES), F32),
            pltpu.VMEM((N_CT, ts * SUBLANES, LANES), F32),
            pltpu.VMEM((N_CT, SUBLANES, LANES), F32),
        ],
        compiler_params=_params(("arbitrary",)),
        name="rglru_scan_bwd" if reverse else "rglru_scan_fwd",
    )(*args)


def kernel(x_prompt, x_sample, mem_prompt, mem_sample, norms, rel_bias, a_w_in, a_ln_g, a_ln_b, a_w_s, a_b_s, a_w_out, b_w_qkv, b_sink, b_w_o, c_w_in, c_conv_w, c_conv_b, c_w_a, c_b_a, c_w_x, c_b_x, c_lam, c_w_out, m_w_q, m_w_kv, m_w_o, f_w_gate_up, f_w_down):
    n_prompt = x_prompt.shape[0]
    nseq = n_prompt + x_sample.shape[0]
    assert nseq <= SUBLANES
    t = nseq * SEQ
    t_prompt = n_prompt * SEQ
    x = None
    x_parts = [x_prompt.reshape(t_prompt, D_MODEL), x_sample.reshape(t - t_prompt, D_MODEL)]
    mem = jnp.concatenate([mem_prompt, mem_sample], axis=0)

    def gain(i, k):
        return norms[i, k].reshape(1, D_MODEL)

    a_w_in_b, a_w_out_b = a_w_in.astype(BF16), a_w_out.astype(BF16)
    q_cols = N_HEADS * HEAD_DIM
    b_w_qkv_b = jnp.concatenate(
        [b_w_qkv[:, :, :q_cols] * (HEAD_DIM ** -0.5 * LOG2E), b_w_qkv[:, :, q_cols:]],
        axis=-1).astype(BF16)
    b_w_o_b = b_w_o.astype(BF16)
    c_w_in_b, c_w_out_b = c_w_in.astype(BF16), c_w_out.astype(BF16)
    m_w_q_b, m_w_o_b = m_w_q.astype(BF16), m_w_o.astype(BF16)
    f_w_gate_up_b, f_w_down_b = f_w_gate_up.astype(BF16), f_w_down.astype(BF16)

    kv_all = _kv_proj(mem, norms[:, 4].reshape(DEPTH, 1, D_MODEL), m_w_kv.astype(BF16))

    for i in range(DEPTH):
        kind, j = i % N_MIXERS, i // N_MIXERS
        if kind == 0:
            parts = x_parts if x is None else [x]
            z, mu, rstd = _gmlp_in(parts, gain(i, 0), a_w_in_b, j, tm=1024, tn=2 * GMLP_GDIM)
            x = _gmlp_out(z, mu, rstd, a_ln_g[j].reshape(GMLP_GROUPS, 1, GMLP_GDIM),
                          a_ln_b[j].reshape(GMLP_GROUPS, 1, GMLP_GDIM),
                          a_w_s[j].astype(BF16), a_b_s[j].reshape(GMLP_GROUPS, BLOCK, 1),
                          a_w_out_b, j, parts, gain(i, 1), tm=512)
        elif kind == 1:
            qkv = _norm_matmul(x, gain(i, 0), b_w_qkv_b, j, tm=1024, tn=1024)
            o = _win_attention(qkv, b_sink[j] * LOG2E, _band_bias(rel_bias), nseq=nseq, r_blocks=4)
            x = _matmul_norm_res(o, b_w_o_b, j, x, gain(i, 1), tm=512)
        else:
            y_in = _norm_matmul(x, gain(i, 0), c_w_in_b, j, tm=1024, tn=D_RNN // 2, n_gelu=2)
            y_in = y_in.reshape(nseq, SEQ, 2 * D_RNN)
            scan_args = (c_conv_w[j], c_conv_b[j].reshape(1, D_RNN))
            hs = None
            for d in range(2):
                hs = _rglru_scan(y_in, *scan_args, _gate_weights(c_w_a[j, d], c_w_x[j, d]),
                                 (c_b_a[j, d] * 0.5).reshape(1, D_RNN),
                                 (c_b_x[j, d] * 0.5).reshape(1, D_RNN),
                                 c_lam[j, d].reshape(1, D_RNN), hs, ts=64, nseq=nseq,
                                 reverse=(d == 1))
            x = _matmul_norm_res(hs.reshape(t, D_RNN), c_w_out_b, j, x, gain(i, 1), tm=512)
        x = _mem_attention(x, gain(i, 2), gain(i, 3), m_w_q_b, kv_all, m_w_o_b, i, tm=512)
        ffn = functools.partial(_ffn, x, gain(i, 5), gain(i, 6), f_w_gate_up_b, f_w_down_b, i,
                                tm=1024, tn=512)
        if i < DEPTH - 1:
            x = ffn()

    y_prompt = ffn(row0=0, rows=t_prompt).reshape(n_prompt, SEQ, D_MODEL)
    y_sample = ffn(row0=t_prompt, rows=t - t_prompt).reshape(nseq - n_prompt, SEQ, D_MODEL)
    return (y_prompt, y_sample)
```

```python
import functools

import numpy as np
import jax
import jax.numpy as jnp
from jax import lax
from jax.experimental import pallas as pl
from jax.experimental.pallas import tpu as pltpu

D_MODEL = 2048
SEQ = 4096
DEPTH = 4
N_MIXERS = 3
BLOCK = 128
HEAD_DIM = 128
EPS = 1e-6
NEG_INF = -1e30
GMLP_HALF = 2 * D_MODEL
GMLP_GROUPS = 8
GMLP_GDIM = GMLP_HALF // GMLP_GROUPS
N_HEADS = D_MODEL // HEAD_DIM
N_KV_HEADS = 4
KV_GROUP = N_HEADS // N_KV_HEADS
WINDOW = 128
N_BUCKETS = 32
MAX_DISTANCE = 128
D_RNN = 2816
RNN_BLOCKS = 16
RNN_BDIM = D_RNN // RNN_BLOCKS
CONV_WIDTH = 4
CONV_LEFT = 2
LRU_C = 8.0
MEM_LEN = 256
MEM_HEADS = 4
MEM_WIDTH = MEM_HEADS * HEAD_DIM
D_FF = 5632

LANES = 128
SUBLANES = 8
VMEM_LIMIT = 56 * 1024 * 1024

F32 = jnp.float32
BF16 = jnp.bfloat16
LOG2E = float(np.log2(np.e))


def _params(sem):
    return pltpu.CompilerParams(dimension_semantics=sem, vmem_limit_bytes=VMEM_LIMIT)


def _rms(x, g):
    ms = jnp.mean(x * x, axis=-1, keepdims=True)
    return x * lax.rsqrt(ms + EPS) * g


def _gelu(x):
    k1 = float(-2.0 * np.sqrt(2.0 / np.pi) * LOG2E)
    return x / (1.0 + jnp.exp2(x * (k1 + (k1 * 0.044715) * (x * x))))


def _sigmoid(x):
    return 1.0 / (1.0 + jnp.exp(-x))


def _dot(a, b):
    return jnp.dot(a, b, preferred_element_type=F32)


def _norm_mm_kernel(x_ref, g_ref, w_ref, o_ref, h_scr, *, n_gelu):
    j = pl.program_id(1)

    @pl.when(j == 0)
    def _():
        h_scr[...] = _rms(x_ref[...], g_ref[...]).astype(BF16)

    def emit(act):
        z = _dot(h_scr[...], w_ref[...])
        o_ref[...] = (_gelu(z) if act else z).astype(o_ref.dtype)

    if n_gelu == 0:
        emit(False)
    else:
        pl.when(j < n_gelu)(lambda: emit(True))
        pl.when(j >= n_gelu)(lambda: emit(False))


def _norm_matmul(x, g, w, layer, *, tm, tn, n_gelu=0):
    t, d = x.shape
    n = w.shape[2]
    return pl.pallas_call(
        functools.partial(_norm_mm_kernel, n_gelu=n_gelu),
        grid=(t // tm, n // tn),
        in_specs=[
            pl.BlockSpec((tm, d), lambda i, j: (i, 0)),
            pl.BlockSpec((1, d), lambda i, j: (0, 0)),
            pl.BlockSpec((None, d, tn), lambda i, j: (layer, 0, j)),
        ],
        out_specs=pl.BlockSpec((tm, tn), lambda i, j: (i, j)),
        out_shape=jax.ShapeDtypeStruct((t, n), BF16),
        scratch_shapes=[pltpu.VMEM((tm, d), BF16)],
        compiler_params=_params(("parallel", "arbitrary")),
        name="norm_matmul",
    )(x, g, w)


def _part_specs(parts, tm, tile_of):
    specs, ranges, lo = [], [], 0
    for p in parts:
        n = p.shape[0] // tm
        specs.append(pl.BlockSpec(
            (tm, p.shape[1]),
            lambda *idx, lo=lo, n=n: (jnp.clip(tile_of(*idx) - lo, 0, n - 1), 0)))
        ranges.append((lo, lo + n))
        lo += n
    return specs, tuple(ranges)


def _for_owner(i, ranges, refs, cond, fn):
    for (lo, hi), ref in zip(ranges, refs):
        pl.when(cond & (i >= lo) & (i < hi))(functools.partial(fn, ref))


MXU_COLS = 256


def _gmlp_in_kernel(*refs, ranges, n_col, tn):
    n_parts = len(ranges)
    x_refs = refs[:n_parts]
    g_ref, w_ref, z_ref, mu_ref, rs_ref, h_scr, s1_scr, s2_scr = refs[n_parts:]
    i, j = pl.program_id(0), pl.program_id(1)
    tm = h_scr.shape[0]

    def prologue(x_ref):
        h_scr[...] = _rms(x_ref[...], g_ref[...]).astype(BF16)
        s1_scr[...] = jnp.zeros_like(s1_scr)
        s2_scr[...] = jnp.zeros_like(s2_scr)

    _for_owner(i, ranges, x_refs, j == 0, prologue)

    p1 = jnp.zeros((tm, LANES), F32)
    p2 = jnp.zeros((tm, LANES), F32)
    per_tile = GMLP_GDIM // MXU_COLS
    for c in range(tn // MXU_COLS):
        cols = slice(c * MXU_COLS, (c + 1) * MXU_COLS)
        zc = _gelu(_dot(h_scr[...], w_ref[:, cols]))
        lo = (c % per_tile) * MXU_COLS
        z_ref[c // per_tile, :, lo:lo + MXU_COLS] = zc.astype(z_ref.dtype)
        for k in range(MXU_COLS // LANES):
            zk = zc[:, k * LANES:(k + 1) * LANES]
            p1 = p1 + zk
            p2 = p2 + zk * zk

    first_v = j == n_col // 2
    s1 = jnp.where(first_v, p1, s1_scr[...] + p1)
    s2 = jnp.where(first_v, p2, s2_scr[...] + p2)
    s1_scr[...] = s1
    s2_scr[...] = s2

    @pl.when(j == n_col - 1)
    def _():
        mu = jnp.sum(s1, axis=-1, keepdims=True) * (1.0 / GMLP_HALF)
        msq = jnp.sum(s2, axis=-1, keepdims=True) * (1.0 / GMLP_HALF)
        mu_ref[...] = mu
        rs_ref[...] = lax.rsqrt(jnp.maximum(msq - mu * mu, 0.0) + EPS)


def _gmlp_in(x_parts, g, w, layer, *, tm, tn):
    t = sum(p.shape[0] for p in x_parts)
    d = x_parts[0].shape[1]
    n_col = 2 * GMLP_HALF // tn
    sub = tn // GMLP_GDIM
    x_specs, ranges = _part_specs(x_parts, tm, lambda i, j: i)
    return pl.pallas_call(
        functools.partial(_gmlp_in_kernel, ranges=ranges, n_col=n_col, tn=tn),
        grid=(t // tm, n_col),
        in_specs=x_specs + [
            pl.BlockSpec((1, d), lambda i, j: (0, 0)),
            pl.BlockSpec((None, d, tn), lambda i, j: (layer, 0, j)),
        ],
        out_specs=[
            pl.BlockSpec((sub, tm, GMLP_GDIM), lambda i, j: (j, i, 0)),
            pl.BlockSpec((tm, 1), lambda i, j: (i, 0)),
            pl.BlockSpec((tm, 1), lambda i, j: (i, 0)),
        ],
        out_shape=[
            jax.ShapeDtypeStruct((2 * GMLP_GROUPS, t, GMLP_GDIM), BF16),
            jax.ShapeDtypeStruct((t, 1), F32),
            jax.ShapeDtypeStruct((t, 1), F32),
        ],
        scratch_shapes=[
            pltpu.VMEM((tm, d), BF16),
            pltpu.VMEM((tm, LANES), F32),
            pltpu.VMEM((tm, LANES), F32),
        ],
        compiler_params=_params(("parallel", "arbitrary")),
        name="gmlp_in",
    )(*x_parts, g, w)


def _mm_norm_res_kernel(a_ref, w_ref, x_ref, g_ref, o_ref):
    y = _dot(a_ref[...], w_ref[...])
    o_ref[...] = x_ref[...] + _rms(y, g_ref[...])


def _matmul_norm_res(a, w, layer, x, g, *, tm):
    t, k = a.shape
    d = w.shape[2]
    return pl.pallas_call(
        _mm_norm_res_kernel,
        grid=(t // tm,),
        in_specs=[
            pl.BlockSpec((tm, k), lambda i: (i, 0)),
            pl.BlockSpec((None, k, d), lambda i: (layer, 0, 0), pipeline_mode=pl.Buffered(1)),
            pl.BlockSpec((tm, d), lambda i: (i, 0)),
            pl.BlockSpec((1, d), lambda i: (0, 0)),
        ],
        out_specs=pl.BlockSpec((tm, d), lambda i: (i, 0)),
        out_shape=jax.ShapeDtypeStruct((t, d), F32),
        compiler_params=_params(("parallel",)),
        name="matmul_norm_res",
    )(a, w, x, g)


PAIRS = GMLP_GROUPS // 2


def _gmlp_out_kernel(*refs, ranges, tm):
    n_parts = len(ranges)
    (u1_ref, v1_ref, mu1_ref, rs1_ref, lng1_ref, lnb1_ref, ws1_ref, bs1_ref,
     u2_ref, v2_ref, mu2_ref, rs2_ref, lng2_ref, lnb2_ref, ws2_ref, bs2_ref,
     wo_ref, g_ref) = refs[:18]
    x_refs = refs[18:18 + n_parts]
    o_ref, a0_scr, a1_scr = refs[18 + n_parts:]
    s = pl.program_id(0)
    q = jnp.maximum(s - 1, 0)
    tile, pair = q // PAIRS, q % PAIRS

    @pl.when(s == 0)
    def _():
        a0_scr[...] = jnp.zeros_like(a0_scr)
        a1_scr[...] = jnp.zeros_like(a1_scr)

    @pl.when(pair == 0)
    def _():
        o_ref[...] = jnp.zeros_like(o_ref)

    def prep(u_ref, v_ref, mu_ref, rs_ref, lng_ref, lnb_ref, ws_ref, bs_ref):
        vn = ((v_ref[0].astype(F32) - mu_ref[...]) * rs_ref[...] * lng_ref[0]
              + lnb_ref[0]).astype(BF16)
        ws = ws_ref[0]
        bs = bs_ref[0]
        parts = []
        for n in range(tm // BLOCK):
            rows = slice(n * BLOCK, (n + 1) * BLOCK)
            mixed = _dot(ws, vn[rows, :]) + bs
            parts.append((u_ref[0, rows, :].astype(F32) * mixed).astype(BF16))
        return jnp.concatenate(parts, axis=0)

    o_ref[...] += _dot(a0_scr[...], wo_ref[:GMLP_GDIM])
    a1_scr[...] = prep(u1_ref, v1_ref, mu1_ref, rs1_ref, lng1_ref, lnb1_ref, ws1_ref, bs1_ref)
    o_ref[...] += _dot(a1_scr[...], wo_ref[GMLP_GDIM:])
    a0_scr[...] = prep(u2_ref, v2_ref, mu2_ref, rs2_ref, lng2_ref, lnb2_ref, ws2_ref, bs2_ref)

    def epilogue(x_ref):
        o_ref[...] = x_ref[...] + _rms(o_ref[...], g_ref[...])

    _for_owner(tile, ranges, x_refs, (pair == PAIRS - 1) & (s > 0), epilogue)


def _gmlp_out(z, mu, rstd, ln_g, ln_b, w_s, b_s, w_out, layer, x_parts, g, *, tm):
    t = sum(p.shape[0] for p in x_parts)
    d = x_parts[0].shape[1]
    n_tiles = t // tm
    last = GMLP_GROUPS * n_tiles - 1

    def proj_tile(s):
        return jnp.maximum(s - 1, 0) // PAIRS

    def prep_specs(pair_of):
        def at(f):
            return lambda s: f(*pair_of(s))
        return [
            pl.BlockSpec((1, tm, GMLP_GDIM), at(lambda i, c: (c, i, 0))),
            pl.BlockSpec((1, tm, GMLP_GDIM), at(lambda i, c: (GMLP_GROUPS + c, i, 0))),
            pl.BlockSpec((tm, 1), at(lambda i, c: (i, 0))),
            pl.BlockSpec((tm, 1), at(lambda i, c: (i, 0))),
            pl.BlockSpec((1, 1, GMLP_GDIM), at(lambda i, c: (c, 0, 0))),
            pl.BlockSpec((1, 1, GMLP_GDIM), at(lambda i, c: (c, 0, 0))),
            pl.BlockSpec((1, BLOCK, BLOCK), at(lambda i, c: (c, 0, 0))),
            pl.BlockSpec((1, BLOCK, 1), at(lambda i, c: (c, 0, 0))),
        ]

    def split(gidx):
        gidx = jnp.clip(gidx, 0, last)
        return gidx // GMLP_GROUPS, gidx % GMLP_GROUPS

    x_specs, ranges = _part_specs(x_parts, tm, proj_tile)
    prep_args = (z, z, mu, rstd, ln_g, ln_b, w_s, b_s)
    return pl.pallas_call(
        functools.partial(_gmlp_out_kernel, ranges=ranges, tm=tm),
        grid=(PAIRS * n_tiles + 1,),
        in_specs=prep_specs(lambda s: split(2 * s - 1)) + prep_specs(lambda s: split(2 * s)) + [
            pl.BlockSpec((None, 2 * GMLP_GDIM, d),
                         lambda s: (layer, jnp.maximum(s - 1, 0) % PAIRS, 0)),
            pl.BlockSpec((1, d), lambda s: (0, 0)),
        ] + x_specs,
        out_specs=pl.BlockSpec((tm, d), lambda s: (proj_tile(s), 0)),
        out_shape=jax.ShapeDtypeStruct((t, d), F32),
        scratch_shapes=[
            pltpu.VMEM((tm, GMLP_GDIM), BF16),
            pltpu.VMEM((tm, GMLP_GDIM), BF16),
        ],
        compiler_params=_params(("arbitrary",)),
        name="gmlp_out",
    )(*prep_args, *prep_args, w_out, g, *x_parts)


def _win_attn_kernel(sink_ref, q_ref, kp_ref, kc_ref, kn_ref, vp_ref, vc_ref, vn_ref, bias_ref,
                     o_ref, *, r_blocks, n_steps):
    i = pl.program_id(1)
    kall = jnp.concatenate([kp_ref[0], kc_ref[0], kn_ref[0]], axis=0)
    vall = jnp.concatenate([vp_ref[0], vc_ref[0], vn_ref[0]], axis=0)
    col = lax.broadcasted_iota(jnp.int32, (1, 3 * BLOCK), 1)
    sk = jnp.concatenate(
        [jnp.full((BLOCK, 1), sink_ref[h], F32) for h in range(N_HEADS)], axis=0)
    hrows = KV_GROUP * BLOCK
    for r in range(r_blocks):
        rows = slice(r * BLOCK, (r + 1) * BLOCK)
        oob = None
        if r == 0:
            oob = (col < BLOCK) & (i == 0)
        if r == r_blocks - 1:
            hi = (col >= 2 * BLOCK) & (i == n_steps - 1)
            oob = hi if oob is None else (oob | hi)
        parts = []
        for kh in range(N_KV_HEADS):
            kb = kall[r * BLOCK:(r + 3) * BLOCK, kh * HEAD_DIM:(kh + 1) * HEAD_DIM]
            qs = jnp.concatenate(
                [q_ref[0, rows, h * HEAD_DIM:(h + 1) * HEAD_DIM]
                 for h in range(kh * KV_GROUP, (kh + 1) * KV_GROUP)], axis=0)
            parts.append(
                lax.dot_general(qs, kb, (((1,), (1,)), ((), ())), preferred_element_type=F32))
        s = jnp.concatenate(parts, axis=0) + bias_ref[...]
        if oob is not None:
            s = jnp.where(oob, NEG_INF, s)
        m = jnp.maximum(jnp.max(s, axis=-1, keepdims=True), sk)
        p = jnp.exp2(s - m)
        inv = 1.0 / (jnp.sum(p, axis=-1, keepdims=True) + jnp.exp2(sk - m))
        pb = p.astype(BF16)
        for kh in range(N_KV_HEADS):
            vb = vall[r * BLOCK:(r + 3) * BLOCK, kh * HEAD_DIM:(kh + 1) * HEAD_DIM]
            hsl = slice(kh * hrows, (kh + 1) * hrows)
            o = _dot(pb[hsl], vb) * inv[hsl]
            for gq in range(KV_GROUP):
                h = kh * KV_GROUP + gq
                o_ref[0, rows, h * HEAD_DIM:(h + 1) * HEAD_DIM] = (
                    o[gq * BLOCK:(gq + 1) * BLOCK].astype(o_ref.dtype))


def _win_attention(qkv, sink, bias, *, nseq, r_blocks):
    tq = r_blocks * BLOCK
    n_steps = SEQ // tq
    nb = SEQ // BLOCK
    qkv3 = qkv.reshape(nseq, SEQ, qkv.shape[-1])
    kcol = N_HEADS * HEAD_DIM // (N_KV_HEADS * HEAD_DIM)
    vcol = kcol + 1
    kvw = N_KV_HEADS * HEAD_DIM

    def prev_map(c):
        return lambda b, i: (b, jnp.maximum(i * r_blocks - 1, 0), c)

    def cur_map(c):
        return lambda b, i: (b, i, c)

    def next_map(c):
        return lambda b, i: (b, jnp.minimum((i + 1) * r_blocks, nb - 1), c)

    out = pl.pallas_call(
        functools.partial(_win_attn_kernel, r_blocks=r_blocks, n_steps=n_steps),
        grid=(nseq, n_steps),
        in_specs=[
            pl.BlockSpec(memory_space=pltpu.SMEM),
            pl.BlockSpec((1, tq, N_HEADS * HEAD_DIM), lambda b, i: (b, i, 0)),
            pl.BlockSpec((1, BLOCK, kvw), prev_map(kcol)),
            pl.BlockSpec((1, tq, kvw), cur_map(kcol)),
            pl.BlockSpec((1, BLOCK, kvw), next_map(kcol)),
            pl.BlockSpec((1, BLOCK, kvw), prev_map(vcol)),
            pl.BlockSpec((1, tq, kvw), cur_map(vcol)),
            pl.BlockSpec((1, BLOCK, kvw), next_map(vcol)),
            pl.BlockSpec((N_HEADS * BLOCK, 3 * BLOCK), lambda b, i: (0, 0)),
        ],
        out_specs=pl.BlockSpec((1, tq, N_HEADS * HEAD_DIM), lambda b, i: (b, i, 0)),
        out_shape=jax.ShapeDtypeStruct((nseq, SEQ, N_HEADS * HEAD_DIM), BF16),
        compiler_params=_params(("parallel", "parallel")),
        name="win_attention",
    )(sink, qkv3, qkv3, qkv3, qkv3, qkv3, qkv3, qkv3, bias)
    return out.reshape(nseq * SEQ, N_HEADS * HEAD_DIM)


def _t5_bucket(rel):
    half = N_BUCKETS // 2
    exact = half // 2
    n = np.abs(rel)
    large = exact + (np.log(np.maximum(n, 1) / exact) / np.log(MAX_DISTANCE / exact)
                     * (half - exact)).astype(np.int32)
    large = np.minimum(large, half - 1)
    return (np.where(rel > 0, half, 0) + np.where(n < exact, n, large)).astype(np.int32)


def _band_bias(rel_table):
    rel = np.arange(3 * BLOCK)[None, :] - BLOCK - np.arange(BLOCK)[:, None]
    onehot = (_t5_bucket(rel).reshape(1, -1) == np.arange(N_BUCKETS)[:, None]).astype(np.float32)
    bias = jnp.einsum("bh,bn->hn", rel_table.astype(F32) * LOG2E, onehot,
                      precision=lax.Precision.HIGHEST)
    bias = jnp.where((np.abs(rel) <= WINDOW).reshape(1, -1), bias, NEG_INF)
    return bias.reshape(N_HEADS * BLOCK, 3 * BLOCK)


def _kv_kernel(mem_ref, g_ref, w_ref, o_ref):
    h = _rms(mem_ref[0], g_ref[0]).astype(BF16)
    o_ref[0, 0] = _dot(h, w_ref[0]).astype(o_ref.dtype)


def _kv_proj(mem, g, w_kv):
    nseq = mem.shape[0]
    return pl.pallas_call(
        _kv_kernel,
        grid=(DEPTH, nseq),
        in_specs=[
            pl.BlockSpec((1, MEM_LEN, D_MODEL), lambda l, b: (b, 0, 0)),
            pl.BlockSpec((1, 1, D_MODEL), lambda l, b: (l, 0, 0)),
            pl.BlockSpec((1, D_MODEL, 2 * MEM_WIDTH), lambda l, b: (l, 0, 0)),
        ],
        out_specs=pl.BlockSpec((1, 1, MEM_LEN, 2 * MEM_WIDTH), lambda l, b: (l, b, 0, 0)),
        out_shape=jax.ShapeDtypeStruct((DEPTH, nseq, MEM_LEN, 2 * MEM_WIDTH), BF16),
        compiler_params=_params(("parallel", "parallel")),
        name="kv_proj",
    )(mem, g, w_kv)


def _mem_attn_kernel(x_ref, gpre_ref, gpost_ref, wq_ref, kv_ref, wo_ref, o_ref):
    x = x_ref[...]
    h = _rms(x, gpre_ref[...]).astype(BF16)
    q = (_dot(h, wq_ref[...]) * (HEAD_DIM ** -0.5)).astype(BF16)
    outs = []
    for hd in range(MEM_HEADS):
        sl = slice(hd * HEAD_DIM, (hd + 1) * HEAD_DIM)
        k = kv_ref[0, :, sl]
        v = kv_ref[0, :, MEM_WIDTH + hd * HEAD_DIM:MEM_WIDTH + (hd + 1) * HEAD_DIM]
        s = lax.dot_general(q[:, sl], k, (((1,), (1,)), ((), ())), preferred_element_type=F32)
        m = jnp.max(s, axis=-1, keepdims=True)
        p = jnp.exp(s - m)
        denom = jnp.sum(p, axis=-1, keepdims=True)
        outs.append((_dot(p.astype(BF16), v) / denom).astype(BF16))
    y = _dot(jnp.concatenate(outs, axis=-1), wo_ref[...])
    o_ref[...] = x + _rms(y, gpost_ref[...])


def _mem_attention(x, gpre, gpost, w_q, kv, w_o, layer, *, tm):
    t, d = x.shape
    per_seq = SEQ // tm
    return pl.pallas_call(
        _mem_attn_kernel,
        grid=(t // tm,),
        in_specs=[
            pl.BlockSpec((tm, d), lambda i: (i, 0)),
            pl.BlockSpec((1, d), lambda i: (0, 0)),
            pl.BlockSpec((1, d), lambda i: (0, 0)),
            pl.BlockSpec((None, d, MEM_WIDTH), lambda i: (layer, 0, 0)),
            pl.BlockSpec((None, 1, MEM_LEN, 2 * MEM_WIDTH), lambda i: (layer, i // per_seq, 0, 0)),
            pl.BlockSpec((None, MEM_WIDTH, d), lambda i: (layer, 0, 0)),
        ],
        out_specs=pl.BlockSpec((tm, d), lambda i: (i, 0)),
        out_shape=jax.ShapeDtypeStruct((t, d), F32),
        compiler_params=_params(("parallel",)),
        name="mem_attention",
    )(x, gpre, gpost, w_q, kv, w_o)


def _ffn_kernel(x_ref, gpre_ref, gpost_ref, wg_ref, wu_ref, wd_ref, o_ref, h_scr, *, n_chunks):
    j = pl.program_id(1)

    @pl.when(j == 0)
    def _():
        h_scr[...] = _rms(x_ref[...], gpre_ref[...]).astype(BF16)
        o_ref[...] = jnp.zeros_like(o_ref)

    h = h_scr[...]
    gate = _dot(h, wg_ref[...])
    up = _dot(h, wu_ref[...])
    a = (gate * _sigmoid(gate) * up).astype(BF16)
    o_ref[...] += _dot(a, wd_ref[...])

    @pl.when(j == n_chunks - 1)
    def _():
        o_ref[...] = x_ref[...] + _rms(o_ref[...], gpost_ref[...])


def _ffn(x, gpre, gpost, w_gate_up, w_down, layer, *, tm, tn, row0=0, rows=None):
    d = x.shape[1]
    t = x.shape[0] if rows is None else rows
    tile0 = row0 // tm
    n_chunks = D_FF // tn
    x_bytes = tm * d * 4
    fixed = 2 * x_bytes + tm * d * 2 + 2 * 3 * d * tn * 2
    temps = 2 * tm * tn * 4 + tm * tn * 2 + tm * d * 4
    x_mode = None if fixed + 2 * x_bytes + temps <= VMEM_LIMIT else pl.Buffered(1)
    return pl.pallas_call(
        functools.partial(_ffn_kernel, n_chunks=n_chunks),
        grid=(t // tm, n_chunks),
        in_specs=[
            pl.BlockSpec((tm, d), lambda i, j: (i + tile0, 0), pipeline_mode=x_mode),
            pl.BlockSpec((1, d), lambda i, j: (0, 0)),
            pl.BlockSpec((1, d), lambda i, j: (0, 0)),
            pl.BlockSpec((None, d, tn), lambda i, j: (layer, 0, j)),
            pl.BlockSpec((None, d, tn), lambda i, j: (layer, 0, j + n_chunks)),
            pl.BlockSpec((None, tn, d), lambda i, j: (layer, j, 0)),
        ],
        out_specs=pl.BlockSpec((tm, d), lambda i, j: (i, 0)),
        out_shape=jax.ShapeDtypeStruct((t, d), F32),
        scratch_shapes=[pltpu.VMEM((tm, d), BF16)],
        compiler_params=_params(("parallel", "arbitrary")),
        name="ffn",
    )(x, gpre, gpost, w_gate_up, w_gate_up, w_down)


N_CT = D_RNN // LANES
HALO = 2 * SUBLANES


def _gate_windows():
    lo_hi = []
    for j in range(N_CT):
        h0 = (LANES * j) // RNN_BDIM
        h1 = (LANES * j + LANES - 1) // RNN_BDIM
        lo = (RNN_BDIM * h0) // LANES * LANES
        hi = -(-(RNN_BDIM * (h1 + 1)) // LANES) * LANES
        lo_hi.append((lo, hi))
    width = max(hi - lo for lo, hi in lo_hi)
    starts = tuple(min(lo, D_RNN - width) for lo, _ in lo_hi)
    return starts, width


GATE_STARTS, GATE_WIDTH = _gate_windows()


def _gate_weights(w_a, w_x):
    def tile(w, j):
        st = GATE_STARTS[j]
        pieces = []
        for h in range((LANES * j) // RNN_BDIM, (LANES * j + LANES - 1) // RNN_BDIM + 1):
            c0 = max(LANES * j, RNN_BDIM * h) - RNN_BDIM * h
            c1 = min(LANES * (j + 1), RNN_BDIM * (h + 1)) - RNN_BDIM * h
            top = RNN_BDIM * h - st
            pieces.append(jnp.pad(w[h, :, c0:c1], ((top, GATE_WIDTH - RNN_BDIM - top), (0, 0))))
        return jnp.concatenate(pieces, axis=1)

    wa, wx = (w_a * 0.5).astype(BF16), (w_x * 0.5).astype(BF16)
    return jnp.stack([jnp.concatenate([tile(wa, j), tile(wx, j)], axis=1) for j in range(N_CT)])


def _scan_kernel(*refs, ts, n_tiles, nseq, reverse):
    if reverse:
        (xp_ref, xc_ref, xn_ref, cw_ref, cb_ref, wp_ref, ba_ref, bx_ref, lam_ref,
         hsf_ref, gate_ref, o_ref, conv_scr, convb_scr, a_scr, u_scr, h_scr) = refs
    else:
        (xp_ref, xc_ref, xn_ref, cw_ref, cb_ref, wp_ref, ba_ref, bx_ref, lam_ref,
         o_ref, conv_scr, convb_scr, a_scr, u_scr, h_scr) = refs
    i = pl.program_id(0)
    tile = (n_tiles - 1 - i) if reverse else i

    @pl.when(i == 0)
    def _():
        h_scr[...] = jnp.zeros_like(h_scr)
        a_scr[...] = jnp.zeros_like(a_scr)
        u_scr[...] = jnp.zeros_like(u_scr)

    cw = cw_ref[...]
    cb = cb_ref[...]
    shifts = [k - CONV_LEFT for k in range(CONV_WIDTH) if k != CONV_LEFT]
    rr = lax.broadcasted_iota(jnp.int32, (ts, ts + 2 * HALO), 0)
    cc = lax.broadcasted_iota(jnp.int32, (ts, ts + 2 * HALO), 1)
    sel = jnp.concatenate(
        [jnp.where(cc == rr + HALO + s, 1.0, 0.0).astype(BF16) for s in shifts], axis=0)
    for b in range(nseq):
        prev = xp_ref[b]
        nxt = xn_ref[b]
        prev = jnp.where(tile == 0, jnp.zeros_like(prev), prev)
        nxt = jnp.where(tile == n_tiles - 1, jnp.zeros_like(nxt), nxt)
        cur = xc_ref[b]
        shifted = _dot(sel, jnp.concatenate([prev, cur, nxt], axis=0))
        acc = cb + cur.astype(F32) * cw[CONV_LEFT:CONV_LEFT + 1]
        for n, s in enumerate(shifts):
            k = s + CONV_LEFT
            acc = acc + shifted[n * ts:(n + 1) * ts] * cw[k:k + 1]
        conv_scr[b * ts:(b + 1) * ts, :] = acc
        convb_scr[b * ts:(b + 1) * ts, :] = acc.astype(BF16)

    for j in range(N_CT):
        st = GATE_STARTS[j]
        sl = slice(LANES * j, LANES * (j + 1))
        pre = _dot(convb_scr[:, st:st + GATE_WIDTH], wp_ref[j])
        tr = jnp.tanh(pre[:, :LANES] + ba_ref[:, sl])
        ti = jnp.tanh(pre[:, LANES:] + bx_ref[:, sl])
        z = -lam_ref[:, sl]
        softplus = jnp.maximum(z, 0.0) + jnp.log(1.0 + jnp.exp(-jnp.abs(z)))
        half = (-0.5 * LRU_C * LOG2E) * softplus
        a = jnp.exp2(half + half * tr)
        y = 1.0 - a * a
        root = jnp.where(y > 0.0, y * lax.rsqrt(y), 0.0)
        hx = 0.5 * conv_scr[:, sl]
        u = root * (hx + hx * ti)
        for b in range(nseq):
            a_scr[j, pl.ds(b, ts, stride=SUBLANES), :] = a[b * ts:(b + 1) * ts]
            u_scr[j, pl.ds(b, ts, stride=SUBLANES), :] = u[b * ts:(b + 1) * ts]

    def step(k, h):
        t = (ts - 1 - k) if reverse else k
        row = pl.multiple_of(t * SUBLANES, SUBLANES)
        h = a_scr[:, pl.ds(row, SUBLANES), :] * h + u_scr[:, pl.ds(row, SUBLANES), :]
        u_scr[:, pl.ds(row, SUBLANES), :] = h
        return h

    h_scr[...] = lax.fori_loop(0, ts, step, h_scr[...], unroll=8)

    for b in range(nseq):
        for j in range(N_CT):
            sl = slice(LANES * j, LANES * (j + 1))
            y = u_scr[j, pl.ds(b, ts, stride=SUBLANES), :]
            if reverse:
                y = gate_ref[b, :, sl].astype(F32) * (y + hsf_ref[b, :, sl].astype(F32))
            o_ref[b, :, sl] = y.astype(o_ref.dtype)


def _rglru_scan(y_in, conv_w, conv_b, wp, b_a, b_x, lam, hs_fwd, *, ts, nseq, reverse):
    n_tiles = SEQ // ts
    per_tile = ts // HALO

    def tmap(i):
        return (n_tiles - 1 - i) if reverse else i

    in_specs = [
        pl.BlockSpec((nseq, HALO, D_RNN), lambda i: (0, jnp.maximum(tmap(i) * per_tile - 1, 0), 1)),
        pl.BlockSpec((nseq, ts, D_RNN), lambda i: (0, tmap(i), 1)),
        pl.BlockSpec((nseq, HALO, D_RNN),
                     lambda i: (0, jnp.minimum((tmap(i) + 1) * per_tile, SEQ // HALO - 1), 1)),
        pl.BlockSpec((CONV_WIDTH, D_RNN), lambda i: (0, 0)),
        pl.BlockSpec((1, D_RNN), lambda i: (0, 0)),
        pl.BlockSpec((N_CT, GATE_WIDTH, 2 * LANES), lambda i: (0, 0, 0)),
        pl.BlockSpec((1, D_RNN), lambda i: (0, 0)),
        pl.BlockSpec((1, D_RNN), lambda i: (0, 0)),
        pl.BlockSpec((1, D_RNN), lambda i: (0, 0)),
    ]
    args = [y_in, y_in, y_in, conv_w, conv_b, wp, b_a, b_x, lam]
    if reverse:
        in_specs += [
            pl.BlockSpec((nseq, ts, D_RNN), lambda i: (0, tmap(i), 0)),
            pl.BlockSpec((nseq, ts, D_RNN), lambda i: (0, tmap(i), 0)),
        ]
        args += [hs_fwd, y_in]
    return pl.pallas_call(
        functools.partial(_scan_kernel, ts=ts, n_tiles=n_tiles, nseq=nseq, reverse=reverse),
        grid=(n_tiles,),
        in_specs=in_specs,
        out_specs=pl.BlockSpec((nseq, ts, D_RNN), lambda i: (0, tmap(i), 0)),
        out_shape=jax.ShapeDtypeStruct((nseq, SEQ, D_RNN), BF16),
        scratch_shapes=[
            pltpu.VMEM((nseq * ts, D_RNN), F32),
            pltpu.VMEM((nseq * ts, D_RNN), BF16),
            pltpu.VMEM((N_CT, ts * SUBLANES, LANES), F32),
            pltpu.VMEM((N_CT, ts * SUBLANES, LANES), F32),
            pltpu.VMEM((N_CT, SUBLANES, LANES), F32),
        ],
        compiler_params=_params(("arbitrary",)),
        name="rglru_scan_bwd" if reverse else "rglru_scan_fwd",
    )(*args)


def kernel(x_prompt, x_sample, mem_prompt, mem_sample, norms, rel_bias, a_w_in, a_ln_g, a_ln_b, a_w_s, a_b_s, a_w_out, b_w_qkv, b_sink, b_w_o, c_w_in, c_conv_w, c_conv_b, c_w_a, c_b_a, c_w_x, c_b_x, c_lam, c_w_out, m_w_q, m_w_kv, m_w_o, f_w_gate_up, f_w_down):
    n_prompt = x_prompt.shape[0]
    nseq = n_prompt + x_sample.shape[0]
    assert nseq <= SUBLANES
    t = nseq * SEQ
    t_prompt = n_prompt * SEQ
    x = None
    x_parts = [x_prompt.reshape(t_prompt, D_MODEL), x_sample.reshape(t - t_prompt, D_MODEL)]
    mem = jnp.concatenate([mem_prompt, mem_sample], axis=0)

    def gain(i, k):
        return norms[i, k].reshape(1, D_MODEL)

    a_w_in_b, a_w_out_b = a_w_in.astype(BF16), a_w_out.astype(BF16)
    q_cols = N_HEADS * HEAD_DIM
    b_w_qkv_b = jnp.concatenate(
        [b_w_qkv[:, :, :q_cols] * (HEAD_DIM ** -0.5 * LOG2E), b_w_qkv[:, :, q_cols:]],
        axis=-1).astype(BF16)
    b_w_o_b = b_w_o.astype(BF16)
    c_w_in_b, c_w_out_b = c_w_in.astype(BF16), c_w_out.astype(BF16)
    m_w_q_b, m_w_o_b = m_w_q.astype(BF16), m_w_o.astype(BF16)
    f_w_gate_up_b, f_w_down_b = f_w_gate_up.astype(BF16), f_w_down.astype(BF16)

    kv_all = _kv_proj(mem, norms[:, 4].reshape(DEPTH, 1, D_MODEL), m_w_kv.astype(BF16))

    for i in range(DEPTH):
        kind, j = i % N_MIXERS, i // N_MIXERS
        if kind == 0:
            parts = x_parts if x is None else [x]
            z, mu, rstd = _gmlp_in(parts, gain(i, 0), a_w_in_b, j, tm=1024, tn=2 * GMLP_GDIM)
            x = _gmlp_out(z, mu, rstd, a_ln_g[j].reshape(GMLP_GROUPS, 1, GMLP_GDIM),
                          a_ln_b[j].reshape(GMLP_GROUPS, 1, GMLP_GDIM),
                          a_w_s[j].astype(BF16), a_b_s[j].reshape(GMLP_GROUPS, BLOCK, 1),
                          a_w_out_b, j, parts, gain(i, 1), tm=512)
        elif kind == 1:
            qkv = _norm_matmul(x, gain(i, 0), b_w_qkv_b, j, tm=1024, tn=1024)
            o = _win_attention(qkv, b_sink[j] * LOG2E, _band_bias(rel_bias), nseq=nseq, r_blocks=4)
            x = _matmul_norm_res(o, b_w_o_b, j, x, gain(i, 1), tm=512)
        else:
            y_in = _norm_matmul(x, gain(i, 0), c_w_in_b, j, tm=1024, tn=D_RNN // 2, n_gelu=2)
            y_in = y_in.reshape(nseq, SEQ, 2 * D_RNN)
            scan_args = (c_conv_w[j], c_conv_b[j].reshape(1, D_RNN))
            hs = None
            for d in range(2):
                hs = _rglru_scan(y_in, *scan_args, _gate_weights(c_w_a[j, d], c_w_x[j, d]),
                                 (c_b_a[j, d] * 0.5).reshape(1, D_RNN),
                                 (c_b_x[j, d] * 0.5).reshape(1, D_RNN),
                                 c_lam[j, d].reshape(1, D_RNN), hs, ts=64, nseq=nseq,
                                 reverse=(d == 1))
            x = _matmul_norm_res(hs.reshape(t, D_RNN), c_w_out_b, j, x, gain(i, 1), tm=512)
        x = _mem_attention(x, gain(i, 2), gain(i, 3), m_w_q_b, kv_all, m_w_o_b, i, tm=512)
        ffn = functools.partial(_ffn, x, gain(i, 5), gain(i, 6), f_w_gate_up_b, f_w_down_b, i,
                                tn=512)
        if i < DEPTH - 1:
            x = ffn(tm=768)

    y_prompt = ffn(tm=1024, row0=0, rows=t_prompt).reshape(n_prompt, SEQ, D_MODEL)
    y_sample = ffn(tm=1024, row0=t_prompt, rows=t - t_prompt).reshape(
        nseq - n_prompt, SEQ, D_MODEL)
    return (y_prompt, y_sample)
```

```python
import functools

import numpy as np
import jax
import jax.numpy as jnp
from jax import lax
from jax.experimental import pallas as pl
from jax.experimental.pallas import tpu as pltpu

D_MODEL = 2048
SEQ = 4096
DEPTH = 4
N_MIXERS = 3
BLOCK = 128
HEAD_DIM = 128
EPS = 1e-6
NEG_INF = -1e30
GMLP_HALF = 2 * D_MODEL
GMLP_GROUPS = 8
GMLP_GDIM = GMLP_HALF // GMLP_GROUPS
N_HEADS = D_MODEL // HEAD_DIM
N_KV_HEADS = 4
KV_GROUP = N_HEADS // N_KV_HEADS
WINDOW = 128
N_BUCKETS = 32
MAX_DISTANCE = 128
D_RNN = 2816
RNN_BLOCKS = 16
RNN_BDIM = D_RNN // RNN_BLOCKS
CONV_WIDTH = 4
CONV_LEFT = 2
LRU_C = 8.0
MEM_LEN = 256
MEM_HEADS = 4
MEM_WIDTH = MEM_HEADS * HEAD_DIM
D_FF = 5632

LANES = 128
SUBLANES = 8
VMEM_LIMIT = 56 * 1024 * 1024

F32 = jnp.float32
BF16 = jnp.bfloat16
LOG2E = float(np.log2(np.e))


def _params(sem):
    return pltpu.CompilerParams(dimension_semantics=sem, vmem_limit_bytes=VMEM_LIMIT)


def _rms(x, g):
    ms = jnp.mean(x * x, axis=-1, keepdims=True)
    return x * lax.rsqrt(ms + EPS) * g


def _gelu(x):
    k1 = float(-2.0 * np.sqrt(2.0 / np.pi) * LOG2E)
    return x / (1.0 + jnp.exp2(x * (k1 + (k1 * 0.044715) * (x * x))))


def _sigmoid(x):
    return 1.0 / (1.0 + jnp.exp(-x))


def _dot(a, b):
    return jnp.dot(a, b, preferred_element_type=F32)


def _norm_mm_kernel(x_ref, g_ref, w_ref, o_ref, h_scr, *, n_gelu):
    j = pl.program_id(1)

    @pl.when(j == 0)
    def _():
        h_scr[...] = _rms(x_ref[...], g_ref[...]).astype(BF16)

    def emit(act):
        z = _dot(h_scr[...], w_ref[...])
        o_ref[...] = (_gelu(z) if act else z).astype(o_ref.dtype)

    if n_gelu == 0:
        emit(False)
    else:
        pl.when(j < n_gelu)(lambda: emit(True))
        pl.when(j >= n_gelu)(lambda: emit(False))


def _norm_matmul(x, g, w, layer, *, tm, tn, n_gelu=0):
    t, d = x.shape
    n = w.shape[2]
    return pl.pallas_call(
        functools.partial(_norm_mm_kernel, n_gelu=n_gelu),
        grid=(t // tm, n // tn),
        in_specs=[
            pl.BlockSpec((tm, d), lambda i, j: (i, 0)),
            pl.BlockSpec((1, d), lambda i, j: (0, 0)),
            pl.BlockSpec((None, d, tn), lambda i, j: (layer, 0, j)),
        ],
        out_specs=pl.BlockSpec((tm, tn), lambda i, j: (i, j)),
        out_shape=jax.ShapeDtypeStruct((t, n), BF16),
        scratch_shapes=[pltpu.VMEM((tm, d), BF16)],
        compiler_params=_params(("parallel", "arbitrary")),
        name="norm_matmul",
    )(x, g, w)


def _part_specs(parts, tm, tile_of):
    specs, ranges, lo = [], [], 0
    for p in parts:
        n = p.shape[0] // tm
        specs.append(pl.BlockSpec(
            (tm, p.shape[1]),
            lambda *idx, lo=lo, n=n: (jnp.clip(tile_of(*idx) - lo, 0, n - 1), 0)))
        ranges.append((lo, lo + n))
        lo += n
    return specs, tuple(ranges)


def _for_owner(i, ranges, refs, cond, fn):
    for (lo, hi), ref in zip(ranges, refs):
        pl.when(cond & (i >= lo) & (i < hi))(functools.partial(fn, ref))


MXU_COLS = 256


def _gmlp_in_kernel(*refs, ranges, n_col, tn):
    n_parts = len(ranges)
    x_refs = refs[:n_parts]
    g_ref, w_ref, z_ref, mu_ref, rs_ref, h_scr, s1_scr, s2_scr = refs[n_parts:]
    i, j = pl.program_id(0), pl.program_id(1)
    tm = h_scr.shape[0]

    def prologue(x_ref):
        h_scr[...] = _rms(x_ref[...], g_ref[...]).astype(BF16)
        s1_scr[...] = jnp.zeros_like(s1_scr)
        s2_scr[...] = jnp.zeros_like(s2_scr)

    _for_owner(i, ranges, x_refs, j == 0, prologue)

    def step(is_v):
        p1 = jnp.zeros((tm, LANES), F32)
        p2 = jnp.zeros((tm, LANES), F32)
        per_tile = GMLP_GDIM // MXU_COLS
        for c in range(tn // MXU_COLS):
            cols = slice(c * MXU_COLS, (c + 1) * MXU_COLS)
            zc = _gelu(_dot(h_scr[...], w_ref[:, cols]))
            lo = (c % per_tile) * MXU_COLS
            z_ref[c // per_tile, :, lo:lo + MXU_COLS] = zc.astype(z_ref.dtype)
            if is_v:
                for k in range(MXU_COLS // LANES):
                    zk = zc[:, k * LANES:(k + 1) * LANES]
                    p1 = p1 + zk
                    p2 = p2 + zk * zk
        if is_v:
            s1_scr[...] += p1
            s2_scr[...] += p2

    pl.when(j < n_col // 2)(functools.partial(step, False))
    pl.when(j >= n_col // 2)(functools.partial(step, True))

    @pl.when(j == n_col - 1)
    def _():
        mu = jnp.sum(s1_scr[...], axis=-1, keepdims=True) * (1.0 / GMLP_HALF)
        msq = jnp.sum(s2_scr[...], axis=-1, keepdims=True) * (1.0 / GMLP_HALF)
        mu_ref[...] = mu
        rs_ref[...] = lax.rsqrt(jnp.maximum(msq - mu * mu, 0.0) + EPS)


def _gmlp_in(x_parts, g, w, layer, *, tm, tn):
    t = sum(p.shape[0] for p in x_parts)
    d = x_parts[0].shape[1]
    n_col = 2 * GMLP_HALF // tn
    sub = tn // GMLP_GDIM
    x_specs, ranges = _part_specs(x_parts, tm, lambda i, j: i)
    return pl.pallas_call(
        functools.partial(_gmlp_in_kernel, ranges=ranges, n_col=n_col, tn=tn),
        grid=(t // tm, n_col),
        in_specs=x_specs + [
            pl.BlockSpec((1, d), lambda i, j: (0, 0)),
            pl.BlockSpec((None, d, tn), lambda i, j: (layer, 0, j)),
        ],
        out_specs=[
            pl.BlockSpec((sub, tm, GMLP_GDIM), lambda i, j: (j, i, 0)),
            pl.BlockSpec((tm, 1), lambda i, j: (i, 0)),
            pl.BlockSpec((tm, 1), lambda i, j: (i, 0)),
        ],
        out_shape=[
            jax.ShapeDtypeStruct((2 * GMLP_GROUPS, t, GMLP_GDIM), BF16),
            jax.ShapeDtypeStruct((t, 1), F32),
            jax.ShapeDtypeStruct((t, 1), F32),
        ],
        scratch_shapes=[
            pltpu.VMEM((tm, d), BF16),
            pltpu.VMEM((tm, LANES), F32),
            pltpu.VMEM((tm, LANES), F32),
        ],
        compiler_params=_params(("parallel", "arbitrary")),
        name="gmlp_in",
    )(*x_parts, g, w)


def _mm_norm_res_kernel(a_ref, w_ref, x_ref, g_ref, o_ref):
    y = _dot(a_ref[...], w_ref[...])
    o_ref[...] = x_ref[...] + _rms(y, g_ref[...])


def _matmul_norm_res(a, w, layer, x, g, *, tm):
    t, k = a.shape
    d = w.shape[2]
    return pl.pallas_call(
        _mm_norm_res_kernel,
        grid=(t // tm,),
        in_specs=[
            pl.BlockSpec((tm, k), lambda i: (i, 0)),
            pl.BlockSpec((None, k, d), lambda i: (layer, 0, 0), pipeline_mode=pl.Buffered(1)),
            pl.BlockSpec((tm, d), lambda i: (i, 0)),
            pl.BlockSpec((1, d), lambda i: (0, 0)),
        ],
        out_specs=pl.BlockSpec((tm, d), lambda i: (i, 0)),
        out_shape=jax.ShapeDtypeStruct((t, d), F32),
        compiler_params=_params(("parallel",)),
        name="matmul_norm_res",
    )(a, w, x, g)


PAIRS = GMLP_GROUPS // 2


def _gmlp_out_kernel(*refs, ranges, tm):
    n_parts = len(ranges)
    (u1_ref, v1_ref, mu1_ref, rs1_ref, lng1_ref, lnb1_ref, ws1_ref, bs1_ref,
     u2_ref, v2_ref, mu2_ref, rs2_ref, lng2_ref, lnb2_ref, ws2_ref, bs2_ref,
     wo_ref, g_ref) = refs[:18]
    x_refs = refs[18:18 + n_parts]
    o_ref, a0_scr, a1_scr = refs[18 + n_parts:]
    s = pl.program_id(0)
    q = jnp.maximum(s - 1, 0)
    tile, pair = q // PAIRS, q % PAIRS

    @pl.when(s == 0)
    def _():
        a0_scr[...] = jnp.zeros_like(a0_scr)
        a1_scr[...] = jnp.zeros_like(a1_scr)

    @pl.when(pair == 0)
    def _():
        o_ref[...] = jnp.zeros_like(o_ref)

    def prep(u_ref, v_ref, mu_ref, rs_ref, lng_ref, lnb_ref, ws_ref, bs_ref):
        vn = ((v_ref[0].astype(F32) - mu_ref[...]) * rs_ref[...] * lng_ref[0]
              + lnb_ref[0]).astype(BF16)
        ws = ws_ref[0]
        bs = bs_ref[0]
        parts = []
        for n in range(tm // BLOCK):
            rows = slice(n * BLOCK, (n + 1) * BLOCK)
            mixed = _dot(ws, vn[rows, :]) + bs
            parts.append((u_ref[0, rows, :].astype(F32) * mixed).astype(BF16))
        return jnp.concatenate(parts, axis=0)

    row0 = pl.multiple_of(pair * (2 * GMLP_GDIM), 2 * GMLP_GDIM)
    o_ref[...] += _dot(a0_scr[...], wo_ref[pl.ds(row0, GMLP_GDIM), :])
    a1_scr[...] = prep(u1_ref, v1_ref, mu1_ref, rs1_ref, lng1_ref, lnb1_ref, ws1_ref, bs1_ref)
    o_ref[...] += _dot(a1_scr[...], wo_ref[pl.ds(row0 + GMLP_GDIM, GMLP_GDIM), :])
    a0_scr[...] = prep(u2_ref, v2_ref, mu2_ref, rs2_ref, lng2_ref, lnb2_ref, ws2_ref, bs2_ref)

    def epilogue(x_ref):
        o_ref[...] = x_ref[...] + _rms(o_ref[...], g_ref[...])

    _for_owner(tile, ranges, x_refs, (pair == PAIRS - 1) & (s > 0), epilogue)


def _gmlp_out(z, mu, rstd, ln_g, ln_b, w_s, b_s, w_out, layer, x_parts, g, *, tm):
    t = sum(p.shape[0] for p in x_parts)
    d = x_parts[0].shape[1]
    n_tiles = t // tm
    last = GMLP_GROUPS * n_tiles - 1

    def proj_tile(s):
        return jnp.maximum(s - 1, 0) // PAIRS

    def prep_specs(pair_of):
        def at(f):
            return lambda s: f(*pair_of(s))
        return [
            pl.BlockSpec((1, tm, GMLP_GDIM), at(lambda i, c: (c, i, 0))),
            pl.BlockSpec((1, tm, GMLP_GDIM), at(lambda i, c: (GMLP_GROUPS + c, i, 0))),
            pl.BlockSpec((tm, 1), at(lambda i, c: (i, 0))),
            pl.BlockSpec((tm, 1), at(lambda i, c: (i, 0))),
            pl.BlockSpec((1, 1, GMLP_GDIM), at(lambda i, c: (c, 0, 0))),
            pl.BlockSpec((1, 1, GMLP_GDIM), at(lambda i, c: (c, 0, 0))),
            pl.BlockSpec((1, BLOCK, BLOCK), at(lambda i, c: (c, 0, 0))),
            pl.BlockSpec((1, BLOCK, 1), at(lambda i, c: (c, 0, 0))),
        ]

    def split(gidx):
        gidx = jnp.clip(gidx, 0, last)
        return gidx // GMLP_GROUPS, gidx % GMLP_GROUPS

    x_specs, ranges = _part_specs(x_parts, tm, proj_tile)
    prep_args = (z, z, mu, rstd, ln_g, ln_b, w_s, b_s)
    return pl.pallas_call(
        functools.partial(_gmlp_out_kernel, ranges=ranges, tm=tm),
        grid=(PAIRS * n_tiles + 1,),
        in_specs=prep_specs(lambda s: split(2 * s - 1)) + prep_specs(lambda s: split(2 * s)) + [
            pl.BlockSpec((None, GMLP_HALF, d), lambda s: (layer, 0, 0),
                         pipeline_mode=pl.Buffered(1)),
            pl.BlockSpec((1, d), lambda s: (0, 0)),
        ] + x_specs,
        out_specs=pl.BlockSpec((tm, d), lambda s: (proj_tile(s), 0)),
        out_shape=jax.ShapeDtypeStruct((t, d), F32),
        scratch_shapes=[
            pltpu.VMEM((tm, GMLP_GDIM), BF16),
            pltpu.VMEM((tm, GMLP_GDIM), BF16),
        ],
        compiler_params=_params(("arbitrary",)),
        name="gmlp_out",
    )(*prep_args, *prep_args, w_out, g, *x_parts)


def _win_attn_kernel(sink_ref, q_ref, kp_ref, kc_ref, kn_ref, vp_ref, vc_ref, vn_ref, bias_ref,
                     o_ref, *, r_blocks, n_steps):
    i = pl.program_id(1)
    kall = jnp.concatenate([kp_ref[0], kc_ref[0], kn_ref[0]], axis=0)
    vall = jnp.concatenate([vp_ref[0], vc_ref[0], vn_ref[0]], axis=0)
    col = lax.broadcasted_iota(jnp.int32, (1, 3 * BLOCK), 1)
    sk = jnp.concatenate(
        [jnp.full((BLOCK, 1), sink_ref[h], F32) for h in range(N_HEADS)], axis=0)
    hrows = KV_GROUP * BLOCK
    for r in range(r_blocks):
        rows = slice(r * BLOCK, (r + 1) * BLOCK)
        oob = None
        if r == 0:
            oob = (col < BLOCK) & (i == 0)
        if r == r_blocks - 1:
            hi = (col >= 2 * BLOCK) & (i == n_steps - 1)
            oob = hi if oob is None else (oob | hi)
        parts = []
        for kh in range(N_KV_HEADS):
            kb = kall[r * BLOCK:(r + 3) * BLOCK, kh * HEAD_DIM:(kh + 1) * HEAD_DIM]
            qs = jnp.concatenate(
                [q_ref[0, rows, h * HEAD_DIM:(h + 1) * HEAD_DIM]
                 for h in range(kh * KV_GROUP, (kh + 1) * KV_GROUP)], axis=0)
            parts.append(
                lax.dot_general(qs, kb, (((1,), (1,)), ((), ())), preferred_element_type=F32))
        s = jnp.concatenate(parts, axis=0) + bias_ref[...]
        if oob is not None:
            s = jnp.where(oob, NEG_INF, s)
        m = jnp.maximum(jnp.max(s, axis=-1, keepdims=True), sk)
        p = jnp.exp2(s - m)
        inv = 1.0 / (jnp.sum(p, axis=-1, keepdims=True) + jnp.exp2(sk - m))
        pb = p.astype(BF16)
        for kh in range(N_KV_HEADS):
            vb = vall[r * BLOCK:(r + 3) * BLOCK, kh * HEAD_DIM:(kh + 1) * HEAD_DIM]
            hsl = slice(kh * hrows, (kh + 1) * hrows)
            o = _dot(pb[hsl], vb) * inv[hsl]
            for gq in range(KV_GROUP):
                h = kh * KV_GROUP + gq
                o_ref[0, rows, h * HEAD_DIM:(h + 1) * HEAD_DIM] = (
                    o[gq * BLOCK:(gq + 1) * BLOCK].astype(o_ref.dtype))


def _win_attention(qkv, sink, bias, *, nseq, r_blocks):
    tq = r_blocks * BLOCK
    n_steps = SEQ // tq
    nb = SEQ // BLOCK
    qkv3 = qkv.reshape(nseq, SEQ, qkv.shape[-1])
    kcol = N_HEADS * HEAD_DIM // (N_KV_HEADS * HEAD_DIM)
    vcol = kcol + 1
    kvw = N_KV_HEADS * HEAD_DIM

    def prev_map(c):
        return lambda b, i: (b, jnp.maximum(i * r_blocks - 1, 0), c)

    def cur_map(c):
        return lambda b, i: (b, i, c)

    def next_map(c):
        return lambda b, i: (b, jnp.minimum((i + 1) * r_blocks, nb - 1), c)

    out = pl.pallas_call(
        functools.partial(_win_attn_kernel, r_blocks=r_blocks, n_steps=n_steps),
        grid=(nseq, n_steps),
        in_specs=[
            pl.BlockSpec(memory_space=pltpu.SMEM),
            pl.BlockSpec((1, tq, N_HEADS * HEAD_DIM), lambda b, i: (b, i, 0)),
            pl.BlockSpec((1, BLOCK, kvw), prev_map(kcol)),
            pl.BlockSpec((1, tq, kvw), cur_map(kcol)),
            pl.BlockSpec((1, BLOCK, kvw), next_map(kcol)),
            pl.BlockSpec((1, BLOCK, kvw), prev_map(vcol)),
            pl.BlockSpec((1, tq, kvw), cur_map(vcol)),
            pl.BlockSpec((1, BLOCK, kvw), next_map(vcol)),
            pl.BlockSpec((N_HEADS * BLOCK, 3 * BLOCK), lambda b, i: (0, 0)),
        ],
        out_specs=pl.BlockSpec((1, tq, N_HEADS * HEAD_DIM), lambda b, i: (b, i, 0)),
        out_shape=jax.ShapeDtypeStruct((nseq, SEQ, N_HEADS * HEAD_DIM), BF16),
        compiler_params=_params(("parallel", "parallel")),
        name="win_attention",
    )(sink, qkv3, qkv3, qkv3, qkv3, qkv3, qkv3, qkv3, bias)
    return out.reshape(nseq * SEQ, N_HEADS * HEAD_DIM)


def _t5_bucket(rel):
    half = N_BUCKETS // 2
    exact = half // 2
    n = np.abs(rel)
    large = exact + (np.log(np.maximum(n, 1) / exact) / np.log(MAX_DISTANCE / exact)
                     * (half - exact)).astype(np.int32)
    large = np.minimum(large, half - 1)
    return (np.where(rel > 0, half, 0) + np.where(n < exact, n, large)).astype(np.int32)


def _band_bias(rel_table):
    rel = np.arange(3 * BLOCK)[None, :] - BLOCK - np.arange(BLOCK)[:, None]
    onehot = (_t5_bucket(rel).reshape(1, -1) == np.arange(N_BUCKETS)[:, None]).astype(np.float32)
    bias = jnp.einsum("bh,bn->hn", rel_table.astype(F32) * LOG2E, onehot,
                      precision=lax.Precision.HIGHEST)
    bias = jnp.where((np.abs(rel) <= WINDOW).reshape(1, -1), bias, NEG_INF)
    return bias.reshape(N_HEADS * BLOCK, 3 * BLOCK)


def _kv_kernel(mem_ref, g_ref, w_ref, o_ref):
    h = _rms(mem_ref[0], g_ref[0]).astype(BF16)
    o_ref[0, 0] = _dot(h, w_ref[0]).astype(o_ref.dtype)


def _kv_proj(mem, g, w_kv):
    nseq = mem.shape[0]
    return pl.pallas_call(
        _kv_kernel,
        grid=(DEPTH, nseq),
        in_specs=[
            pl.BlockSpec((1, MEM_LEN, D_MODEL), lambda l, b: (b, 0, 0)),
            pl.BlockSpec((1, 1, D_MODEL), lambda l, b: (l, 0, 0)),
            pl.BlockSpec((1, D_MODEL, 2 * MEM_WIDTH), lambda l, b: (l, 0, 0)),
        ],
        out_specs=pl.BlockSpec((1, 1, MEM_LEN, 2 * MEM_WIDTH), lambda l, b: (l, b, 0, 0)),
        out_shape=jax.ShapeDtypeStruct((DEPTH, nseq, MEM_LEN, 2 * MEM_WIDTH), BF16),
        compiler_params=_params(("parallel", "parallel")),
        name="kv_proj",
    )(mem, g, w_kv)


def _mem_attn_kernel(x_ref, gpre_ref, gpost_ref, wq_ref, kv_ref, wo_ref, o_ref):
    x = x_ref[...]
    h = _rms(x, gpre_ref[...]).astype(BF16)
    q = (_dot(h, wq_ref[...]) * (HEAD_DIM ** -0.5)).astype(BF16)
    outs = []
    for hd in range(MEM_HEADS):
        sl = slice(hd * HEAD_DIM, (hd + 1) * HEAD_DIM)
        k = kv_ref[0, :, sl]
        v = kv_ref[0, :, MEM_WIDTH + hd * HEAD_DIM:MEM_WIDTH + (hd + 1) * HEAD_DIM]
        s = lax.dot_general(q[:, sl], k, (((1,), (1,)), ((), ())), preferred_element_type=F32)
        m = jnp.max(s, axis=-1, keepdims=True)
        p = jnp.exp(s - m)
        denom = jnp.sum(p, axis=-1, keepdims=True)
        outs.append((_dot(p.astype(BF16), v) / denom).astype(BF16))
    y = _dot(jnp.concatenate(outs, axis=-1), wo_ref[...])
    o_ref[...] = x + _rms(y, gpost_ref[...])


def _mem_attention(x, gpre, gpost, w_q, kv, w_o, layer, *, tm):
    t, d = x.shape
    per_seq = SEQ // tm
    return pl.pallas_call(
        _mem_attn_kernel,
        grid=(t // tm,),
        in_specs=[
            pl.BlockSpec((tm, d), lambda i: (i, 0)),
            pl.BlockSpec((1, d), lambda i: (0, 0)),
            pl.BlockSpec((1, d), lambda i: (0, 0)),
            pl.BlockSpec((None, d, MEM_WIDTH), lambda i: (layer, 0, 0)),
            pl.BlockSpec((None, 1, MEM_LEN, 2 * MEM_WIDTH), lambda i: (layer, i // per_seq, 0, 0)),
            pl.BlockSpec((None, MEM_WIDTH, d), lambda i: (layer, 0, 0)),
        ],
        out_specs=pl.BlockSpec((tm, d), lambda i: (i, 0)),
        out_shape=jax.ShapeDtypeStruct((t, d), F32),
        compiler_params=_params(("parallel",)),
        name="mem_attention",
    )(x, gpre, gpost, w_q, kv, w_o)


def _ffn_kernel(x_ref, gpre_ref, gpost_ref, wg_ref, wu_ref, wd_ref, o_ref, h_scr, *, n_chunks):
    j = pl.program_id(1)

    @pl.when(j == 0)
    def _():
        h_scr[...] = _rms(x_ref[...], gpre_ref[...]).astype(BF16)
        o_ref[...] = jnp.zeros_like(o_ref)

    h = h_scr[...]
    gate = _dot(h, wg_ref[...])
    up = _dot(h, wu_ref[...])
    a = (gate * _sigmoid(gate) * up).astype(BF16)
    o_ref[...] += _dot(a, wd_ref[...])

    @pl.when(j == n_chunks - 1)
    def _():
        o_ref[...] = x_ref[...] + _rms(o_ref[...], gpost_ref[...])


def _ffn(x, gpre, gpost, w_gate_up, w_down, layer, *, tm, tn, row0=0, rows=None):
    d = x.shape[1]
    t = x.shape[0] if rows is None else rows
    tile0 = row0 // tm
    n_chunks = D_FF // tn
    x_bytes = tm * d * 4
    fixed = 2 * x_bytes + tm * d * 2 + 2 * 3 * d * tn * 2
    temps = 2 * tm * tn * 4 + tm * tn * 2 + tm * d * 4
    x_mode = None if fixed + 2 * x_bytes + temps <= VMEM_LIMIT else pl.Buffered(1)
    return pl.pallas_call(
        functools.partial(_ffn_kernel, n_chunks=n_chunks),
        grid=(t // tm, n_chunks),
        in_specs=[
            pl.BlockSpec((tm, d), lambda i, j: (i + tile0, 0), pipeline_mode=x_mode),
            pl.BlockSpec((1, d), lambda i, j: (0, 0)),
            pl.BlockSpec((1, d), lambda i, j: (0, 0)),
            pl.BlockSpec((None, d, tn), lambda i, j: (layer, 0, j)),
            pl.BlockSpec((None, d, tn), lambda i, j: (layer, 0, j + n_chunks)),
            pl.BlockSpec((None, tn, d), lambda i, j: (layer, j, 0)),
        ],
        out_specs=pl.BlockSpec((tm, d), lambda i, j: (i, 0)),
        out_shape=jax.ShapeDtypeStruct((t, d), F32),
        scratch_shapes=[pltpu.VMEM((tm, d), BF16)],
        compiler_params=_params(("parallel", "arbitrary")),
        name="ffn",
    )(x, gpre, gpost, w_gate_up, w_gate_up, w_down)


N_CT = D_RNN // LANES
HALO = 2 * SUBLANES


def _gate_windows():
    lo_hi = []
    for j in range(N_CT):
        h0 = (LANES * j) // RNN_BDIM
        h1 = (LANES * j + LANES - 1) // RNN_BDIM
        lo = (RNN_BDIM * h0) // LANES * LANES
        hi = -(-(RNN_BDIM * (h1 + 1)) // LANES) * LANES
        lo_hi.append((lo, hi))
    width = max(hi - lo for lo, hi in lo_hi)
    starts = tuple(min(lo, D_RNN - width) for lo, _ in lo_hi)
    return starts, width


GATE_STARTS, GATE_WIDTH = _gate_windows()


def _gate_weights(w_a, w_x):
    def tile(w, j):
        st = GATE_STARTS[j]
        pieces = []
        for h in range((LANES * j) // RNN_BDIM, (LANES * j + LANES - 1) // RNN_BDIM + 1):
            c0 = max(LANES * j, RNN_BDIM * h) - RNN_BDIM * h
            c1 = min(LANES * (j + 1), RNN_BDIM * (h + 1)) - RNN_BDIM * h
            top = RNN_BDIM * h - st
            pieces.append(jnp.pad(w[h, :, c0:c1], ((top, GATE_WIDTH - RNN_BDIM - top), (0, 0))))
        return jnp.concatenate(pieces, axis=1)

    wa, wx = (w_a * 0.5).astype(BF16), (w_x * 0.5).astype(BF16)
    return jnp.stack([jnp.concatenate([tile(wa, j), tile(wx, j)], axis=1) for j in range(N_CT)])


def _scan_kernel(*refs, ts, n_tiles, nseq, reverse):
    if reverse:
        (xp_ref, xc_ref, xn_ref, cw_ref, cb_ref, wp_ref, ba_ref, bx_ref, lam_ref,
         hsf_ref, gate_ref, o_ref, conv_scr, convb_scr, a_scr, u_scr, h_scr) = refs
    else:
        (xp_ref, xc_ref, xn_ref, cw_ref, cb_ref, wp_ref, ba_ref, bx_ref, lam_ref,
         o_ref, conv_scr, convb_scr, a_scr, u_scr, h_scr) = refs
    i = pl.program_id(0)
    tile = (n_tiles - 1 - i) if reverse else i

    @pl.when(i == 0)
    def _():
        h_scr[...] = jnp.zeros_like(h_scr)
        a_scr[...] = jnp.zeros_like(a_scr)
        u_scr[...] = jnp.zeros_like(u_scr)

    cw = cw_ref[...]
    cb = cb_ref[...]
    shifts = [k - CONV_LEFT for k in range(CONV_WIDTH) if k != CONV_LEFT]
    rr = lax.broadcasted_iota(jnp.int32, (ts, ts + 2 * HALO), 0)
    cc = lax.broadcasted_iota(jnp.int32, (ts, ts + 2 * HALO), 1)
    sel = jnp.concatenate(
        [jnp.where(cc == rr + HALO + s, 1.0, 0.0).astype(BF16) for s in shifts], axis=0)
    for b in range(nseq):
        prev = xp_ref[b]
        nxt = xn_ref[b]
        prev = jnp.where(tile == 0, jnp.zeros_like(prev), prev)
        nxt = jnp.where(tile == n_tiles - 1, jnp.zeros_like(nxt), nxt)
        cur = xc_ref[b]
        shifted = _dot(sel, jnp.concatenate([prev, cur, nxt], axis=0))
        acc = cb + cur.astype(F32) * cw[CONV_LEFT:CONV_LEFT + 1]
        for n, s in enumerate(shifts):
            k = s + CONV_LEFT
            acc = acc + shifted[n * ts:(n + 1) * ts] * cw[k:k + 1]
        conv_scr[b * ts:(b + 1) * ts, :] = acc
        convb_scr[b * ts:(b + 1) * ts, :] = acc.astype(BF16)

    for j in range(N_CT):
        st = GATE_STARTS[j]
        sl = slice(LANES * j, LANES * (j + 1))
        pre = _dot(convb_scr[:, st:st + GATE_WIDTH], wp_ref[j])
        tr = jnp.tanh(pre[:, :LANES] + ba_ref[:, sl])
        ti = jnp.tanh(pre[:, LANES:] + bx_ref[:, sl])
        z = -lam_ref[:, sl]
        softplus = jnp.maximum(z, 0.0) + jnp.log(1.0 + jnp.exp(-jnp.abs(z)))
        half = (-0.5 * LRU_C * LOG2E) * softplus
        a = jnp.exp2(half + half * tr)
        y = 1.0 - a * a
        root = jnp.where(y > 0.0, y * lax.rsqrt(y), 0.0)
        hx = 0.5 * conv_scr[:, sl]
        u = root * (hx + hx * ti)
        for b in range(nseq):
            a_scr[j, pl.ds(b, ts, stride=SUBLANES), :] = a[b * ts:(b + 1) * ts]
            u_scr[j, pl.ds(b, ts, stride=SUBLANES), :] = u[b * ts:(b + 1) * ts]

    def step(k, h):
        t = (ts - 1 - k) if reverse else k
        row = pl.multiple_of(t * SUBLANES, SUBLANES)
        h = a_scr[:, pl.ds(row, SUBLANES), :] * h + u_scr[:, pl.ds(row, SUBLANES), :]
        u_scr[:, pl.ds(row, SUBLANES), :] = h
        return h

    h_scr[...] = lax.fori_loop(0, ts, step, h_scr[...], unroll=8)

    for b in range(nseq):
        for j in range(N_CT):
            sl = slice(LANES * j, LANES * (j + 1))
            y = u_scr[j, pl.ds(b, ts, stride=SUBLANES), :]
            if reverse:
                y = gate_ref[b, :, sl].astype(F32) * (y + hsf_ref[b, :, sl].astype(F32))
            o_ref[b, :, sl] = y.astype(o_ref.dtype)


def _rglru_scan(y_in, conv_w, conv_b, wp, b_a, b_x, lam, hs_fwd, *, ts, nseq, reverse):
    n_tiles = SEQ // ts
    per_tile = ts // HALO

    def tmap(i):
        return (n_tiles - 1 - i) if reverse else i

    in_specs = [
        pl.BlockSpec((nseq, HALO, D_RNN), lambda i: (0, jnp.maximum(tmap(i) * per_tile - 1, 0), 1)),
        pl.BlockSpec((nseq, ts, D_RNN), lambda i: (0, tmap(i), 1)),
        pl.BlockSpec((nseq, HALO, D_RNN),
                     lambda i: (0, jnp.minimum((tmap(i) + 1) * per_tile, SEQ // HALO - 1), 1)),
        pl.BlockSpec((CONV_WIDTH, D_RNN), lambda i: (0, 0)),
        pl.BlockSpec((1, D_RNN), lambda i: (0, 0)),
        pl.BlockSpec((N_CT, GATE_WIDTH, 2 * LANES), lambda i: (0, 0, 0)),
        pl.BlockSpec((1, D_RNN), lambda i: (0, 0)),
        pl.BlockSpec((1, D_RNN), lambda i: (0, 0)),
        pl.BlockSpec((1, D_RNN), lambda i: (0, 0)),
    ]
    args = [y_in, y_in, y_in, conv_w, conv_b, wp, b_a, b_x, lam]
    if reverse:
        in_specs += [
            pl.BlockSpec((nseq, ts, D_RNN), lambda i: (0, tmap(i), 0)),
            pl.BlockSpec((nseq, ts, D_RNN), lambda i: (0, tmap(i), 0)),
        ]
        args += [hs_fwd, y_in]
    return pl.pallas_call(
        functools.partial(_scan_kernel, ts=ts, n_tiles=n_tiles, nseq=nseq, reverse=reverse),
        grid=(n_tiles,),
        in_specs=in_specs,
        out_specs=pl.BlockSpec((nseq, ts, D_RNN), lambda i: (0, tmap(i), 0)),
        out_shape=jax.ShapeDtypeStruct((nseq, SEQ, D_RNN), BF16),
        scratch_shapes=[
            pltpu.VMEM((nseq * ts, D_RNN), F32),
            pltpu.VMEM((nseq * ts, D_RNN), BF16),
            pltpu.VMEM((N_CT, ts * SUBLANES, LANES), F32),
            pltpu.VMEM((N_CT, ts * SUBLANES, LANES), F32),
            pltpu.VMEM((N_CT, SUBLANES, LANES), F32),
        ],
        compiler_params=_params(("arbitrary",)),
        name="rglru_scan_bwd" if reverse else "rglru_scan_fwd",
    )(*args)


def kernel(x_prompt, x_sample, mem_prompt, mem_sample, norms, rel_bias, a_w_in, a_ln_g, a_ln_b, a_w_s, a_b_s, a_w_out, b_w_qkv, b_sink, b_w_o, c_w_in, c_conv_w, c_conv_b, c_w_a, c_b_a, c_w_x, c_b_x, c_lam, c_w_out, m_w_q, m_w_kv, m_w_o, f_w_gate_up, f_w_down):
    n_prompt = x_prompt.shape[0]
    nseq = n_prompt + x_sample.shape[0]
    assert nseq <= SUBLANES
    t = nseq * SEQ
    t_prompt = n_prompt * SEQ
    x = None
    x_parts = [x_prompt.reshape(t_prompt, D_MODEL), x_sample.reshape(t - t_prompt, D_MODEL)]
    mem = jnp.concatenate([mem_prompt, mem_sample], axis=0)

    def gain(i, k):
        return norms[i, k].reshape(1, D_MODEL)

    a_w_in_b, a_w_out_b = a_w_in.astype(BF16), a_w_out.astype(BF16)
    q_cols = N_HEADS * HEAD_DIM
    b_w_qkv_b = jnp.concatenate(
        [b_w_qkv[:, :, :q_cols] * (HEAD_DIM ** -0.5 * LOG2E), b_w_qkv[:, :, q_cols:]],
        axis=-1).astype(BF16)
    b_w_o_b = b_w_o.astype(BF16)
    c_w_in_b, c_w_out_b = c_w_in.astype(BF16), c_w_out.astype(BF16)
    m_w_q_b, m_w_o_b = m_w_q.astype(BF16), m_w_o.astype(BF16)
    f_w_gate_up_b, f_w_down_b = f_w_gate_up.astype(BF16), f_w_down.astype(BF16)

    kv_all = _kv_proj(mem, norms[:, 4].reshape(DEPTH, 1, D_MODEL), m_w_kv.astype(BF16))

    for i in range(DEPTH):
        kind, j = i % N_MIXERS, i // N_MIXERS
        if kind == 0:
            parts = x_parts if x is None else [x]
            z, mu, rstd = _gmlp_in(parts, gain(i, 0), a_w_in_b, j, tm=1024, tn=2 * GMLP_GDIM)
            x = _gmlp_out(z, mu, rstd, a_ln_g[j].reshape(GMLP_GROUPS, 1, GMLP_GDIM),
                          a_ln_b[j].reshape(GMLP_GROUPS, 1, GMLP_GDIM),
                          a_w_s[j].astype(BF16), a_b_s[j].reshape(GMLP_GROUPS, BLOCK, 1),
                          a_w_out_b, j, parts, gain(i, 1), tm=512)
        elif kind == 1:
            qkv = _norm_matmul(x, gain(i, 0), b_w_qkv_b, j, tm=1024, tn=1024)
            o = _win_attention(qkv, b_sink[j] * LOG2E, _band_bias(rel_bias), nseq=nseq, r_blocks=4)
            x = _matmul_norm_res(o, b_w_o_b, j, x, gain(i, 1), tm=512)
        else:
            y_in = _norm_matmul(x, gain(i, 0), c_w_in_b, j, tm=1024, tn=D_RNN // 2, n_gelu=2)
            y_in = y_in.reshape(nseq, SEQ, 2 * D_RNN)
            scan_args = (c_conv_w[j], c_conv_b[j].reshape(1, D_RNN))
            hs = None
            for d in range(2):
                hs = _rglru_scan(y_in, *scan_args, _gate_weights(c_w_a[j, d], c_w_x[j, d]),
                                 (c_b_a[j, d] * 0.5).reshape(1, D_RNN),
                                 (c_b_x[j, d] * 0.5).reshape(1, D_RNN),
                                 c_lam[j, d].reshape(1, D_RNN), hs, ts=64, nseq=nseq,
                                 reverse=(d == 1))
            x = _matmul_norm_res(hs.reshape(t, D_RNN), c_w_out_b, j, x, gain(i, 1), tm=512)
        x = _mem_attention(x, gain(i, 2), gain(i, 3), m_w_q_b, kv_all, m_w_o_b, i, tm=512)
        ffn = functools.partial(_ffn, x, gain(i, 5), gain(i, 6), f_w_gate_up_b, f_w_down_b, i,
                                tn=512)
        if i < DEPTH - 1:
            x = ffn(tm=768)

    y_prompt = ffn(tm=1024, row0=0, rows=t_prompt).reshape(n_prompt, SEQ, D_MODEL)
    y_sample = ffn(tm=1024, row0=t_prompt, rows=t - t_prompt).reshape(
        nseq - n_prompt, SEQ, D_MODEL)
    return (y_prompt, y_sample)
```

```python
import functools

import numpy as np
import jax
import jax.numpy as jnp
from jax import lax
from jax.experimental import pallas as pl
from jax.experimental.pallas import tpu as pltpu

D_MODEL = 2048
SEQ = 4096
DEPTH = 4
N_MIXERS = 3
BLOCK = 128
HEAD_DIM = 128
EPS = 1e-6
NEG_INF = -1e30
GMLP_HALF = 2 * D_MODEL
GMLP_GROUPS = 8
GMLP_GDIM = GMLP_HALF // GMLP_GROUPS
N_HEADS = D_MODEL // HEAD_DIM
N_KV_HEADS = 4
KV_GROUP = N_HEADS // N_KV_HEADS
WINDOW = 128
N_BUCKETS = 32
MAX_DISTANCE = 128
D_RNN = 2816
RNN_BLOCKS = 16
RNN_BDIM = D_RNN // RNN_BLOCKS
CONV_WIDTH = 4
CONV_LEFT = 2
LRU_C = 8.0
MEM_LEN = 256
MEM_HEADS = 4
MEM_WIDTH = MEM_HEADS * HEAD_DIM
D_FF = 5632

LANES = 128
SUBLANES = 8
VMEM_LIMIT = 56 * 1024 * 1024

F32 = jnp.float32
BF16 = jnp.bfloat16
LOG2E = float(np.log2(np.e))


def _params(sem):
    return pltpu.CompilerParams(dimension_semantics=sem, vmem_limit_bytes=VMEM_LIMIT)


def _rms(x, g):
    ms = jnp.mean(x * x, axis=-1, keepdims=True)
    return x * lax.rsqrt(ms + EPS) * g


def _gelu(x):
    k1 = float(-2.0 * np.sqrt(2.0 / np.pi) * LOG2E)
    return x / (1.0 + jnp.exp2(x * (k1 + (k1 * 0.044715) * (x * x))))


def _sigmoid(x):
    return 1.0 / (1.0 + jnp.exp(-x))


def _dot(a, b):
    return jnp.dot(a, b, preferred_element_type=F32)


def _norm_mm_kernel(x_ref, g_ref, w_ref, o_ref, h_scr, *, n_gelu):
    j = pl.program_id(1)

    @pl.when(j == 0)
    def _():
        h_scr[...] = _rms(x_ref[...], g_ref[...]).astype(BF16)

    def emit(act):
        z = _dot(h_scr[...], w_ref[...])
        o_ref[...] = (_gelu(z) if act else z).astype(o_ref.dtype)

    if n_gelu == 0:
        emit(False)
    else:
        pl.when(j < n_gelu)(lambda: emit(True))
        pl.when(j >= n_gelu)(lambda: emit(False))


def _norm_matmul(x, g, w, layer, *, tm, tn, n_gelu=0):
    t, d = x.shape
    n = w.shape[2]
    return pl.pallas_call(
        functools.partial(_norm_mm_kernel, n_gelu=n_gelu),
        grid=(t // tm, n // tn),
        in_specs=[
            pl.BlockSpec((tm, d), lambda i, j: (i, 0)),
            pl.BlockSpec((1, d), lambda i, j: (0, 0)),
            pl.BlockSpec((None, d, tn), lambda i, j: (layer, 0, j)),
        ],
        out_specs=pl.BlockSpec((tm, tn), lambda i, j: (i, j)),
        out_shape=jax.ShapeDtypeStruct((t, n), BF16),
        scratch_shapes=[pltpu.VMEM((tm, d), BF16)],
        compiler_params=_params(("parallel", "arbitrary")),
        name="norm_matmul",
    )(x, g, w)


def _part_specs(parts, tm, tile_of):
    specs, ranges, lo = [], [], 0
    for p in parts:
        n = p.shape[0] // tm
        specs.append(pl.BlockSpec(
            (tm, p.shape[1]),
            lambda *idx, lo=lo, n=n: (jnp.clip(tile_of(*idx) - lo, 0, n - 1), 0)))
        ranges.append((lo, lo + n))
        lo += n
    return specs, tuple(ranges)


def _for_owner(i, ranges, refs, cond, fn):
    for (lo, hi), ref in zip(ranges, refs):
        pl.when(cond & (i >= lo) & (i < hi))(functools.partial(fn, ref))


MXU_COLS = 256


def _gmlp_in_kernel(*refs, ranges, n_col, tn):
    n_parts = len(ranges)
    x_refs = refs[:n_parts]
    g_ref, w_ref, z_ref, mu_ref, rs_ref, h_scr, s1_scr, s2_scr = refs[n_parts:]
    i, j = pl.program_id(0), pl.program_id(1)
    tm = h_scr.shape[0]

    def prologue(x_ref):
        h_scr[...] = _rms(x_ref[...], g_ref[...]).astype(BF16)
        s1_scr[...] = jnp.zeros_like(s1_scr)
        s2_scr[...] = jnp.zeros_like(s2_scr)

    _for_owner(i, ranges, x_refs, j == 0, prologue)

    def step(is_v):
        p1 = jnp.zeros((tm, LANES), F32)
        p2 = jnp.zeros((tm, LANES), F32)
        per_tile = GMLP_GDIM // MXU_COLS
        for c in range(tn // MXU_COLS):
            cols = slice(c * MXU_COLS, (c + 1) * MXU_COLS)
            zc = _gelu(_dot(h_scr[...], w_ref[:, cols]))
            lo = (c % per_tile) * MXU_COLS
            z_ref[c // per_tile, :, lo:lo + MXU_COLS] = zc.astype(z_ref.dtype)
            if is_v:
                for k in range(MXU_COLS // LANES):
                    zk = zc[:, k * LANES:(k + 1) * LANES]
                    p1 = p1 + zk
                    p2 = p2 + zk * zk
        if is_v:
            s1_scr[...] += p1
            s2_scr[...] += p2

    pl.when(j < n_col // 2)(functools.partial(step, False))
    pl.when(j >= n_col // 2)(functools.partial(step, True))

    @pl.when(j == n_col - 1)
    def _():
        mu = jnp.sum(s1_scr[...], axis=-1, keepdims=True) * (1.0 / GMLP_HALF)
        msq = jnp.sum(s2_scr[...], axis=-1, keepdims=True) * (1.0 / GMLP_HALF)
        mu_ref[...] = mu
        rs_ref[...] = lax.rsqrt(jnp.maximum(msq - mu * mu, 0.0) + EPS)


def _gmlp_in(x_parts, g, w, layer, *, tm, tn):
    t = sum(p.shape[0] for p in x_parts)
    d = x_parts[0].shape[1]
    n_col = 2 * GMLP_HALF // tn
    sub = tn // GMLP_GDIM
    x_specs, ranges = _part_specs(x_parts, tm, lambda i, j: i)
    return pl.pallas_call(
        functools.partial(_gmlp_in_kernel, ranges=ranges, n_col=n_col, tn=tn),
        grid=(t // tm, n_col),
        in_specs=x_specs + [
            pl.BlockSpec((1, d), lambda i, j: (0, 0)),
            pl.BlockSpec((None, d, tn), lambda i, j: (layer, 0, j)),
        ],
        out_specs=[
            pl.BlockSpec((sub, tm, GMLP_GDIM), lambda i, j: (j, i, 0)),
            pl.BlockSpec((tm, 1), lambda i, j: (i, 0)),
            pl.BlockSpec((tm, 1), lambda i, j: (i, 0)),
        ],
        out_shape=[
            jax.ShapeDtypeStruct((2 * GMLP_GROUPS, t, GMLP_GDIM), BF16),
            jax.ShapeDtypeStruct((t, 1), F32),
            jax.ShapeDtypeStruct((t, 1), F32),
        ],
        scratch_shapes=[
            pltpu.VMEM((tm, d), BF16),
            pltpu.VMEM((tm, LANES), F32),
            pltpu.VMEM((tm, LANES), F32),
        ],
        compiler_params=_params(("parallel", "arbitrary")),
        name="gmlp_in",
    )(*x_parts, g, w)


def _mm_norm_res_kernel(a_ref, w_ref, x_ref, g_ref, o_ref):
    y = _dot(a_ref[...], w_ref[...])
    o_ref[...] = x_ref[...] + _rms(y, g_ref[...])


def _matmul_norm_res(a, w, layer, x, g, *, tm):
    t, k = a.shape
    d = w.shape[2]
    return pl.pallas_call(
        _mm_norm_res_kernel,
        grid=(t // tm,),
        in_specs=[
            pl.BlockSpec((tm, k), lambda i: (i, 0)),
            pl.BlockSpec((None, k, d), lambda i: (layer, 0, 0), pipeline_mode=pl.Buffered(1)),
            pl.BlockSpec((tm, d), lambda i: (i, 0)),
            pl.BlockSpec((1, d), lambda i: (0, 0)),
        ],
        out_specs=pl.BlockSpec((tm, d), lambda i: (i, 0)),
        out_shape=jax.ShapeDtypeStruct((t, d), F32),
        compiler_params=_params(("parallel",)),
        name="matmul_norm_res",
    )(a, w, x, g)


PAIRS = GMLP_GROUPS // 2


def _gmlp_out_kernel(*refs, ranges, tm):
    n_parts = len(ranges)
    (u1_ref, v1_ref, mu1_ref, rs1_ref, lng1_ref, lnb1_ref, ws1_ref, bs1_ref,
     u2_ref, v2_ref, mu2_ref, rs2_ref, lng2_ref, lnb2_ref, ws2_ref, bs2_ref,
     wo_ref, g_ref) = refs[:18]
    x_refs = refs[18:18 + n_parts]
    o_ref, a0_scr, a1_scr = refs[18 + n_parts:]
    s = pl.program_id(0)
    q = jnp.maximum(s - 1, 0)
    tile, pair = q // PAIRS, q % PAIRS

    @pl.when(s == 0)
    def _():
        a0_scr[...] = jnp.zeros_like(a0_scr)
        a1_scr[...] = jnp.zeros_like(a1_scr)

    @pl.when(pair == 0)
    def _():
        o_ref[...] = jnp.zeros_like(o_ref)

    def prep(u_ref, v_ref, mu_ref, rs_ref, lng_ref, lnb_ref, ws_ref, bs_ref):
        vn = ((v_ref[0].astype(F32) - mu_ref[...]) * rs_ref[...] * lng_ref[0]
              + lnb_ref[0]).astype(BF16)
        ws = ws_ref[0]
        bs = bs_ref[0]
        parts = []
        for n in range(tm // BLOCK):
            rows = slice(n * BLOCK, (n + 1) * BLOCK)
            mixed = _dot(ws, vn[rows, :]) + bs
            parts.append((u_ref[0, rows, :].astype(F32) * mixed).astype(BF16))
        return jnp.concatenate(parts, axis=0)

    row0 = pl.multiple_of(pair * (2 * GMLP_GDIM), 2 * GMLP_GDIM)
    o_ref[...] += _dot(a0_scr[...], wo_ref[pl.ds(row0, GMLP_GDIM), :])
    a1_scr[...] = prep(u1_ref, v1_ref, mu1_ref, rs1_ref, lng1_ref, lnb1_ref, ws1_ref, bs1_ref)
    o_ref[...] += _dot(a1_scr[...], wo_ref[pl.ds(row0 + GMLP_GDIM, GMLP_GDIM), :])
    a0_scr[...] = prep(u2_ref, v2_ref, mu2_ref, rs2_ref, lng2_ref, lnb2_ref, ws2_ref, bs2_ref)

    def epilogue(x_ref):
        o_ref[...] = x_ref[...] + _rms(o_ref[...], g_ref[...])

    _for_owner(tile, ranges, x_refs, (pair == PAIRS - 1) & (s > 0), epilogue)


def _gmlp_out(z, mu, rstd, ln_g, ln_b, w_s, b_s, w_out, layer, x_parts, g, *, tm):
    t = sum(p.shape[0] for p in x_parts)
    d = x_parts[0].shape[1]
    n_tiles = t // tm
    last = GMLP_GROUPS * n_tiles - 1

    def proj_tile(s):
        return jnp.maximum(s - 1, 0) // PAIRS

    def prep_specs(pair_of):
        def at(f):
            return lambda s: f(*pair_of(s))
        return [
            pl.BlockSpec((1, tm, GMLP_GDIM), at(lambda i, c: (c, i, 0))),
            pl.BlockSpec((1, tm, GMLP_GDIM), at(lambda i, c: (GMLP_GROUPS + c, i, 0))),
            pl.BlockSpec((tm, 1), at(lambda i, c: (i, 0))),
            pl.BlockSpec((tm, 1), at(lambda i, c: (i, 0))),
            pl.BlockSpec((1, 1, GMLP_GDIM), at(lambda i, c: (c, 0, 0))),
            pl.BlockSpec((1, 1, GMLP_GDIM), at(lambda i, c: (c, 0, 0))),
            pl.BlockSpec((1, BLOCK, BLOCK), at(lambda i, c: (c, 0, 0))),
            pl.BlockSpec((1, BLOCK, 1), at(lambda i, c: (c, 0, 0))),
        ]

    def split(gidx):
        gidx = jnp.clip(gidx, 0, last)
        return gidx // GMLP_GROUPS, gidx % GMLP_GROUPS

    x_specs, ranges = _part_specs(x_parts, tm, proj_tile)
    prep_args = (z, z, mu, rstd, ln_g, ln_b, w_s, b_s)
    return pl.pallas_call(
        functools.partial(_gmlp_out_kernel, ranges=ranges, tm=tm),
        grid=(PAIRS * n_tiles + 1,),
        in_specs=prep_specs(lambda s: split(2 * s - 1)) + prep_specs(lambda s: split(2 * s)) + [
            pl.BlockSpec((None, GMLP_HALF, d), lambda s: (layer, 0, 0),
                         pipeline_mode=pl.Buffered(1)),
            pl.BlockSpec((1, d), lambda s: (0, 0)),
        ] + x_specs,
        out_specs=pl.BlockSpec((tm, d), lambda s: (proj_tile(s), 0)),
        out_shape=jax.ShapeDtypeStruct((t, d), F32),
        scratch_shapes=[
            pltpu.VMEM((tm, GMLP_GDIM), BF16),
            pltpu.VMEM((tm, GMLP_GDIM), BF16),
        ],
        compiler_params=_params(("arbitrary",)),
        name="gmlp_out",
    )(*prep_args, *prep_args, w_out, g, *x_parts)


def _win_attn_kernel(sink_ref, q_ref, kp_ref, kc_ref, kn_ref, vp_ref, vc_ref, vn_ref, bias_ref,
                     o_ref, *, r_blocks, n_steps):
    i = pl.program_id(1)
    kall = jnp.concatenate([kp_ref[0], kc_ref[0], kn_ref[0]], axis=0)
    vall = jnp.concatenate([vp_ref[0], vc_ref[0], vn_ref[0]], axis=0)
    col = lax.broadcasted_iota(jnp.int32, (1, 3 * BLOCK), 1)
    sk = jnp.concatenate(
        [jnp.full((BLOCK, 1), sink_ref[h], F32) for h in range(N_HEADS)], axis=0)
    hrows = KV_GROUP * BLOCK
    for r in range(r_blocks):
        rows = slice(r * BLOCK, (r + 1) * BLOCK)
        oob = None
        if r == 0:
            oob = (col < BLOCK) & (i == 0)
        if r == r_blocks - 1:
            hi = (col >= 2 * BLOCK) & (i == n_steps - 1)
            oob = hi if oob is None else (oob | hi)
        parts = []
        for kh in range(N_KV_HEADS):
            kb = kall[r * BLOCK:(r + 3) * BLOCK, kh * HEAD_DIM:(kh + 1) * HEAD_DIM]
            qs = jnp.concatenate(
                [q_ref[0, rows, h * HEAD_DIM:(h + 1) * HEAD_DIM]
                 for h in range(kh * KV_GROUP, (kh + 1) * KV_GROUP)], axis=0)
            parts.append(
                lax.dot_general(qs, kb, (((1,), (1,)), ((), ())), preferred_element_type=F32))
        s = jnp.concatenate(parts, axis=0) + bias_ref[...]
        if oob is not None:
            s = jnp.where(oob, NEG_INF, s)
        m = jnp.maximum(jnp.max(s, axis=-1, keepdims=True), sk)
        p = jnp.exp2(s - m)
        inv = 1.0 / (jnp.sum(p, axis=-1, keepdims=True) + jnp.exp2(sk - m))
        pb = p.astype(BF16)
        for kh in range(N_KV_HEADS):
            vb = vall[r * BLOCK:(r + 3) * BLOCK, kh * HEAD_DIM:(kh + 1) * HEAD_DIM]
            hsl = slice(kh * hrows, (kh + 1) * hrows)
            o = _dot(pb[hsl], vb) * inv[hsl]
            for gq in range(KV_GROUP):
                h = kh * KV_GROUP + gq
                o_ref[0, rows, h * HEAD_DIM:(h + 1) * HEAD_DIM] = (
                    o[gq * BLOCK:(gq + 1) * BLOCK].astype(o_ref.dtype))


def _win_attention(qkv, sink, bias, *, nseq, r_blocks):
    tq = r_blocks * BLOCK
    n_steps = SEQ // tq
    nb = SEQ // BLOCK
    qkv3 = qkv.reshape(nseq, SEQ, qkv.shape[-1])
    kcol = N_HEADS * HEAD_DIM // (N_KV_HEADS * HEAD_DIM)
    vcol = kcol + 1
    kvw = N_KV_HEADS * HEAD_DIM

    def prev_map(c):
        return lambda b, i: (b, jnp.maximum(i * r_blocks - 1, 0), c)

    def cur_map(c):
        return lambda b, i: (b, i, c)

    def next_map(c):
        return lambda b, i: (b, jnp.minimum((i + 1) * r_blocks, nb - 1), c)

    out = pl.pallas_call(
        functools.partial(_win_attn_kernel, r_blocks=r_blocks, n_steps=n_steps),
        grid=(nseq, n_steps),
        in_specs=[
            pl.BlockSpec(memory_space=pltpu.SMEM),
            pl.BlockSpec((1, tq, N_HEADS * HEAD_DIM), lambda b, i: (b, i, 0)),
            pl.BlockSpec((1, BLOCK, kvw), prev_map(kcol)),
            pl.BlockSpec((1, tq, kvw), cur_map(kcol)),
            pl.BlockSpec((1, BLOCK, kvw), next_map(kcol)),
            pl.BlockSpec((1, BLOCK, kvw), prev_map(vcol)),
            pl.BlockSpec((1, tq, kvw), cur_map(vcol)),
            pl.BlockSpec((1, BLOCK, kvw), next_map(vcol)),
            pl.BlockSpec((N_HEADS * BLOCK, 3 * BLOCK), lambda b, i: (0, 0)),
        ],
        out_specs=pl.BlockSpec((1, tq, N_HEADS * HEAD_DIM), lambda b, i: (b, i, 0)),
        out_shape=jax.ShapeDtypeStruct((nseq, SEQ, N_HEADS * HEAD_DIM), BF16),
        compiler_params=_params(("parallel", "parallel")),
        name="win_attention",
    )(sink, qkv3, qkv3, qkv3, qkv3, qkv3, qkv3, qkv3, bias)
    return out.reshape(nseq * SEQ, N_HEADS * HEAD_DIM)


def _t5_bucket(rel):
    half = N_BUCKETS // 2
    exact = half // 2
    n = np.abs(rel)
    large = exact + (np.log(np.maximum(n, 1) / exact) / np.log(MAX_DISTANCE / exact)
                     * (half - exact)).astype(np.int32)
    large = np.minimum(large, half - 1)
    return (np.where(rel > 0, half, 0) + np.where(n < exact, n, large)).astype(np.int32)


def _band_bias(rel_table):
    rel = np.arange(3 * BLOCK)[None, :] - BLOCK - np.arange(BLOCK)[:, None]
    onehot = (_t5_bucket(rel).reshape(1, -1) == np.arange(N_BUCKETS)[:, None]).astype(np.float32)
    bias = jnp.einsum("bh,bn->hn", rel_table.astype(F32) * LOG2E, onehot,
                      precision=lax.Precision.HIGHEST)
    bias = jnp.where((np.abs(rel) <= WINDOW).reshape(1, -1), bias, NEG_INF)
    return bias.reshape(N_HEADS * BLOCK, 3 * BLOCK)


def _kv_kernel(mem_ref, g_ref, w_ref, o_ref):
    h = _rms(mem_ref[0], g_ref[0]).astype(BF16)
    o_ref[0, 0] = _dot(h, w_ref[0]).astype(o_ref.dtype)


def _kv_proj(mem, g, w_kv):
    nseq = mem.shape[0]
    return pl.pallas_call(
        _kv_kernel,
        grid=(DEPTH, nseq),
        in_specs=[
            pl.BlockSpec((1, MEM_LEN, D_MODEL), lambda l, b: (b, 0, 0)),
            pl.BlockSpec((1, 1, D_MODEL), lambda l, b: (l, 0, 0)),
            pl.BlockSpec((1, D_MODEL, 2 * MEM_WIDTH), lambda l, b: (l, 0, 0)),
        ],
        out_specs=pl.BlockSpec((1, 1, MEM_LEN, 2 * MEM_WIDTH), lambda l, b: (l, b, 0, 0)),
        out_shape=jax.ShapeDtypeStruct((DEPTH, nseq, MEM_LEN, 2 * MEM_WIDTH), BF16),
        compiler_params=_params(("parallel", "parallel")),
        name="kv_proj",
    )(mem, g, w_kv)


def _mem_attn_kernel(x_ref, gpre_ref, gpost_ref, wq_ref, kv_ref, wo_ref, o_ref):
    x = x_ref[...]
    h = _rms(x, gpre_ref[...]).astype(BF16)
    q = (_dot(h, wq_ref[...]) * (HEAD_DIM ** -0.5)).astype(BF16)
    outs = []
    for hd in range(MEM_HEADS):
        sl = slice(hd * HEAD_DIM, (hd + 1) * HEAD_DIM)
        k = kv_ref[0, :, sl]
        v = kv_ref[0, :, MEM_WIDTH + hd * HEAD_DIM:MEM_WIDTH + (hd + 1) * HEAD_DIM]
        s = lax.dot_general(q[:, sl], k, (((1,), (1,)), ((), ())), preferred_element_type=F32)
        m = jnp.max(s, axis=-1, keepdims=True)
        p = jnp.exp(s - m)
        denom = jnp.sum(p, axis=-1, keepdims=True)
        outs.append((_dot(p.astype(BF16), v) / denom).astype(BF16))
    y = _dot(jnp.concatenate(outs, axis=-1), wo_ref[...])
    o_ref[...] = x + _rms(y, gpost_ref[...])


def _mem_attention(x, gpre, gpost, w_q, kv, w_o, layer, *, tm):
    t, d = x.shape
    per_seq = SEQ // tm
    return pl.pallas_call(
        _mem_attn_kernel,
        grid=(t // tm,),
        in_specs=[
            pl.BlockSpec((tm, d), lambda i: (i, 0)),
            pl.BlockSpec((1, d), lambda i: (0, 0)),
            pl.BlockSpec((1, d), lambda i: (0, 0)),
            pl.BlockSpec((None, d, MEM_WIDTH), lambda i: (layer, 0, 0)),
            pl.BlockSpec((None, 1, MEM_LEN, 2 * MEM_WIDTH), lambda i: (layer, i // per_seq, 0, 0)),
            pl.BlockSpec((None, MEM_WIDTH, d), lambda i: (layer, 0, 0)),
        ],
        out_specs=pl.BlockSpec((tm, d), lambda i: (i, 0)),
        out_shape=jax.ShapeDtypeStruct((t, d), F32),
        compiler_params=_params(("parallel",)),
        name="mem_attention",
    )(x, gpre, gpost, w_q, kv, w_o)


def _ffn_kernel(x_ref, gpre_ref, gpost_ref, wg_ref, wu_ref, wd_ref, o_ref, h_scr, *, n_chunks):
    j = pl.program_id(1)

    @pl.when(j == 0)
    def _():
        h_scr[...] = _rms(x_ref[...], gpre_ref[...]).astype(BF16)
        o_ref[...] = jnp.zeros_like(o_ref)

    h = h_scr[...]
    gate = _dot(h, wg_ref[...])
    up = _dot(h, wu_ref[...])
    a = (gate * _sigmoid(gate) * up).astype(BF16)
    o_ref[...] += _dot(a, wd_ref[...])

    @pl.when(j == n_chunks - 1)
    def _():
        o_ref[...] = x_ref[...] + _rms(o_ref[...], gpost_ref[...])


def _ffn(x, gpre, gpost, w_gate_up, w_down, layer, *, tm, tn, row0=0, rows=None):
    d = x.shape[1]
    t = x.shape[0] if rows is None else rows
    tile0 = row0 // tm
    n_chunks = D_FF // tn
    x_bytes = tm * d * 4
    fixed = 2 * x_bytes + tm * d * 2 + 2 * 3 * d * tn * 2
    temps = 2 * tm * tn * 4 + tm * tn * 2 + tm * d * 4
    x_mode = None if fixed + 2 * x_bytes + temps <= VMEM_LIMIT else pl.Buffered(1)
    return pl.pallas_call(
        functools.partial(_ffn_kernel, n_chunks=n_chunks),
        grid=(t // tm, n_chunks),
        in_specs=[
            pl.BlockSpec((tm, d), lambda i, j: (i + tile0, 0), pipeline_mode=x_mode),
            pl.BlockSpec((1, d), lambda i, j: (0, 0)),
            pl.BlockSpec((1, d), lambda i, j: (0, 0)),
            pl.BlockSpec((None, d, tn), lambda i, j: (layer, 0, j)),
            pl.BlockSpec((None, d, tn), lambda i, j: (layer, 0, j + n_chunks)),
            pl.BlockSpec((None, tn, d), lambda i, j: (layer, j, 0)),
        ],
        out_specs=pl.BlockSpec((tm, d), lambda i, j: (i, 0)),
        out_shape=jax.ShapeDtypeStruct((t, d), F32),
        scratch_shapes=[pltpu.VMEM((tm, d), BF16)],
        compiler_params=_params(("parallel", "arbitrary")),
        name="ffn",
    )(x, gpre, gpost, w_gate_up, w_gate_up, w_down)


N_CT = D_RNN // LANES
HALO = 2 * SUBLANES


def _gate_windows():
    lo_hi = []
    for j in range(N_CT):
        h0 = (LANES * j) // RNN_BDIM
        h1 = (LANES * j + LANES - 1) // RNN_BDIM
        lo = (RNN_BDIM * h0) // LANES * LANES
        hi = -(-(RNN_BDIM * (h1 + 1)) // LANES) * LANES
        lo_hi.append((lo, hi))
    width = max(hi - lo for lo, hi in lo_hi)
    starts = tuple(min(lo, D_RNN - width) for lo, _ in lo_hi)
    return starts, width


GATE_STARTS, GATE_WIDTH = _gate_windows()


def _gate_weights(w_a, w_x):
    def tile(w, j):
        st = GATE_STARTS[j]
        pieces = []
        for h in range((LANES * j) // RNN_BDIM, (LANES * j + LANES - 1) // RNN_BDIM + 1):
            c0 = max(LANES * j, RNN_BDIM * h) - RNN_BDIM * h
            c1 = min(LANES * (j + 1), RNN_BDIM * (h + 1)) - RNN_BDIM * h
            top = RNN_BDIM * h - st
            pieces.append(jnp.pad(w[h, :, c0:c1], ((top, GATE_WIDTH - RNN_BDIM - top), (0, 0))))
        return jnp.concatenate(pieces, axis=1)

    wa, wx = (w_a * 0.5).astype(BF16), (w_x * 0.5).astype(BF16)
    return jnp.stack([jnp.concatenate([tile(wa, j), tile(wx, j)], axis=1) for j in range(N_CT)])


def _scan_kernel(*refs, ts, n_tiles, nseq, reverse):
    if reverse:
        (xp_ref, xc_ref, xn_ref, cw_ref, cb_ref, wp_ref, ba_ref, bx_ref, lam_ref,
         hsf_ref, gate_ref, o_ref, conv_scr, convb_scr, a_scr, u_scr, h_scr) = refs
    else:
        (xp_ref, xc_ref, xn_ref, cw_ref, cb_ref, wp_ref, ba_ref, bx_ref, lam_ref,
         o_ref, conv_scr, convb_scr, a_scr, u_scr, h_scr) = refs
    i = pl.program_id(0)
    tile = (n_tiles - 1 - i) if reverse else i

    @pl.when(i == 0)
    def _():
        h_scr[...] = jnp.zeros_like(h_scr)
        a_scr[...] = jnp.zeros_like(a_scr)
        u_scr[...] = jnp.zeros_like(u_scr)

    cw = cw_ref[...]
    cb = cb_ref[...]
    shifts = [k - CONV_LEFT for k in range(CONV_WIDTH) if k != CONV_LEFT]
    rr = lax.broadcasted_iota(jnp.int32, (ts, ts + 2 * HALO), 0)
    cc = lax.broadcasted_iota(jnp.int32, (ts, ts + 2 * HALO), 1)
    sel = jnp.concatenate(
        [jnp.where(cc == rr + HALO + s, 1.0, 0.0).astype(BF16) for s in shifts], axis=0)
    for b in range(nseq):
        prev = xp_ref[b]
        nxt = xn_ref[b]
        prev = jnp.where(tile == 0, jnp.zeros_like(prev), prev)
        nxt = jnp.where(tile == n_tiles - 1, jnp.zeros_like(nxt), nxt)
        cur = xc_ref[b]
        shifted = _dot(sel, jnp.concatenate([prev, cur, nxt], axis=0))
        acc = cb + cur.astype(F32) * cw[CONV_LEFT:CONV_LEFT + 1]
        for n, s in enumerate(shifts):
            k = s + CONV_LEFT
            acc = acc + shifted[n * ts:(n + 1) * ts] * cw[k:k + 1]
        conv_scr[b * ts:(b + 1) * ts, :] = acc
        convb_scr[b * ts:(b + 1) * ts, :] = acc.astype(BF16)

    for j in range(N_CT):
        st = GATE_STARTS[j]
        sl = slice(LANES * j, LANES * (j + 1))
        pre = _dot(convb_scr[:, st:st + GATE_WIDTH], wp_ref[j])
        tr = jnp.tanh(pre[:, :LANES] + ba_ref[:, sl])
        ti = jnp.tanh(pre[:, LANES:] + bx_ref[:, sl])
        z = -lam_ref[:, sl]
        softplus = jnp.maximum(z, 0.0) + jnp.log(1.0 + jnp.exp(-jnp.abs(z)))
        half = (-0.5 * LRU_C * LOG2E) * softplus
        a = jnp.exp2(half + half * tr)
        y = 1.0 - a * a
        root = jnp.where(y > 0.0, y * lax.rsqrt(y), 0.0)
        hx = 0.5 * conv_scr[:, sl]
        u = root * (hx + hx * ti)
        for b in range(nseq):
            a_scr[j, pl.ds(b, ts, stride=SUBLANES), :] = a[b * ts:(b + 1) * ts]
            u_scr[j, pl.ds(b, ts, stride=SUBLANES), :] = u[b * ts:(b + 1) * ts]

    def step(k, h):
        t = (ts - 1 - k) if reverse else k
        row = pl.multiple_of(t * SUBLANES, SUBLANES)
        h = a_scr[:, pl.ds(row, SUBLANES), :] * h + u_scr[:, pl.ds(row, SUBLANES), :]
        u_scr[:, pl.ds(row, SUBLANES), :] = h
        return h

    h_scr[...] = lax.fori_loop(0, ts, step, h_scr[...], unroll=8)

    for b in range(nseq):
        for j in range(N_CT):
            sl = slice(LANES * j, LANES * (j + 1))
            y = u_scr[j, pl.ds(b, ts, stride=SUBLANES), :]
            if reverse:
                y = gate_ref[b, :, sl].astype(F32) * (y + hsf_ref[b, :, sl].astype(F32))
            o_ref[b, :, sl] = y.astype(o_ref.dtype)


def _rglru_scan(y_in, conv_w, conv_b, wp, b_a, b_x, lam, hs_fwd, *, ts, nseq, reverse):
    n_tiles = SEQ // ts
    per_tile = ts // HALO

    def tmap(i):
        return (n_tiles - 1 - i) if reverse else i

    in_specs = [
        pl.BlockSpec((nseq, HALO, D_RNN), lambda i: (0, jnp.maximum(tmap(i) * per_tile - 1, 0), 1)),
        pl.BlockSpec((nseq, ts, D_RNN), lambda i: (0, tmap(i), 1)),
        pl.BlockSpec((nseq, HALO, D_RNN),
                     lambda i: (0, jnp.minimum((tmap(i) + 1) * per_tile, SEQ // HALO - 1), 1)),
        pl.BlockSpec((CONV_WIDTH, D_RNN), lambda i: (0, 0)),
        pl.BlockSpec((1, D_RNN), lambda i: (0, 0)),
        pl.BlockSpec((N_CT, GATE_WIDTH, 2 * LANES), lambda i: (0, 0, 0)),
        pl.BlockSpec((1, D_RNN), lambda i: (0, 0)),
        pl.BlockSpec((1, D_RNN), lambda i: (0, 0)),
        pl.BlockSpec((1, D_RNN), lambda i: (0, 0)),
    ]
    args = [y_in, y_in, y_in, conv_w, conv_b, wp, b_a, b_x, lam]
    if reverse:
        in_specs += [
            pl.BlockSpec((nseq, ts, D_RNN), lambda i: (0, tmap(i), 0)),
            pl.BlockSpec((nseq, ts, D_RNN), lambda i: (0, tmap(i), 0)),
        ]
        args += [hs_fwd, y_in]
    return pl.pallas_call(
        functools.partial(_scan_kernel, ts=ts, n_tiles=n_tiles, nseq=nseq, reverse=reverse),
        grid=(n_tiles,),
        in_specs=in_specs,
        out_specs=pl.BlockSpec((nseq, ts, D_RNN), lambda i: (0, tmap(i), 0)),
        out_shape=jax.ShapeDtypeStruct((nseq, SEQ, D_RNN), BF16),
        scratch_shapes=[
            pltpu.VMEM((nseq * ts, D_RNN), F32),
            pltpu.VMEM((nseq * ts, D_RNN), BF16),
            pltpu.VMEM((N_CT, ts * SUBLANES, LANES), F32),
            pltpu.VMEM((N_CT, ts * SUBLANES, LANES), F32),
            pltpu.VMEM((N_CT, SUBLANES, LANES), F32),
        ],
        compiler_params=_params(("arbitrary",)),
        name="rglru_scan_bwd" if reverse else "rglru_scan_fwd",
    )(*args)


def kernel(x_prompt, x_sample, mem_prompt, mem_sample, norms, rel_bias, a_w_in, a_ln_g, a_ln_b, a_w_s, a_b_s, a_w_out, b_w_qkv, b_sink, b_w_o, c_w_in, c_conv_w, c_conv_b, c_w_a, c_b_a, c_w_x, c_b_x, c_lam, c_w_out, m_w_q, m_w_kv, m_w_o, f_w_gate_up, f_w_down):
    n_prompt = x_prompt.shape[0]
    nseq = n_prompt + x_sample.shape[0]
    assert nseq <= SUBLANES
    t = nseq * SEQ
    t_prompt = n_prompt * SEQ
    x = None
    x_parts = [x_prompt.reshape(t_prompt, D_MODEL), x_sample.reshape(t - t_prompt, D_MODEL)]
    mem = jnp.concatenate([mem_prompt, mem_sample], axis=0)

    def gain(i, k):
        return norms[i, k].reshape(1, D_MODEL)

    a_w_in_b, a_w_out_b = a_w_in.astype(BF16), a_w_out.astype(BF16)
    q_cols = N_HEADS * HEAD_DIM
    b_w_qkv_b = jnp.concatenate(
        [b_w_qkv[:, :, :q_cols] * (HEAD_DIM ** -0.5 * LOG2E), b_w_qkv[:, :, q_cols:]],
        axis=-1).astype(BF16)
    b_w_o_b = b_w_o.astype(BF16)
    c_w_in_b, c_w_out_b = c_w_in.astype(BF16), c_w_out.astype(BF16)
    m_w_q_b, m_w_o_b = m_w_q.astype(BF16), m_w_o.astype(BF16)
    f_w_gate_up_b, f_w_down_b = f_w_gate_up.astype(BF16), f_w_down.astype(BF16)

    kv_all = _kv_proj(mem, norms[:, 4].reshape(DEPTH, 1, D_MODEL), m_w_kv.astype(BF16))

    for i in range(DEPTH):
        kind, j = i % N_MIXERS, i // N_MIXERS
        if kind == 0:
            parts = x_parts if x is None else [x]
            z, mu, rstd = _gmlp_in(parts, gain(i, 0), a_w_in_b, j, tm=1024,
                                   tn=(4 if len(parts) == 1 else 2) * GMLP_GDIM)
            x = _gmlp_out(z, mu, rstd, a_ln_g[j].reshape(GMLP_GROUPS, 1, GMLP_GDIM),
                          a_ln_b[j].reshape(GMLP_GROUPS, 1, GMLP_GDIM),
                          a_w_s[j].astype(BF16), a_b_s[j].reshape(GMLP_GROUPS, BLOCK, 1),
                          a_w_out_b, j, parts, gain(i, 1), tm=512)
        elif kind == 1:
            qkv = _norm_matmul(x, gain(i, 0), b_w_qkv_b, j, tm=1024, tn=1024)
            o = _win_attention(qkv, b_sink[j] * LOG2E, _band_bias(rel_bias), nseq=nseq, r_blocks=4)
            x = _matmul_norm_res(o, b_w_o_b, j, x, gain(i, 1), tm=512)
        else:
            y_in = _norm_matmul(x, gain(i, 0), c_w_in_b, j, tm=1024, tn=D_RNN // 2, n_gelu=2)
            y_in = y_in.reshape(nseq, SEQ, 2 * D_RNN)
            scan_args = (c_conv_w[j], c_conv_b[j].reshape(1, D_RNN))
            hs = None
            for d in range(2):
                hs = _rglru_scan(y_in, *scan_args, _gate_weights(c_w_a[j, d], c_w_x[j, d]),
                                 (c_b_a[j, d] * 0.5).reshape(1, D_RNN),
                                 (c_b_x[j, d] * 0.5).reshape(1, D_RNN),
                                 c_lam[j, d].reshape(1, D_RNN), hs, ts=64, nseq=nseq,
                                 reverse=(d == 1))
            x = _matmul_norm_res(hs.reshape(t, D_RNN), c_w_out_b, j, x, gain(i, 1), tm=512)
        x = _mem_attention(x, gain(i, 2), gain(i, 3), m_w_q_b, kv_all, m_w_o_b, i, tm=1024)
        ffn = functools.partial(_ffn, x, gain(i, 5), gain(i, 6), f_w_gate_up_b, f_w_down_b, i)
        if i < DEPTH - 1:
            x = ffn(tm=768, tn=512)

    y_prompt = ffn(tm=1024, tn=256, row0=0, rows=t_prompt).reshape(n_prompt, SEQ, D_MODEL)
    y_sample = ffn(tm=1024, tn=256, row0=t_prompt, rows=t - t_prompt).reshape(
        nseq - n_prompt, SEQ, D_MODEL)
    return (y_prompt, y_sample)
```

```python
import functools

import numpy as np
import jax
import jax.numpy as jnp
from jax import lax
from jax.experimental import pallas as pl
from jax.experimental.pallas import tpu as pltpu

D_MODEL = 2048
SEQ = 4096
DEPTH = 4
N_MIXERS = 3
BLOCK = 128
HEAD_DIM = 128
EPS = 1e-6
NEG_INF = -1e30
GMLP_HALF = 2 * D_MODEL
GMLP_GROUPS = 8
GMLP_GDIM = GMLP_HALF // GMLP_GROUPS
N_HEADS = D_MODEL // HEAD_DIM
N_KV_HEADS = 4
KV_GROUP = N_HEADS // N_KV_HEADS
WINDOW = 128
N_BUCKETS = 32
MAX_DISTANCE = 128
D_RNN = 2816
RNN_BLOCKS = 16
RNN_BDIM = D_RNN // RNN_BLOCKS
CONV_WIDTH = 4
CONV_LEFT = 2
LRU_C = 8.0
MEM_LEN = 256
MEM_HEADS = 4
MEM_WIDTH = MEM_HEADS * HEAD_DIM
D_FF = 5632

LANES = 128
SUBLANES = 8
VMEM_LIMIT = 56 * 1024 * 1024

F32 = jnp.float32
BF16 = jnp.bfloat16
LOG2E = float(np.log2(np.e))


def _params(sem):
    return pltpu.CompilerParams(dimension_semantics=sem, vmem_limit_bytes=VMEM_LIMIT)


def _rms(x, g):
    ms = jnp.mean(x * x, axis=-1, keepdims=True)
    return x * lax.rsqrt(ms + EPS) * g


def _gelu(x):
    k1 = float(-2.0 * np.sqrt(2.0 / np.pi) * LOG2E)
    return x / (1.0 + jnp.exp2(x * (k1 + (k1 * 0.044715) * (x * x))))


def _sigmoid(x):
    return 1.0 / (1.0 + jnp.exp(-x))


def _dot(a, b):
    return jnp.dot(a, b, preferred_element_type=F32)


def _norm_mm_kernel(x_ref, g_ref, w_ref, o_ref, h_scr, *, n_gelu):
    j = pl.program_id(1)

    def emit(act, first):
        if first:
            h_scr[...] = _rms(x_ref[...], g_ref[...]).astype(BF16)
        z = _dot(h_scr[...], w_ref[...])
        o_ref[...] = (_gelu(z) if act else z).astype(o_ref.dtype)

    pl.when(j == 0)(functools.partial(emit, n_gelu > 0, True))
    if n_gelu > 1:
        pl.when((j > 0) & (j < n_gelu))(functools.partial(emit, True, False))
    pl.when(j >= max(n_gelu, 1))(functools.partial(emit, False, False))


def _norm_matmul(x, g, w, layer, *, tm, tn, n_gelu=0):
    t, d = x.shape
    n = w.shape[2]
    return pl.pallas_call(
        functools.partial(_norm_mm_kernel, n_gelu=n_gelu),
        grid=(t // tm, n // tn),
        in_specs=[
            pl.BlockSpec((tm, d), lambda i, j: (i, 0)),
            pl.BlockSpec((1, d), lambda i, j: (0, 0)),
            pl.BlockSpec((None, d, tn), lambda i, j: (layer, 0, j)),
        ],
        out_specs=pl.BlockSpec((tm, tn), lambda i, j: (i, j)),
        out_shape=jax.ShapeDtypeStruct((t, n), BF16),
        scratch_shapes=[pltpu.VMEM((tm, d), BF16)],
        compiler_params=_params(("parallel", "arbitrary")),
        name="norm_matmul",
    )(x, g, w)


def _part_specs(parts, tm, tile_of):
    specs, ranges, lo = [], [], 0
    for p in parts:
        n = p.shape[0] // tm
        specs.append(pl.BlockSpec(
            (tm, p.shape[1]),
            lambda *idx, lo=lo, n=n: (jnp.clip(tile_of(*idx) - lo, 0, n - 1), 0)))
        ranges.append((lo, lo + n))
        lo += n
    return specs, tuple(ranges)


def _for_owner(i, ranges, refs, cond, fn):
    for (lo, hi), ref in zip(ranges, refs):
        pl.when(cond & (i >= lo) & (i < hi))(functools.partial(fn, ref))


MXU_COLS = 256


def _gmlp_in_kernel(*refs, ranges, n_col, tn):
    n_parts = len(ranges)
    x_refs = refs[:n_parts]
    g_ref, w_ref, z_ref, mu_ref, rs_ref, h_scr, s1_scr, s2_scr = refs[n_parts:]
    i, j = pl.program_id(0), pl.program_id(1)
    tm = h_scr.shape[0]

    def step(is_v, x_ref=None):
        if x_ref is not None:
            h_scr[...] = _rms(x_ref[...], g_ref[...]).astype(BF16)
            s1_scr[...] = jnp.zeros_like(s1_scr)
            s2_scr[...] = jnp.zeros_like(s2_scr)
        p1 = jnp.zeros((tm, LANES), F32)
        p2 = jnp.zeros((tm, LANES), F32)
        per_tile = GMLP_GDIM // MXU_COLS
        for c in range(tn // MXU_COLS):
            cols = slice(c * MXU_COLS, (c + 1) * MXU_COLS)
            zc = _gelu(_dot(h_scr[...], w_ref[:, cols]))
            lo = (c % per_tile) * MXU_COLS
            z_ref[c // per_tile, :, lo:lo + MXU_COLS] = zc.astype(z_ref.dtype)
            if is_v:
                for k in range(MXU_COLS // LANES):
                    zk = zc[:, k * LANES:(k + 1) * LANES]
                    p1 = p1 + zk
                    p2 = p2 + zk * zk
        if is_v:
            s1_scr[...] += p1
            s2_scr[...] += p2

    _for_owner(i, ranges, x_refs, j == 0, functools.partial(step, False))
    pl.when((j > 0) & (j < n_col // 2))(functools.partial(step, False))
    pl.when(j >= n_col // 2)(functools.partial(step, True))

    @pl.when(j == n_col - 1)
    def _():
        mu = jnp.sum(s1_scr[...], axis=-1, keepdims=True) * (1.0 / GMLP_HALF)
        msq = jnp.sum(s2_scr[...], axis=-1, keepdims=True) * (1.0 / GMLP_HALF)
        mu_ref[...] = mu
        rs_ref[...] = lax.rsqrt(jnp.maximum(msq - mu * mu, 0.0) + EPS)


def _gmlp_in(x_parts, g, w, layer, *, tm, tn):
    t = sum(p.shape[0] for p in x_parts)
    d = x_parts[0].shape[1]
    n_col = 2 * GMLP_HALF // tn
    sub = tn // GMLP_GDIM
    x_specs, ranges = _part_specs(x_parts, tm, lambda i, j: i)
    return pl.pallas_call(
        functools.partial(_gmlp_in_kernel, ranges=ranges, n_col=n_col, tn=tn),
        grid=(t // tm, n_col),
        in_specs=x_specs + [
            pl.BlockSpec((1, d), lambda i, j: (0, 0)),
            pl.BlockSpec((None, d, tn), lambda i, j: (layer, 0, j)),
        ],
        out_specs=[
            pl.BlockSpec((sub, tm, GMLP_GDIM), lambda i, j: (j, i, 0)),
            pl.BlockSpec((tm, 1), lambda i, j: (i, 0)),
            pl.BlockSpec((tm, 1), lambda i, j: (i, 0)),
        ],
        out_shape=[
            jax.ShapeDtypeStruct((2 * GMLP_GROUPS, t, GMLP_GDIM), BF16),
            jax.ShapeDtypeStruct((t, 1), F32),
            jax.ShapeDtypeStruct((t, 1), F32),
        ],
        scratch_shapes=[
            pltpu.VMEM((tm, d), BF16),
            pltpu.VMEM((tm, LANES), F32),
            pltpu.VMEM((tm, LANES), F32),
        ],
        compiler_params=_params(("parallel", "arbitrary")),
        name="gmlp_in",
    )(*x_parts, g, w)


def _mm_norm_res_kernel(a_ref, w_ref, x_ref, g_ref, o_ref):
    y = _dot(a_ref[...], w_ref[...])
    o_ref[...] = x_ref[...] + _rms(y, g_ref[...])


def _matmul_norm_res(a, w, layer, x, g, *, tm):
    t, k = a.shape
    d = w.shape[2]
    return pl.pallas_call(
        _mm_norm_res_kernel,
        grid=(t // tm,),
        in_specs=[
            pl.BlockSpec((tm, k), lambda i: (i, 0)),
            pl.BlockSpec((None, k, d), lambda i: (layer, 0, 0), pipeline_mode=pl.Buffered(1)),
            pl.BlockSpec((tm, d), lambda i: (i, 0)),
            pl.BlockSpec((1, d), lambda i: (0, 0)),
        ],
        out_specs=pl.BlockSpec((tm, d), lambda i: (i, 0)),
        out_shape=jax.ShapeDtypeStruct((t, d), F32),
        compiler_params=_params(("parallel",)),
        name="matmul_norm_res",
    )(a, w, x, g)


PAIRS = GMLP_GROUPS // 2


def _gmlp_out_kernel(*refs, ranges, tm):
    n_parts = len(ranges)
    (u1_ref, v1_ref, mu1_ref, rs1_ref, lng1_ref, lnb1_ref, ws1_ref, bs1_ref,
     u2_ref, v2_ref, mu2_ref, rs2_ref, lng2_ref, lnb2_ref, ws2_ref, bs2_ref,
     wo_ref, g_ref) = refs[:18]
    x_refs = refs[18:18 + n_parts]
    o_ref, a0_scr, a1_scr = refs[18 + n_parts:]
    s = pl.program_id(0)
    q = jnp.maximum(s - 1, 0)
    tile, pair = q // PAIRS, q % PAIRS

    @pl.when(s == 0)
    def _():
        a0_scr[...] = jnp.zeros_like(a0_scr)
        a1_scr[...] = jnp.zeros_like(a1_scr)

    @pl.when(pair == 0)
    def _():
        o_ref[...] = jnp.zeros_like(o_ref)

    def prep(u_ref, v_ref, mu_ref, rs_ref, lng_ref, lnb_ref, ws_ref, bs_ref):
        vn = ((v_ref[0].astype(F32) - mu_ref[...]) * rs_ref[...] * lng_ref[0]
              + lnb_ref[0]).astype(BF16)
        ws = ws_ref[0]
        bs = bs_ref[0]
        parts = []
        for n in range(tm // BLOCK):
            rows = slice(n * BLOCK, (n + 1) * BLOCK)
            mixed = _dot(ws, vn[rows, :]) + bs
            parts.append((u_ref[0, rows, :].astype(F32) * mixed).astype(BF16))
        return jnp.concatenate(parts, axis=0)

    row0 = pl.multiple_of(pair * (2 * GMLP_GDIM), 2 * GMLP_GDIM)
    o_ref[...] += _dot(a0_scr[...], wo_ref[pl.ds(row0, GMLP_GDIM), :])
    a1_scr[...] = prep(u1_ref, v1_ref, mu1_ref, rs1_ref, lng1_ref, lnb1_ref, ws1_ref, bs1_ref)
    o_ref[...] += _dot(a1_scr[...], wo_ref[pl.ds(row0 + GMLP_GDIM, GMLP_GDIM), :])
    a0_scr[...] = prep(u2_ref, v2_ref, mu2_ref, rs2_ref, lng2_ref, lnb2_ref, ws2_ref, bs2_ref)

    def epilogue(x_ref):
        o_ref[...] = x_ref[...] + _rms(o_ref[...], g_ref[...])

    _for_owner(tile, ranges, x_refs, (pair == PAIRS - 1) & (s > 0), epilogue)


def _gmlp_out(z, mu, rstd, ln_g, ln_b, w_s, b_s, w_out, layer, x_parts, g, *, tm):
    t = sum(p.shape[0] for p in x_parts)
    d = x_parts[0].shape[1]
    n_tiles = t // tm
    last = GMLP_GROUPS * n_tiles - 1

    def proj_tile(s):
        return jnp.maximum(s - 1, 0) // PAIRS

    def prep_specs(pair_of):
        def at(f):
            return lambda s: f(*pair_of(s))
        return [
            pl.BlockSpec((1, tm, GMLP_GDIM), at(lambda i, c: (c, i, 0))),
            pl.BlockSpec((1, tm, GMLP_GDIM), at(lambda i, c: (GMLP_GROUPS + c, i, 0))),
            pl.BlockSpec((tm, 1), at(lambda i, c: (i, 0))),
            pl.BlockSpec((tm, 1), at(lambda i, c: (i, 0))),
            pl.BlockSpec((1, 1, GMLP_GDIM), at(lambda i, c: (c, 0, 0))),
            pl.BlockSpec((1, 1, GMLP_GDIM), at(lambda i, c: (c, 0, 0))),
            pl.BlockSpec((1, BLOCK, BLOCK), at(lambda i, c: (c, 0, 0))),
            pl.BlockSpec((1, BLOCK, 1), at(lambda i, c: (c, 0, 0))),
        ]

    def split(gidx):
        gidx = jnp.clip(gidx, 0, last)
        return gidx // GMLP_GROUPS, gidx % GMLP_GROUPS

    x_specs, ranges = _part_specs(x_parts, tm, proj_tile)
    prep_args = (z, z, mu, rstd, ln_g, ln_b, w_s, b_s)
    return pl.pallas_call(
        functools.partial(_gmlp_out_kernel, ranges=ranges, tm=tm),
        grid=(PAIRS * n_tiles + 1,),
        in_specs=prep_specs(lambda s: split(2 * s - 1)) + prep_specs(lambda s: split(2 * s)) + [
            pl.BlockSpec((None, GMLP_HALF, d), lambda s: (layer, 0, 0),
                         pipeline_mode=pl.Buffered(1)),
            pl.BlockSpec((1, d), lambda s: (0, 0)),
        ] + x_specs,
        out_specs=pl.BlockSpec((tm, d), lambda s: (proj_tile(s), 0)),
        out_shape=jax.ShapeDtypeStruct((t, d), F32),
        scratch_shapes=[
            pltpu.VMEM((tm, GMLP_GDIM), BF16),
            pltpu.VMEM((tm, GMLP_GDIM), BF16),
        ],
        compiler_params=_params(("arbitrary",)),
        name="gmlp_out",
    )(*prep_args, *prep_args, w_out, g, *x_parts)


def _win_attn_kernel(sink_ref, q_ref, kp_ref, kc_ref, kn_ref, vp_ref, vc_ref, vn_ref, bias_ref,
                     o_ref, *, r_blocks, n_steps):
    i = pl.program_id(1)
    kall = jnp.concatenate([kp_ref[0], kc_ref[0], kn_ref[0]], axis=0)
    vall = jnp.concatenate([vp_ref[0], vc_ref[0], vn_ref[0]], axis=0)
    col = lax.broadcasted_iota(jnp.int32, (1, 3 * BLOCK), 1)
    sk = jnp.concatenate(
        [jnp.full((BLOCK, 1), sink_ref[h], F32) for h in range(N_HEADS)], axis=0)
    hrows = KV_GROUP * BLOCK
    for r in range(r_blocks):
        rows = slice(r * BLOCK, (r + 1) * BLOCK)
        oob = None
        if r == 0:
            oob = (col < BLOCK) & (i == 0)
        if r == r_blocks - 1:
            hi = (col >= 2 * BLOCK) & (i == n_steps - 1)
            oob = hi if oob is None else (oob | hi)
        parts = []
        for kh in range(N_KV_HEADS):
            kb = kall[r * BLOCK:(r + 3) * BLOCK, kh * HEAD_DIM:(kh + 1) * HEAD_DIM]
            qs = jnp.concatenate(
                [q_ref[0, rows, h * HEAD_DIM:(h + 1) * HEAD_DIM]
                 for h in range(kh * KV_GROUP, (kh + 1) * KV_GROUP)], axis=0)
            parts.append(
                lax.dot_general(qs, kb, (((1,), (1,)), ((), ())), preferred_element_type=F32))
        s = jnp.concatenate(parts, axis=0) + bias_ref[...]
        if oob is not None:
            s = jnp.where(oob, NEG_INF, s)
        m = jnp.maximum(jnp.max(s, axis=-1, keepdims=True), sk)
        p = jnp.exp2(s - m)
        inv = 1.0 / (jnp.sum(p, axis=-1, keepdims=True) + jnp.exp2(sk - m))
        pb = p.astype(BF16)
        for kh in range(N_KV_HEADS):
            vb = vall[r * BLOCK:(r + 3) * BLOCK, kh * HEAD_DIM:(kh + 1) * HEAD_DIM]
            hsl = slice(kh * hrows, (kh + 1) * hrows)
            o = _dot(pb[hsl], vb) * inv[hsl]
            for gq in range(KV_GROUP):
                h = kh * KV_GROUP + gq
                o_ref[0, rows, h * HEAD_DIM:(h + 1) * HEAD_DIM] = (
                    o[gq * BLOCK:(gq + 1) * BLOCK].astype(o_ref.dtype))


def _win_attention(qkv, sink, bias, *, nseq, r_blocks):
    tq = r_blocks * BLOCK
    n_steps = SEQ // tq
    nb = SEQ // BLOCK
    qkv3 = qkv.reshape(nseq, SEQ, qkv.shape[-1])
    kcol = N_HEADS * HEAD_DIM // (N_KV_HEADS * HEAD_DIM)
    vcol = kcol + 1
    kvw = N_KV_HEADS * HEAD_DIM

    def prev_map(c):
        return lambda b, i: (b, jnp.maximum(i * r_blocks - 1, 0), c)

    def cur_map(c):
        return lambda b, i: (b, i, c)

    def next_map(c):
        return lambda b, i: (b, jnp.minimum((i + 1) * r_blocks, nb - 1), c)

    out = pl.pallas_call(
        functools.partial(_win_attn_kernel, r_blocks=r_blocks, n_steps=n_steps),
        grid=(nseq, n_steps),
        in_specs=[
            pl.BlockSpec(memory_space=pltpu.SMEM),
            pl.BlockSpec((1, tq, N_HEADS * HEAD_DIM), lambda b, i: (b, i, 0)),
            pl.BlockSpec((1, BLOCK, kvw), prev_map(kcol)),
            pl.BlockSpec((1, tq, kvw), cur_map(kcol)),
            pl.BlockSpec((1, BLOCK, kvw), next_map(kcol)),
            pl.BlockSpec((1, BLOCK, kvw), prev_map(vcol)),
            pl.BlockSpec((1, tq, kvw), cur_map(vcol)),
            pl.BlockSpec((1, BLOCK, kvw), next_map(vcol)),
            pl.BlockSpec((N_HEADS * BLOCK, 3 * BLOCK), lambda b, i: (0, 0)),
        ],
        out_specs=pl.BlockSpec((1, tq, N_HEADS * HEAD_DIM), lambda b, i: (b, i, 0)),
        out_shape=jax.ShapeDtypeStruct((nseq, SEQ, N_HEADS * HEAD_DIM), BF16),
        compiler_params=_params(("parallel", "parallel")),
        name="win_attention",
    )(sink, qkv3, qkv3, qkv3, qkv3, qkv3, qkv3, qkv3, bias)
    return out.reshape(nseq * SEQ, N_HEADS * HEAD_DIM)


def _t5_bucket(rel):
    half = N_BUCKETS // 2
    exact = half // 2
    n = np.abs(rel)
    large = exact + (np.log(np.maximum(n, 1) / exact) / np.log(MAX_DISTANCE / exact)
                     * (half - exact)).astype(np.int32)
    large = np.minimum(large, half - 1)
    return (np.where(rel > 0, half, 0) + np.where(n < exact, n, large)).astype(np.int32)


def _band_bias(rel_table):
    rel = np.arange(3 * BLOCK)[None, :] - BLOCK - np.arange(BLOCK)[:, None]
    onehot = (_t5_bucket(rel).reshape(1, -1) == np.arange(N_BUCKETS)[:, None]).astype(np.float32)
    bias = jnp.einsum("bh,bn->hn", rel_table.astype(F32) * LOG2E, onehot,
                      precision=lax.Precision.HIGHEST)
    bias = jnp.where((np.abs(rel) <= WINDOW).reshape(1, -1), bias, NEG_INF)
    return bias.reshape(N_HEADS * BLOCK, 3 * BLOCK)


def _kv_kernel(mem_ref, g_ref, w_ref, o_ref):
    h = _rms(mem_ref[0], g_ref[0]).astype(BF16)
    o_ref[0, 0] = _dot(h, w_ref[0]).astype(o_ref.dtype)


def _kv_proj(mem, g, w_kv):
    nseq = mem.shape[0]
    return pl.pallas_call(
        _kv_kernel,
        grid=(DEPTH, nseq),
        in_specs=[
            pl.BlockSpec((1, MEM_LEN, D_MODEL), lambda l, b: (b, 0, 0)),
            pl.BlockSpec((1, 1, D_MODEL), lambda l, b: (l, 0, 0)),
            pl.BlockSpec((1, D_MODEL, 2 * MEM_WIDTH), lambda l, b: (l, 0, 0)),
        ],
        out_specs=pl.BlockSpec((1, 1, MEM_LEN, 2 * MEM_WIDTH), lambda l, b: (l, b, 0, 0)),
        out_shape=jax.ShapeDtypeStruct((DEPTH, nseq, MEM_LEN, 2 * MEM_WIDTH), BF16),
        compiler_params=_params(("parallel", "parallel")),
        name="kv_proj",
    )(mem, g, w_kv)


def _mem_attn_kernel(x_ref, gpre_ref, gpost_ref, wq_ref, kv_ref, wo_ref, o_ref):
    x = x_ref[...]
    h = _rms(x, gpre_ref[...]).astype(BF16)
    q = (_dot(h, wq_ref[...]) * (HEAD_DIM ** -0.5)).astype(BF16)
    outs = []
    for hd in range(MEM_HEADS):
        sl = slice(hd * HEAD_DIM, (hd + 1) * HEAD_DIM)
        k = kv_ref[0, :, sl]
        v = kv_ref[0, :, MEM_WIDTH + hd * HEAD_DIM:MEM_WIDTH + (hd + 1) * HEAD_DIM]
        s = lax.dot_general(q[:, sl], k, (((1,), (1,)), ((), ())), preferred_element_type=F32)
        m = jnp.max(s, axis=-1, keepdims=True)
        p = jnp.exp(s - m)
        denom = jnp.sum(p, axis=-1, keepdims=True)
        outs.append((_dot(p.astype(BF16), v) / denom).astype(BF16))
    y = _dot(jnp.concatenate(outs, axis=-1), wo_ref[...])
    o_ref[...] = x + _rms(y, gpost_ref[...])


def _mem_attention(x, gpre, gpost, w_q, kv, w_o, layer, *, tm):
    t, d = x.shape
    per_seq = SEQ // tm
    return pl.pallas_call(
        _mem_attn_kernel,
        grid=(t // tm,),
        in_specs=[
            pl.BlockSpec((tm, d), lambda i: (i, 0)),
            pl.BlockSpec((1, d), lambda i: (0, 0)),
            pl.BlockSpec((1, d), lambda i: (0, 0)),
            pl.BlockSpec((None, d, MEM_WIDTH), lambda i: (layer, 0, 0)),
            pl.BlockSpec((None, 1, MEM_LEN, 2 * MEM_WIDTH), lambda i: (layer, i // per_seq, 0, 0)),
            pl.BlockSpec((None, MEM_WIDTH, d), lambda i: (layer, 0, 0)),
        ],
        out_specs=pl.BlockSpec((tm, d), lambda i: (i, 0)),
        out_shape=jax.ShapeDtypeStruct((t, d), F32),
        compiler_params=_params(("parallel",)),
        name="mem_attention",
    )(x, gpre, gpost, w_q, kv, w_o)


def _ffn_kernel(x_ref, gpre_ref, gpost_ref, wg_ref, wu_ref, wd_ref, o_ref, h_scr, *, n_chunks):
    j = pl.program_id(1)

    def chunk(first, last):
        if first:
            h_scr[...] = _rms(x_ref[...], gpre_ref[...]).astype(BF16)
        h = h_scr[...]
        gate = _dot(h, wg_ref[...])
        up = _dot(h, wu_ref[...])
        a = (gate * _sigmoid(gate) * up).astype(BF16)
        if first:
            o_ref[...] = _dot(a, wd_ref[...])
        else:
            o_ref[...] += _dot(a, wd_ref[...])
        if last:
            o_ref[...] = x_ref[...] + _rms(o_ref[...], gpost_ref[...])

    pl.when(j == 0)(functools.partial(chunk, True, False))
    pl.when((j > 0) & (j < n_chunks - 1))(functools.partial(chunk, False, False))
    pl.when(j == n_chunks - 1)(functools.partial(chunk, False, True))


def _ffn(x, gpre, gpost, w_gate_up, w_down, layer, *, tm, tn, row0=0, rows=None):
    d = x.shape[1]
    t = x.shape[0] if rows is None else rows
    tile0 = row0 // tm
    n_chunks = D_FF // tn
    x_bytes = tm * d * 4
    fixed = 2 * x_bytes + tm * d * 2 + 2 * 3 * d * tn * 2
    temps = 2 * tm * tn * 4 + tm * tn * 2 + tm * d * 4
    x_mode = None if fixed + 2 * x_bytes + temps <= VMEM_LIMIT else pl.Buffered(1)
    return pl.pallas_call(
        functools.partial(_ffn_kernel, n_chunks=n_chunks),
        grid=(t // tm, n_chunks),
        in_specs=[
            pl.BlockSpec((tm, d), lambda i, j: (i + tile0, 0), pipeline_mode=x_mode),
            pl.BlockSpec((1, d), lambda i, j: (0, 0)),
            pl.BlockSpec((1, d), lambda i, j: (0, 0)),
            pl.BlockSpec((None, d, tn), lambda i, j: (layer, 0, j)),
            pl.BlockSpec((None, d, tn), lambda i, j: (layer, 0, j + n_chunks)),
            pl.BlockSpec((None, tn, d), lambda i, j: (layer, j, 0)),
        ],
        out_specs=pl.BlockSpec((tm, d), lambda i, j: (i, 0)),
        out_shape=jax.ShapeDtypeStruct((t, d), F32),
        scratch_shapes=[pltpu.VMEM((tm, d), BF16)],
        compiler_params=_params(("parallel", "arbitrary")),
        name="ffn",
    )(x, gpre, gpost, w_gate_up, w_gate_up, w_down)


N_CT = D_RNN // LANES
HALO = 2 * SUBLANES


def _gate_windows():
    lo_hi = []
    for j in range(N_CT):
        h0 = (LANES * j) // RNN_BDIM
        h1 = (LANES * j + LANES - 1) // RNN_BDIM
        lo = (RNN_BDIM * h0) // LANES * LANES
        hi = -(-(RNN_BDIM * (h1 + 1)) // LANES) * LANES
        lo_hi.append((lo, hi))
    width = max(hi - lo for lo, hi in lo_hi)
    starts = tuple(min(lo, D_RNN - width) for lo, _ in lo_hi)
    return starts, width


GATE_STARTS, GATE_WIDTH = _gate_windows()


def _gate_weights(w_a, w_x):
    def tile(w, j):
        st = GATE_STARTS[j]
        pieces = []
        for h in range((LANES * j) // RNN_BDIM, (LANES * j + LANES - 1) // RNN_BDIM + 1):
            c0 = max(LANES * j, RNN_BDIM * h) - RNN_BDIM * h
            c1 = min(LANES * (j + 1), RNN_BDIM * (h + 1)) - RNN_BDIM * h
            top = RNN_BDIM * h - st
            pieces.append(jnp.pad(w[h, :, c0:c1], ((top, GATE_WIDTH - RNN_BDIM - top), (0, 0))))
        return jnp.concatenate(pieces, axis=1)

    wa, wx = (w_a * 0.5).astype(BF16), (w_x * 0.5).astype(BF16)
    return jnp.stack([jnp.concatenate([tile(wa, j), tile(wx, j)], axis=1) for j in range(N_CT)])


def _scan_kernel(*refs, ts, n_tiles, nseq, reverse):
    if reverse:
        (xp_ref, xc_ref, xn_ref, cw_ref, cb_ref, wp_ref, ba_ref, bx_ref, lam_ref,
         hsf_ref, gate_ref, o_ref, conv_scr, convb_scr, a_scr, u_scr, h_scr) = refs
    else:
        (xp_ref, xc_ref, xn_ref, cw_ref, cb_ref, wp_ref, ba_ref, bx_ref, lam_ref,
         o_ref, conv_scr, convb_scr, a_scr, u_scr, h_scr) = refs
    i = pl.program_id(0)
    tile = (n_tiles - 1 - i) if reverse else i

    @pl.when(i == 0)
    def _():
        h_scr[...] = jnp.zeros_like(h_scr)
        a_scr[...] = jnp.zeros_like(a_scr)
        u_scr[...] = jnp.zeros_like(u_scr)

    cw = cw_ref[...]
    cb = cb_ref[...]
    shifts = [k - CONV_LEFT for k in range(CONV_WIDTH) if k != CONV_LEFT]
    rr = lax.broadcasted_iota(jnp.int32, (ts, ts + 2 * HALO), 0)
    cc = lax.broadcasted_iota(jnp.int32, (ts, ts + 2 * HALO), 1)
    sel = jnp.concatenate(
        [jnp.where(cc == rr + HALO + s, 1.0, 0.0).astype(BF16) for s in shifts], axis=0)
    def conv_columns(cols):
        for b in range(nseq):
            prev = xp_ref[b, :, cols]
            nxt = xn_ref[b, :, cols]
            prev = jnp.where(tile == 0, jnp.zeros_like(prev), prev)
            nxt = jnp.where(tile == n_tiles - 1, jnp.zeros_like(nxt), nxt)
            cur = xc_ref[b, :, cols]
            shifted = _dot(sel, jnp.concatenate([prev, cur, nxt], axis=0))
            acc = cb[:, cols] + cur.astype(F32) * cw[CONV_LEFT:CONV_LEFT + 1, cols]
            for n, s in enumerate(shifts):
                k = s + CONV_LEFT
                acc = acc + shifted[n * ts:(n + 1) * ts] * cw[k:k + 1, cols]
            conv_scr[b * ts:(b + 1) * ts, cols] = acc
            convb_scr[b * ts:(b + 1) * ts, cols] = acc.astype(BF16)

    def gate_tile(j):
        st = GATE_STARTS[j]
        sl = slice(LANES * j, LANES * (j + 1))
        pre = _dot(convb_scr[:, st:st + GATE_WIDTH], wp_ref[j])
        tr = jnp.tanh(pre[:, :LANES] + ba_ref[:, sl])
        ti = jnp.tanh(pre[:, LANES:] + bx_ref[:, sl])
        z = -lam_ref[:, sl]
        softplus = jnp.maximum(z, 0.0) + jnp.log(1.0 + jnp.exp(-jnp.abs(z)))
        half = (-0.5 * LRU_C * LOG2E) * softplus
        a = jnp.exp2(half + half * tr)
        y = 1.0 - a * a
        root = jnp.where(y > 0.0, y * lax.rsqrt(y), 0.0)
        hx = 0.5 * conv_scr[:, sl]
        u = root * (hx + hx * ti)
        for b in range(nseq):
            a_scr[j, pl.ds(b, ts, stride=SUBLANES), :] = a[b * ts:(b + 1) * ts]
            u_scr[j, pl.ds(b, ts, stride=SUBLANES), :] = u[b * ts:(b + 1) * ts]

    conv_columns(slice(None))
    for j in range(N_CT):
        gate_tile(j)

    def step(k, h):
        t = (ts - 1 - k) if reverse else k
        row = pl.multiple_of(t * SUBLANES, SUBLANES)
        h = a_scr[:, pl.ds(row, SUBLANES), :] * h + u_scr[:, pl.ds(row, SUBLANES), :]
        u_scr[:, pl.ds(row, SUBLANES), :] = h
        return h

    h_scr[...] = lax.fori_loop(0, ts, step, h_scr[...], unroll=8)

    for b in range(nseq):
        for j in range(N_CT):
            sl = slice(LANES * j, LANES * (j + 1))
            y = u_scr[j, pl.ds(b, ts, stride=SUBLANES), :]
            if reverse:
                y = gate_ref[b, :, sl].astype(F32) * (y + hsf_ref[b, :, sl].astype(F32))
            o_ref[b, :, sl] = y.astype(o_ref.dtype)


def _rglru_scan(y_in, conv_w, conv_b, wp, b_a, b_x, lam, hs_fwd, *, ts, nseq, reverse):
    n_tiles = SEQ // ts
    per_tile = ts // HALO

    def tmap(i):
        return (n_tiles - 1 - i) if reverse else i

    in_specs = [
        pl.BlockSpec((nseq, HALO, D_RNN), lambda i: (0, jnp.maximum(tmap(i) * per_tile - 1, 0), 1)),
        pl.BlockSpec((nseq, ts, D_RNN), lambda i: (0, tmap(i), 1)),
        pl.BlockSpec((nseq, HALO, D_RNN),
                     lambda i: (0, jnp.minimum((tmap(i) + 1) * per_tile, SEQ // HALO - 1), 1)),
        pl.BlockSpec((CONV_WIDTH, D_RNN), lambda i: (0, 0)),
        pl.BlockSpec((1, D_RNN), lambda i: (0, 0)),
        pl.BlockSpec((N_CT, GATE_WIDTH, 2 * LANES), lambda i: (0, 0, 0)),
        pl.BlockSpec((1, D_RNN), lambda i: (0, 0)),
        pl.BlockSpec((1, D_RNN), lambda i: (0, 0)),
        pl.BlockSpec((1, D_RNN), lambda i: (0, 0)),
    ]
    args = [y_in, y_in, y_in, conv_w, conv_b, wp, b_a, b_x, lam]
    if reverse:
        in_specs += [
            pl.BlockSpec((nseq, ts, D_RNN), lambda i: (0, tmap(i), 0)),
            pl.BlockSpec((nseq, ts, D_RNN), lambda i: (0, tmap(i), 0)),
        ]
        args += [hs_fwd, y_in]
    return pl.pallas_call(
        functools.partial(_scan_kernel, ts=ts, n_tiles=n_tiles, nseq=nseq, reverse=reverse),
        grid=(n_tiles,),
        in_specs=in_specs,
        out_specs=pl.BlockSpec((nseq, ts, D_RNN), lambda i: (0, tmap(i), 0)),
        out_shape=jax.ShapeDtypeStruct((nseq, SEQ, D_RNN), BF16),
        scratch_shapes=[
            pltpu.VMEM((nseq * ts, D_RNN), F32),
            pltpu.VMEM((nseq * ts, D_RNN), BF16),
            pltpu.VMEM((N_CT, ts * SUBLANES, LANES), F32),
            pltpu.VMEM((N_CT, ts * SUBLANES, LANES), F32),
            pltpu.VMEM((N_CT, SUBLANES, LANES), F32),
        ],
        compiler_params=_params(("arbitrary",)),
        name="rglru_scan_bwd" if reverse else "rglru_scan_fwd",
    )(*args)


def kernel(x_prompt, x_sample, mem_prompt, mem_sample, norms, rel_bias, a_w_in, a_ln_g, a_ln_b, a_w_s, a_b_s, a_w_out, b_w_qkv, b_sink, b_w_o, c_w_in, c_conv_w, c_conv_b, c_w_a, c_b_a, c_w_x, c_b_x, c_lam, c_w_out, m_w_q, m_w_kv, m_w_o, f_w_gate_up, f_w_down):
    n_prompt = x_prompt.shape[0]
    nseq = n_prompt + x_sample.shape[0]
    assert nseq <= SUBLANES
    t = nseq * SEQ
    t_prompt = n_prompt * SEQ
    x = None
    x_parts = [x_prompt.reshape(t_prompt, D_MODEL), x_sample.reshape(t - t_prompt, D_MODEL)]
    mem = jnp.concatenate([mem_prompt, mem_sample], axis=0)

    def gain(i, k):
        return norms[i, k].reshape(1, D_MODEL)

    a_w_in_b, a_w_out_b = a_w_in.astype(BF16), a_w_out.astype(BF16)
    q_cols = N_HEADS * HEAD_DIM
    b_w_qkv_b = jnp.concatenate(
        [b_w_qkv[:, :, :q_cols] * (HEAD_DIM ** -0.5 * LOG2E), b_w_qkv[:, :, q_cols:]],
        axis=-1).astype(BF16)
    b_w_o_b = b_w_o.astype(BF16)
    c_w_in_b, c_w_out_b = c_w_in.astype(BF16), c_w_out.astype(BF16)
    m_w_q_b, m_w_o_b = m_w_q.astype(BF16), m_w_o.astype(BF16)
    f_w_gate_up_b, f_w_down_b = f_w_gate_up.astype(BF16), f_w_down.astype(BF16)

    kv_all = _kv_proj(mem, norms[:, 4].reshape(DEPTH, 1, D_MODEL), m_w_kv.astype(BF16))

    for i in range(DEPTH):
        kind, j = i % N_MIXERS, i // N_MIXERS
        if kind == 0:
            parts = x_parts if x is None else [x]
            z, mu, rstd = _gmlp_in(parts, gain(i, 0), a_w_in_b, j, tm=1024,
                                   tn=(4 if len(parts) == 1 else 2) * GMLP_GDIM)
            x = _gmlp_out(z, mu, rstd, a_ln_g[j].reshape(GMLP_GROUPS, 1, GMLP_GDIM),
                          a_ln_b[j].reshape(GMLP_GROUPS, 1, GMLP_GDIM),
                          a_w_s[j].astype(BF16), a_b_s[j].reshape(GMLP_GROUPS, BLOCK, 1),
                          a_w_out_b, j, parts, gain(i, 1), tm=512)
        elif kind == 1:
            qkv = _norm_matmul(x, gain(i, 0), b_w_qkv_b, j, tm=1024, tn=1024)
            o = _win_attention(qkv, b_sink[j] * LOG2E, _band_bias(rel_bias), nseq=nseq, r_blocks=4)
            x = _matmul_norm_res(o, b_w_o_b, j, x, gain(i, 1), tm=512)
        else:
            y_in = _norm_matmul(x, gain(i, 0), c_w_in_b, j, tm=1024, tn=D_RNN // 2, n_gelu=2)
            y_in = y_in.reshape(nseq, SEQ, 2 * D_RNN)
            scan_args = (c_conv_w[j], c_conv_b[j].reshape(1, D_RNN))
            hs = None
            for d in range(2):
                hs = _rglru_scan(y_in, *scan_args, _gate_weights(c_w_a[j, d], c_w_x[j, d]),
                                 (c_b_a[j, d] * 0.5).reshape(1, D_RNN),
                                 (c_b_x[j, d] * 0.5).reshape(1, D_RNN),
                                 c_lam[j, d].reshape(1, D_RNN), hs, ts=64, nseq=nseq,
                                 reverse=(d == 1))
            x = _matmul_norm_res(hs.reshape(t, D_RNN), c_w_out_b, j, x, gain(i, 1), tm=512)
        x = _mem_attention(x, gain(i, 2), gain(i, 3), m_w_q_b, kv_all, m_w_o_b, i, tm=1024)
        ffn = functools.partial(_ffn, x, gain(i, 5), gain(i, 6), f_w_gate_up_b, f_w_down_b, i)
        if i < DEPTH - 1:
            x = ffn(tm=768, tn=512)

    y_prompt = ffn(tm=1024, tn=256, row0=0, rows=t_prompt).reshape(n_prompt, SEQ, D_MODEL)
    y_sample = ffn(tm=1024, tn=256, row0=t_prompt, rows=t - t_prompt).reshape(
        nseq - n_prompt, SEQ, D_MODEL)
    return (y_prompt, y_sample)
```

```python
import functools

import numpy as np
import jax
import jax.numpy as jnp
from jax import lax
from jax.experimental import pallas as pl
from jax.experimental.pallas import tpu as pltpu

D_MODEL = 2048
SEQ = 4096
DEPTH = 4
N_MIXERS = 3
BLOCK = 128
HEAD_DIM = 128
EPS = 1e-6
NEG_INF = -1e30
GMLP_HALF = 2 * D_MODEL
GMLP_GROUPS = 8
GMLP_GDIM = GMLP_HALF // GMLP_GROUPS
N_HEADS = D_MODEL // HEAD_DIM
N_KV_HEADS = 4
KV_GROUP = N_HEADS // N_KV_HEADS
WINDOW = 128
N_BUCKETS = 32
MAX_DISTANCE = 128
D_RNN = 2816
RNN_BLOCKS = 16
RNN_BDIM = D_RNN // RNN_BLOCKS
CONV_WIDTH = 4
CONV_LEFT = 2
LRU_C = 8.0
MEM_LEN = 256
MEM_HEADS = 4
MEM_WIDTH = MEM_HEADS * HEAD_DIM
D_FF = 5632

LANES = 128
SUBLANES = 8
VMEM_LIMIT = 56 * 1024 * 1024

F32 = jnp.float32
BF16 = jnp.bfloat16
LOG2E = float(np.log2(np.e))


def _params(sem):
    return pltpu.CompilerParams(dimension_semantics=sem, vmem_limit_bytes=VMEM_LIMIT)


def _rms(x, g):
    ms = jnp.mean(x * x, axis=-1, keepdims=True)
    return x * lax.rsqrt(ms + EPS) * g


def _gelu(x):
    k1 = float(-2.0 * np.sqrt(2.0 / np.pi) * LOG2E)
    return x / (1.0 + jnp.exp2(x * (k1 + (k1 * 0.044715) * (x * x))))


def _sigmoid(x):
    return 1.0 / (1.0 + jnp.exp(-x))


def _dot(a, b):
    return jnp.dot(a, b, preferred_element_type=F32)


def _norm_mm_kernel(x_ref, g_ref, w_ref, o_ref, h_scr, *, n_gelu):
    j = pl.program_id(1)

    def emit(act, first):
        if first:
            h_scr[...] = _rms(x_ref[...], g_ref[...]).astype(BF16)
        z = _dot(h_scr[...], w_ref[...])
        o_ref[...] = (_gelu(z) if act else z).astype(o_ref.dtype)

    pl.when(j == 0)(functools.partial(emit, n_gelu > 0, True))
    if n_gelu > 1:
        pl.when((j > 0) & (j < n_gelu))(functools.partial(emit, True, False))
    pl.when(j >= max(n_gelu, 1))(functools.partial(emit, False, False))


def _norm_matmul(x, g, w, layer, *, tm, tn, n_gelu=0):
    t, d = x.shape
    n = w.shape[2]
    return pl.pallas_call(
        functools.partial(_norm_mm_kernel, n_gelu=n_gelu),
        grid=(t // tm, n // tn),
        in_specs=[
            pl.BlockSpec((tm, d), lambda i, j: (i, 0)),
            pl.BlockSpec((1, d), lambda i, j: (0, 0)),
            pl.BlockSpec((None, d, tn), lambda i, j: (layer, 0, j)),
        ],
        out_specs=pl.BlockSpec((tm, tn), lambda i, j: (i, j)),
        out_shape=jax.ShapeDtypeStruct((t, n), BF16),
        scratch_shapes=[pltpu.VMEM((tm, d), BF16)],
        compiler_params=_params(("parallel", "arbitrary")),
        name="norm_matmul",
    )(x, g, w)


def _part_specs(parts, tm, tile_of):
    specs, ranges, lo = [], [], 0
    for p in parts:
        n = p.shape[0] // tm
        specs.append(pl.BlockSpec(
            (tm, p.shape[1]),
            lambda *idx, lo=lo, n=n: (jnp.clip(tile_of(*idx) - lo, 0, n - 1), 0)))
        ranges.append((lo, lo + n))
        lo += n
    return specs, tuple(ranges)


def _for_owner(i, ranges, refs, cond, fn):
    for (lo, hi), ref in zip(ranges, refs):
        pl.when(cond & (i >= lo) & (i < hi))(functools.partial(fn, ref))


MXU_COLS = 256


def _gmlp_in_kernel(*refs, ranges, n_col, tn):
    n_parts = len(ranges)
    x_refs = refs[:n_parts]
    g_ref, w_ref, z_ref, mu_ref, rs_ref, h_scr, s1_scr, s2_scr = refs[n_parts:]
    i, j = pl.program_id(0), pl.program_id(1)
    tm = h_scr.shape[0]

    def step(is_v, x_ref=None):
        if x_ref is not None:
            h_scr[...] = _rms(x_ref[...], g_ref[...]).astype(BF16)
            s1_scr[...] = jnp.zeros_like(s1_scr)
            s2_scr[...] = jnp.zeros_like(s2_scr)
        p1 = jnp.zeros((tm, LANES), F32)
        p2 = jnp.zeros((tm, LANES), F32)
        per_tile = GMLP_GDIM // MXU_COLS
        for c in range(tn // MXU_COLS):
            cols = slice(c * MXU_COLS, (c + 1) * MXU_COLS)
            zc = _gelu(_dot(h_scr[...], w_ref[:, cols]))
            lo = (c % per_tile) * MXU_COLS
            z_ref[c // per_tile, :, lo:lo + MXU_COLS] = zc.astype(z_ref.dtype)
            if is_v:
                for k in range(MXU_COLS // LANES):
                    zk = zc[:, k * LANES:(k + 1) * LANES]
                    p1 = p1 + zk
                    p2 = p2 + zk * zk
        if is_v:
            s1_scr[...] += p1
            s2_scr[...] += p2

    _for_owner(i, ranges, x_refs, j == 0, functools.partial(step, False))
    pl.when((j > 0) & (j < n_col // 2))(functools.partial(step, False))
    pl.when(j >= n_col // 2)(functools.partial(step, True))

    @pl.when(j == n_col - 1)
    def _():
        mu = jnp.sum(s1_scr[...], axis=-1, keepdims=True) * (1.0 / GMLP_HALF)
        msq = jnp.sum(s2_scr[...], axis=-1, keepdims=True) * (1.0 / GMLP_HALF)
        mu_ref[...] = mu
        rs_ref[...] = lax.rsqrt(jnp.maximum(msq - mu * mu, 0.0) + EPS)


def _gmlp_in(x_parts, g, w, layer, *, tm, tn):
    t = sum(p.shape[0] for p in x_parts)
    d = x_parts[0].shape[1]
    n_col = 2 * GMLP_HALF // tn
    sub = tn // GMLP_GDIM
    x_specs, ranges = _part_specs(x_parts, tm, lambda i, j: i)
    return pl.pallas_call(
        functools.partial(_gmlp_in_kernel, ranges=ranges, n_col=n_col, tn=tn),
        grid=(t // tm, n_col),
        in_specs=x_specs + [
            pl.BlockSpec((1, d), lambda i, j: (0, 0)),
            pl.BlockSpec((None, d, tn), lambda i, j: (layer, 0, j)),
        ],
        out_specs=[
            pl.BlockSpec((sub, tm, GMLP_GDIM), lambda i, j: (j, i, 0)),
            pl.BlockSpec((tm, 1), lambda i, j: (i, 0)),
            pl.BlockSpec((tm, 1), lambda i, j: (i, 0)),
        ],
        out_shape=[
            jax.ShapeDtypeStruct((2 * GMLP_GROUPS, t, GMLP_GDIM), BF16),
            jax.ShapeDtypeStruct((t, 1), F32),
            jax.ShapeDtypeStruct((t, 1), F32),
        ],
        scratch_shapes=[
            pltpu.VMEM((tm, d), BF16),
            pltpu.VMEM((tm, LANES), F32),
            pltpu.VMEM((tm, LANES), F32),
        ],
        compiler_params=_params(("parallel", "arbitrary")),
        name="gmlp_in",
    )(*x_parts, g, w)


def _mm_norm_res_kernel(a_ref, w_ref, x_ref, g_ref, o_ref):
    y = _dot(a_ref[...], w_ref[...])
    o_ref[...] = x_ref[...] + _rms(y, g_ref[...])


def _matmul_norm_res(a, w, layer, x, g, *, tm):
    t, k = a.shape
    d = w.shape[2]
    return pl.pallas_call(
        _mm_norm_res_kernel,
        grid=(t // tm,),
        in_specs=[
            pl.BlockSpec((tm, k), lambda i: (i, 0)),
            pl.BlockSpec((None, k, d), lambda i: (layer, 0, 0), pipeline_mode=pl.Buffered(1)),
            pl.BlockSpec((tm, d), lambda i: (i, 0)),
            pl.BlockSpec((1, d), lambda i: (0, 0)),
        ],
        out_specs=pl.BlockSpec((tm, d), lambda i: (i, 0)),
        out_shape=jax.ShapeDtypeStruct((t, d), F32),
        compiler_params=_params(("parallel",)),
        name="matmul_norm_res",
    )(a, w, x, g)


PAIRS = GMLP_GROUPS // 2


def _gmlp_out_kernel(*refs, ranges, tm):
    n_parts = len(ranges)
    (u1_ref, v1_ref, mu1_ref, rs1_ref, lng1_ref, lnb1_ref, ws1_ref, bs1_ref,
     u2_ref, v2_ref, mu2_ref, rs2_ref, lng2_ref, lnb2_ref, ws2_ref, bs2_ref,
     wo_ref, g_ref) = refs[:18]
    x_refs = refs[18:18 + n_parts]
    o_ref, a0_scr, a1_scr = refs[18 + n_parts:]
    s = pl.program_id(0)
    q = jnp.maximum(s - 1, 0)
    tile, pair = q // PAIRS, q % PAIRS

    @pl.when(s == 0)
    def _():
        a0_scr[...] = jnp.zeros_like(a0_scr)
        a1_scr[...] = jnp.zeros_like(a1_scr)

    def prep(u_ref, v_ref, mu_ref, rs_ref, lng_ref, lnb_ref, ws_ref, bs_ref):
        vn = ((v_ref[0].astype(F32) - mu_ref[...]) * rs_ref[...] * lng_ref[0]
              + lnb_ref[0]).astype(BF16)
        ws = ws_ref[0]
        bs = bs_ref[0]
        parts = []
        for n in range(tm // BLOCK):
            rows = slice(n * BLOCK, (n + 1) * BLOCK)
            mixed = _dot(ws, vn[rows, :]) + bs
            parts.append((u_ref[0, rows, :].astype(F32) * mixed).astype(BF16))
        return jnp.concatenate(parts, axis=0)

    def step(first, x_ref=None):
        row0 = pl.multiple_of(pair * (2 * GMLP_GDIM), 2 * GMLP_GDIM)
        proj = _dot(a0_scr[...], wo_ref[pl.ds(row0, GMLP_GDIM), :])
        if first:
            o_ref[...] = proj
        else:
            o_ref[...] += proj
        a1_scr[...] = prep(u1_ref, v1_ref, mu1_ref, rs1_ref, lng1_ref, lnb1_ref, ws1_ref, bs1_ref)
        o_ref[...] += _dot(a1_scr[...], wo_ref[pl.ds(row0 + GMLP_GDIM, GMLP_GDIM), :])
        a0_scr[...] = prep(u2_ref, v2_ref, mu2_ref, rs2_ref, lng2_ref, lnb2_ref, ws2_ref, bs2_ref)
        if x_ref is not None:
            o_ref[...] = x_ref[...] + _rms(o_ref[...], g_ref[...])

    pl.when(pair == 0)(functools.partial(step, True))
    pl.when((pair > 0) & (pair < PAIRS - 1))(functools.partial(step, False))
    _for_owner(tile, ranges, x_refs, pair == PAIRS - 1, functools.partial(step, False))


def _gmlp_out(z, mu, rstd, ln_g, ln_b, w_s, b_s, w_out, layer, x_parts, g, *, tm):
    t = sum(p.shape[0] for p in x_parts)
    d = x_parts[0].shape[1]
    n_tiles = t // tm
    last = GMLP_GROUPS * n_tiles - 1

    def proj_tile(s):
        return jnp.maximum(s - 1, 0) // PAIRS

    def prep_specs(pair_of):
        def at(f):
            return lambda s: f(*pair_of(s))
        return [
            pl.BlockSpec((1, tm, GMLP_GDIM), at(lambda i, c: (c, i, 0))),
            pl.BlockSpec((1, tm, GMLP_GDIM), at(lambda i, c: (GMLP_GROUPS + c, i, 0))),
            pl.BlockSpec((tm, 1), at(lambda i, c: (i, 0))),
            pl.BlockSpec((tm, 1), at(lambda i, c: (i, 0))),
            pl.BlockSpec((1, 1, GMLP_GDIM), at(lambda i, c: (c, 0, 0))),
            pl.BlockSpec((1, 1, GMLP_GDIM), at(lambda i, c: (c, 0, 0))),
            pl.BlockSpec((1, BLOCK, BLOCK), at(lambda i, c: (c, 0, 0))),
            pl.BlockSpec((1, BLOCK, 1), at(lambda i, c: (c, 0, 0))),
        ]

    def split(gidx):
        gidx = jnp.clip(gidx, 0, last)
        return gidx // GMLP_GROUPS, gidx % GMLP_GROUPS

    x_specs, ranges = _part_specs(x_parts, tm, proj_tile)
    prep_args = (z, z, mu, rstd, ln_g, ln_b, w_s, b_s)
    return pl.pallas_call(
        functools.partial(_gmlp_out_kernel, ranges=ranges, tm=tm),
        grid=(PAIRS * n_tiles + 1,),
        in_specs=prep_specs(lambda s: split(2 * s - 1)) + prep_specs(lambda s: split(2 * s)) + [
            pl.BlockSpec((None, GMLP_HALF, d), lambda s: (layer, 0, 0),
                         pipeline_mode=pl.Buffered(1)),
            pl.BlockSpec((1, d), lambda s: (0, 0)),
        ] + x_specs,
        out_specs=pl.BlockSpec((tm, d), lambda s: (proj_tile(s), 0)),
        out_shape=jax.ShapeDtypeStruct((t, d), F32),
        scratch_shapes=[
            pltpu.VMEM((tm, GMLP_GDIM), BF16),
            pltpu.VMEM((tm, GMLP_GDIM), BF16),
        ],
        compiler_params=_params(("arbitrary",)),
        name="gmlp_out",
    )(*prep_args, *prep_args, w_out, g, *x_parts)


def _win_attn_kernel(sink_ref, q_ref, kp_ref, kc_ref, kn_ref, vp_ref, vc_ref, vn_ref, bias_ref,
                     o_ref, *, r_blocks, n_steps):
    i = pl.program_id(1)
    kall = jnp.concatenate([kp_ref[0], kc_ref[0], kn_ref[0]], axis=0)
    vall = jnp.concatenate([vp_ref[0], vc_ref[0], vn_ref[0]], axis=0)
    col = lax.broadcasted_iota(jnp.int32, (1, 3 * BLOCK), 1)
    sk = jnp.concatenate(
        [jnp.full((BLOCK, 1), sink_ref[h], F32) for h in range(N_HEADS)], axis=0)
    hrows = KV_GROUP * BLOCK
    for r in range(r_blocks):
        rows = slice(r * BLOCK, (r + 1) * BLOCK)
        oob = None
        if r == 0:
            oob = (col < BLOCK) & (i == 0)
        if r == r_blocks - 1:
            hi = (col >= 2 * BLOCK) & (i == n_steps - 1)
            oob = hi if oob is None else (oob | hi)
        parts = []
        for kh in range(N_KV_HEADS):
            kb = kall[r * BLOCK:(r + 3) * BLOCK, kh * HEAD_DIM:(kh + 1) * HEAD_DIM]
            qs = jnp.concatenate(
                [q_ref[0, rows, h * HEAD_DIM:(h + 1) * HEAD_DIM]
                 for h in range(kh * KV_GROUP, (kh + 1) * KV_GROUP)], axis=0)
            parts.append(
                lax.dot_general(qs, kb, (((1,), (1,)), ((), ())), preferred_element_type=F32))
        s = jnp.concatenate(parts, axis=0) + bias_ref[...]
        if oob is not None:
            s = jnp.where(oob, NEG_INF, s)
        m = jnp.maximum(jnp.max(s, axis=-1, keepdims=True), sk)
        p = jnp.exp2(s - m)
        inv = 1.0 / (jnp.sum(p, axis=-1, keepdims=True) + jnp.exp2(sk - m))
        pb = p.astype(BF16)
        for kh in range(N_KV_HEADS):
            vb = vall[r * BLOCK:(r + 3) * BLOCK, kh * HEAD_DIM:(kh + 1) * HEAD_DIM]
            hsl = slice(kh * hrows, (kh + 1) * hrows)
            o = _dot(pb[hsl], vb) * inv[hsl]
            for gq in range(KV_GROUP):
                h = kh * KV_GROUP + gq
                o_ref[0, rows, h * HEAD_DIM:(h + 1) * HEAD_DIM] = (
                    o[gq * BLOCK:(gq + 1) * BLOCK].astype(o_ref.dtype))


def _win_attention(qkv, sink, bias, *, nseq, r_blocks):
    tq = r_blocks * BLOCK
    n_steps = SEQ // tq
    nb = SEQ // BLOCK
    qkv3 = qkv.reshape(nseq, SEQ, qkv.shape[-1])
    kcol = N_HEADS * HEAD_DIM // (N_KV_HEADS * HEAD_DIM)
    vcol = kcol + 1
    kvw = N_KV_HEADS * HEAD_DIM

    def prev_map(c):
        return lambda b, i: (b, jnp.maximum(i * r_blocks - 1, 0), c)

    def cur_map(c):
        return lambda b, i: (b, i, c)

    def next_map(c):
        return lambda b, i: (b, jnp.minimum((i + 1) * r_blocks, nb - 1), c)

    out = pl.pallas_call(
        functools.partial(_win_attn_kernel, r_blocks=r_blocks, n_steps=n_steps),
        grid=(nseq, n_steps),
        in_specs=[
            pl.BlockSpec(memory_space=pltpu.SMEM),
            pl.BlockSpec((1, tq, N_HEADS * HEAD_DIM), lambda b, i: (b, i, 0)),
            pl.BlockSpec((1, BLOCK, kvw), prev_map(kcol)),
            pl.BlockSpec((1, tq, kvw), cur_map(kcol)),
            pl.BlockSpec((1, BLOCK, kvw), next_map(kcol)),
            pl.BlockSpec((1, BLOCK, kvw), prev_map(vcol)),
            pl.BlockSpec((1, tq, kvw), cur_map(vcol)),
            pl.BlockSpec((1, BLOCK, kvw), next_map(vcol)),
            pl.BlockSpec((N_HEADS * BLOCK, 3 * BLOCK), lambda b, i: (0, 0)),
        ],
        out_specs=pl.BlockSpec((1, tq, N_HEADS * HEAD_DIM), lambda b, i: (b, i, 0)),
        out_shape=jax.ShapeDtypeStruct((nseq, SEQ, N_HEADS * HEAD_DIM), BF16),
        compiler_params=_params(("parallel", "parallel")),
        name="win_attention",
    )(sink, qkv3, qkv3, qkv3, qkv3, qkv3, qkv3, qkv3, bias)
    return out.reshape(nseq * SEQ, N_HEADS * HEAD_DIM)


def _t5_bucket(rel):
    half = N_BUCKETS // 2
    exact = half // 2
    n = np.abs(rel)
    large = exact + (np.log(np.maximum(n, 1) / exact) / np.log(MAX_DISTANCE / exact)
                     * (half - exact)).astype(np.int32)
    large = np.minimum(large, half - 1)
    return (np.where(rel > 0, half, 0) + np.where(n < exact, n, large)).astype(np.int32)


def _band_bias(rel_table):
    rel = np.arange(3 * BLOCK)[None, :] - BLOCK - np.arange(BLOCK)[:, None]
    onehot = (_t5_bucket(rel).reshape(1, -1) == np.arange(N_BUCKETS)[:, None]).astype(np.float32)
    bias = jnp.einsum("bh,bn->hn", rel_table.astype(F32) * LOG2E, onehot,
                      precision=lax.Precision.HIGHEST)
    bias = jnp.where((np.abs(rel) <= WINDOW).reshape(1, -1), bias, NEG_INF)
    return bias.reshape(N_HEADS * BLOCK, 3 * BLOCK)


def _kv_kernel(mem_ref, g_ref, w_ref, o_ref):
    h = _rms(mem_ref[0], g_ref[0]).astype(BF16)
    o_ref[0, 0] = _dot(h, w_ref[0]).astype(o_ref.dtype)


def _kv_proj(mem, g, w_kv):
    nseq = mem.shape[0]
    return pl.pallas_call(
        _kv_kernel,
        grid=(DEPTH, nseq),
        in_specs=[
            pl.BlockSpec((1, MEM_LEN, D_MODEL), lambda l, b: (b, 0, 0)),
            pl.BlockSpec((1, 1, D_MODEL), lambda l, b: (l, 0, 0)),
            pl.BlockSpec((1, D_MODEL, 2 * MEM_WIDTH), lambda l, b: (l, 0, 0)),
        ],
        out_specs=pl.BlockSpec((1, 1, MEM_LEN, 2 * MEM_WIDTH), lambda l, b: (l, b, 0, 0)),
        out_shape=jax.ShapeDtypeStruct((DEPTH, nseq, MEM_LEN, 2 * MEM_WIDTH), BF16),
        compiler_params=_params(("parallel", "parallel")),
        name="kv_proj",
    )(mem, g, w_kv)


def _mem_attn_kernel(x_ref, gpre_ref, gpost_ref, wq_ref, kv_ref, wo_ref, o_ref):
    x = x_ref[...]
    h = _rms(x, gpre_ref[...]).astype(BF16)
    q = (_dot(h, wq_ref[...]) * (HEAD_DIM ** -0.5)).astype(BF16)
    outs = []
    for hd in range(MEM_HEADS):
        sl = slice(hd * HEAD_DIM, (hd + 1) * HEAD_DIM)
        k = kv_ref[0, :, sl]
        v = kv_ref[0, :, MEM_WIDTH + hd * HEAD_DIM:MEM_WIDTH + (hd + 1) * HEAD_DIM]
        s = lax.dot_general(q[:, sl], k, (((1,), (1,)), ((), ())), preferred_element_type=F32)
        m = jnp.max(s, axis=-1, keepdims=True)
        p = jnp.exp(s - m)
        denom = jnp.sum(p, axis=-1, keepdims=True)
        outs.append((_dot(p.astype(BF16), v) / denom).astype(BF16))
    y = _dot(jnp.concatenate(outs, axis=-1), wo_ref[...])
    o_ref[...] = x + _rms(y, gpost_ref[...])


def _mem_attention(x, gpre, gpost, w_q, kv, w_o, layer, *, tm):
    t, d = x.shape
    per_seq = SEQ // tm
    return pl.pallas_call(
        _mem_attn_kernel,
        grid=(t // tm,),
        in_specs=[
            pl.BlockSpec((tm, d), lambda i: (i, 0)),
            pl.BlockSpec((1, d), lambda i: (0, 0)),
            pl.BlockSpec((1, d), lambda i: (0, 0)),
            pl.BlockSpec((None, d, MEM_WIDTH), lambda i: (layer, 0, 0)),
            pl.BlockSpec((None, 1, MEM_LEN, 2 * MEM_WIDTH), lambda i: (layer, i // per_seq, 0, 0)),
            pl.BlockSpec((None, MEM_WIDTH, d), lambda i: (layer, 0, 0)),
        ],
        out_specs=pl.BlockSpec((tm, d), lambda i: (i, 0)),
        out_shape=jax.ShapeDtypeStruct((t, d), F32),
        compiler_params=_params(("parallel",)),
        name="mem_attention",
    )(x, gpre, gpost, w_q, kv, w_o)


def _ffn_kernel(x_ref, gpre_ref, gpost_ref, wg_ref, wu_ref, wd_ref, o_ref, h_scr, *, n_chunks):
    j = pl.program_id(1)

    def chunk(first, last):
        if first:
            h_scr[...] = _rms(x_ref[...], gpre_ref[...]).astype(BF16)
        h = h_scr[...]
        gate = _dot(h, wg_ref[...])
        up = _dot(h, wu_ref[...])
        a = (gate * _sigmoid(gate) * up).astype(BF16)
        if first:
            o_ref[...] = _dot(a, wd_ref[...])
        else:
            o_ref[...] += _dot(a, wd_ref[...])
        if last:
            o_ref[...] = x_ref[...] + _rms(o_ref[...], gpost_ref[...])

    pl.when(j == 0)(functools.partial(chunk, True, False))
    pl.when((j > 0) & (j < n_chunks - 1))(functools.partial(chunk, False, False))
    pl.when(j == n_chunks - 1)(functools.partial(chunk, False, True))


def _ffn(x, gpre, gpost, w_gate_up, w_down, layer, *, tm, tn, row0=0, rows=None):
    d = x.shape[1]
    t = x.shape[0] if rows is None else rows
    tile0 = row0 // tm
    n_chunks = D_FF // tn
    x_bytes = tm * d * 4
    fixed = 2 * x_bytes + tm * d * 2 + 2 * 3 * d * tn * 2
    temps = 2 * tm * tn * 4 + tm * tn * 2 + tm * d * 4
    x_mode = None if fixed + 2 * x_bytes + temps <= VMEM_LIMIT else pl.Buffered(1)
    return pl.pallas_call(
        functools.partial(_ffn_kernel, n_chunks=n_chunks),
        grid=(t // tm, n_chunks),
        in_specs=[
            pl.BlockSpec((tm, d), lambda i, j: (i + tile0, 0), pipeline_mode=x_mode),
            pl.BlockSpec((1, d), lambda i, j: (0, 0)),
            pl.BlockSpec((1, d), lambda i, j: (0, 0)),
            pl.BlockSpec((None, d, tn), lambda i, j: (layer, 0, j)),
            pl.BlockSpec((None, d, tn), lambda i, j: (layer, 0, j + n_chunks)),
            pl.BlockSpec((None, tn, d), lambda i, j: (layer, j, 0)),
        ],
        out_specs=pl.BlockSpec((tm, d), lambda i, j: (i, 0)),
        out_shape=jax.ShapeDtypeStruct((t, d), F32),
        scratch_shapes=[pltpu.VMEM((tm, d), BF16)],
        compiler_params=_params(("parallel", "arbitrary")),
        name="ffn",
    )(x, gpre, gpost, w_gate_up, w_gate_up, w_down)


N_CT = D_RNN // LANES
HALO = 2 * SUBLANES


def _gate_windows():
    lo_hi = []
    for j in range(N_CT):
        h0 = (LANES * j) // RNN_BDIM
        h1 = (LANES * j + LANES - 1) // RNN_BDIM
        lo = (RNN_BDIM * h0) // LANES * LANES
        hi = -(-(RNN_BDIM * (h1 + 1)) // LANES) * LANES
        lo_hi.append((lo, hi))
    width = max(hi - lo for lo, hi in lo_hi)
    starts = tuple(min(lo, D_RNN - width) for lo, _ in lo_hi)
    return starts, width


GATE_STARTS, GATE_WIDTH = _gate_windows()


def _gate_weights(w_a, w_x):
    def tile(w, j):
        st = GATE_STARTS[j]
        pieces = []
        for h in range((LANES * j) // RNN_BDIM, (LANES * j + LANES - 1) // RNN_BDIM + 1):
            c0 = max(LANES * j, RNN_BDIM * h) - RNN_BDIM * h
            c1 = min(LANES * (j + 1), RNN_BDIM * (h + 1)) - RNN_BDIM * h
            top = RNN_BDIM * h - st
            pieces.append(jnp.pad(w[h, :, c0:c1], ((top, GATE_WIDTH - RNN_BDIM - top), (0, 0))))
        return jnp.concatenate(pieces, axis=1)

    wa, wx = (w_a * 0.5).astype(BF16), (w_x * 0.5).astype(BF16)
    return jnp.stack([jnp.concatenate([tile(wa, j), tile(wx, j)], axis=1) for j in range(N_CT)])


def _scan_kernel(*refs, ts, n_tiles, nseq, reverse):
    if reverse:
        (xp_ref, xc_ref, xn_ref, cw_ref, cb_ref, wp_ref, ba_ref, bx_ref, lam_ref,
         hsf_ref, gate_ref, o_ref, conv_scr, convb_scr, a_scr, u_scr, h_scr) = refs
    else:
        (xp_ref, xc_ref, xn_ref, cw_ref, cb_ref, wp_ref, ba_ref, bx_ref, lam_ref,
         o_ref, conv_scr, convb_scr, a_scr, u_scr, h_scr) = refs
    i = pl.program_id(0)
    tile = (n_tiles - 1 - i) if reverse else i

    @pl.when(i == 0)
    def _():
        h_scr[...] = jnp.zeros_like(h_scr)
        a_scr[...] = jnp.zeros_like(a_scr)
        u_scr[...] = jnp.zeros_like(u_scr)

    cw = cw_ref[...]
    cb = cb_ref[...]
    shifts = [k - CONV_LEFT for k in range(CONV_WIDTH) if k != CONV_LEFT]
    rr = lax.broadcasted_iota(jnp.int32, (ts, ts + 2 * HALO), 0)
    cc = lax.broadcasted_iota(jnp.int32, (ts, ts + 2 * HALO), 1)
    sel = jnp.concatenate(
        [jnp.where(cc == rr + HALO + s, 1.0, 0.0).astype(BF16) for s in shifts], axis=0)
    def conv_columns(cols):
        for b in range(nseq):
            prev = xp_ref[b, :, cols]
            nxt = xn_ref[b, :, cols]
            prev = jnp.where(tile == 0, jnp.zeros_like(prev), prev)
            nxt = jnp.where(tile == n_tiles - 1, jnp.zeros_like(nxt), nxt)
            cur = xc_ref[b, :, cols]
            shifted = _dot(sel, jnp.concatenate([prev, cur, nxt], axis=0))
            acc = cb[:, cols] + cur.astype(F32) * cw[CONV_LEFT:CONV_LEFT + 1, cols]
            for n, s in enumerate(shifts):
                k = s + CONV_LEFT
                acc = acc + shifted[n * ts:(n + 1) * ts] * cw[k:k + 1, cols]
            conv_scr[b * ts:(b + 1) * ts, cols] = acc
            convb_scr[b * ts:(b + 1) * ts, cols] = acc.astype(BF16)

    def gate_tile(j):
        st = GATE_STARTS[j]
        sl = slice(LANES * j, LANES * (j + 1))
        pre = _dot(convb_scr[:, st:st + GATE_WIDTH], wp_ref[j])
        tr = jnp.tanh(pre[:, :LANES] + ba_ref[:, sl])
        ti = jnp.tanh(pre[:, LANES:] + bx_ref[:, sl])
        z = -lam_ref[:, sl]
        softplus = jnp.maximum(z, 0.0) + jnp.log(1.0 + jnp.exp(-jnp.abs(z)))
        half = (-0.5 * LRU_C * LOG2E) * softplus
        a = jnp.exp2(half + half * tr)
        y = 1.0 - a * a
        root = jnp.where(y > 0.0, y * lax.rsqrt(y), 0.0)
        hx = 0.5 * conv_scr[:, sl]
        u = root * (hx + hx * ti)
        for b in range(nseq):
            a_scr[j, pl.ds(b, ts, stride=SUBLANES), :] = a[b * ts:(b + 1) * ts]
            u_scr[j, pl.ds(b, ts, stride=SUBLANES), :] = u[b * ts:(b + 1) * ts]

    conv_columns(slice(None))
    for j in range(N_CT):
        gate_tile(j)

    def step(k, h):
        t = (ts - 1 - k) if reverse else k
        row = pl.multiple_of(t * SUBLANES, SUBLANES)
        h = a_scr[:, pl.ds(row, SUBLANES), :] * h + u_scr[:, pl.ds(row, SUBLANES), :]
        u_scr[:, pl.ds(row, SUBLANES), :] = h
        return h

    h_scr[...] = lax.fori_loop(0, ts, step, h_scr[...], unroll=8)

    for b in range(nseq):
        for j in range(N_CT):
            sl = slice(LANES * j, LANES * (j + 1))
            y = u_scr[j, pl.ds(b, ts, stride=SUBLANES), :]
            if reverse:
                y = gate_ref[b, :, sl].astype(F32) * (y + hsf_ref[b, :, sl].astype(F32))
            o_ref[b, :, sl] = y.astype(o_ref.dtype)


def _rglru_scan(y_in, conv_w, conv_b, wp, b_a, b_x, lam, hs_fwd, *, ts, nseq, reverse):
    n_tiles = SEQ // ts
    per_tile = ts // HALO

    def tmap(i):
        return (n_tiles - 1 - i) if reverse else i

    in_specs = [
        pl.BlockSpec((nseq, HALO, D_RNN), lambda i: (0, jnp.maximum(tmap(i) * per_tile - 1, 0), 1)),
        pl.BlockSpec((nseq, ts, D_RNN), lambda i: (0, tmap(i), 1)),
        pl.BlockSpec((nseq, HALO, D_RNN),
                     lambda i: (0, jnp.minimum((tmap(i) + 1) * per_tile, SEQ // HALO - 1), 1)),
        pl.BlockSpec((CONV_WIDTH, D_RNN), lambda i: (0, 0)),
        pl.BlockSpec((1, D_RNN), lambda i: (0, 0)),
        pl.BlockSpec((N_CT, GATE_WIDTH, 2 * LANES), lambda i: (0, 0, 0)),
        pl.BlockSpec((1, D_RNN), lambda i: (0, 0)),
        pl.BlockSpec((1, D_RNN), lambda i: (0, 0)),
        pl.BlockSpec((1, D_RNN), lambda i: (0, 0)),
    ]
    args = [y_in, y_in, y_in, conv_w, conv_b, wp, b_a, b_x, lam]
    if reverse:
        in_specs += [
            pl.BlockSpec((nseq, ts, D_RNN), lambda i: (0, tmap(i), 0)),
            pl.BlockSpec((nseq, ts, D_RNN), lambda i: (0, tmap(i), 0)),
        ]
        args += [hs_fwd, y_in]
    return pl.pallas_call(
        functools.partial(_scan_kernel, ts=ts, n_tiles=n_tiles, nseq=nseq, reverse=reverse),
        grid=(n_tiles,),
        in_specs=in_specs,
        out_specs=pl.BlockSpec((nseq, ts, D_RNN), lambda i: (0, tmap(i), 0)),
        out_shape=jax.ShapeDtypeStruct((nseq, SEQ, D_RNN), BF16),
        scratch_shapes=[
            pltpu.VMEM((nseq * ts, D_RNN), F32),
            pltpu.VMEM((nseq * ts, D_RNN), BF16),
            pltpu.VMEM((N_CT, ts * SUBLANES, LANES), F32),
            pltpu.VMEM((N_CT, ts * SUBLANES, LANES), F32),
            pltpu.VMEM((N_CT, SUBLANES, LANES), F32),
        ],
        compiler_params=_params(("arbitrary",)),
        name="rglru_scan_bwd" if reverse else "rglru_scan_fwd",
    )(*args)


def kernel(x_prompt, x_sample, mem_prompt, mem_sample, norms, rel_bias, a_w_in, a_ln_g, a_ln_b, a_w_s, a_b_s, a_w_out, b_w_qkv, b_sink, b_w_o, c_w_in, c_conv_w, c_conv_b, c_w_a, c_b_a, c_w_x, c_b_x, c_lam, c_w_out, m_w_q, m_w_kv, m_w_o, f_w_gate_up, f_w_down):
    n_prompt = x_prompt.shape[0]
    nseq = n_prompt + x_sample.shape[0]
    assert nseq <= SUBLANES
    t = nseq * SEQ
    t_prompt = n_prompt * SEQ
    x = None
    x_parts = [x_prompt.reshape(t_prompt, D_MODEL), x_sample.reshape(t - t_prompt, D_MODEL)]
    mem = jnp.concatenate([mem_prompt, mem_sample], axis=0)

    def gain(i, k):
        return norms[i, k].reshape(1, D_MODEL)

    a_w_in_b, a_w_out_b = a_w_in.astype(BF16), a_w_out.astype(BF16)
    q_cols = N_HEADS * HEAD_DIM
    b_w_qkv_b = jnp.concatenate(
        [b_w_qkv[:, :, :q_cols] * (HEAD_DIM ** -0.5 * LOG2E), b_w_qkv[:, :, q_cols:]],
        axis=-1).astype(BF16)
    b_w_o_b = b_w_o.astype(BF16)
    c_w_in_b, c_w_out_b = c_w_in.astype(BF16), c_w_out.astype(BF16)
    m_w_q_b, m_w_o_b = m_w_q.astype(BF16), m_w_o.astype(BF16)
    f_w_gate_up_b, f_w_down_b = f_w_gate_up.astype(BF16), f_w_down.astype(BF16)

    kv_all = _kv_proj(mem, norms[:, 4].reshape(DEPTH, 1, D_MODEL), m_w_kv.astype(BF16))

    for i in range(DEPTH):
        kind, j = i % N_MIXERS, i // N_MIXERS
        if kind == 0:
            parts = x_parts if x is None else [x]
            z, mu, rstd = _gmlp_in(parts, gain(i, 0), a_w_in_b, j, tm=1024,
                                   tn=(4 if len(parts) == 1 else 2) * GMLP_GDIM)
            x = _gmlp_out(z, mu, rstd, a_ln_g[j].reshape(GMLP_GROUPS, 1, GMLP_GDIM),
                          a_ln_b[j].reshape(GMLP_GROUPS, 1, GMLP_GDIM),
                          a_w_s[j].astype(BF16), a_b_s[j].reshape(GMLP_GROUPS, BLOCK, 1),
                          a_w_out_b, j, parts, gain(i, 1), tm=512)
        elif kind == 1:
            qkv = _norm_matmul(x, gain(i, 0), b_w_qkv_b, j, tm=1024, tn=1024)
            o = _win_attention(qkv, b_sink[j] * LOG2E, _band_bias(rel_bias), nseq=nseq, r_blocks=4)
            x = _matmul_norm_res(o, b_w_o_b, j, x, gain(i, 1), tm=512)
        else:
            y_in = _norm_matmul(x, gain(i, 0), c_w_in_b, j, tm=1024, tn=D_RNN // 2, n_gelu=2)
            y_in = y_in.reshape(nseq, SEQ, 2 * D_RNN)
            scan_args = (c_conv_w[j], c_conv_b[j].reshape(1, D_RNN))
            hs = None
            for d in range(2):
                hs = _rglru_scan(y_in, *scan_args, _gate_weights(c_w_a[j, d], c_w_x[j, d]),
                                 (c_b_a[j, d] * 0.5).reshape(1, D_RNN),
                                 (c_b_x[j, d] * 0.5).reshape(1, D_RNN),
                                 c_lam[j, d].reshape(1, D_RNN), hs, ts=64, nseq=nseq,
                                 reverse=(d == 1))
            x = _matmul_norm_res(hs.reshape(t, D_RNN), c_w_out_b, j, x, gain(i, 1), tm=512)
        x = _mem_attention(x, gain(i, 2), gain(i, 3), m_w_q_b, kv_all, m_w_o_b, i, tm=1024)
        ffn = functools.partial(_ffn, x, gain(i, 5), gain(i, 6), f_w_gate_up_b, f_w_down_b, i)
        if i < DEPTH - 1:
            x = ffn(tm=768, tn=512)

    y_prompt = ffn(tm=1024, tn=256, row0=0, rows=t_prompt).reshape(n_prompt, SEQ, D_MODEL)
    y_sample = ffn(tm=1024, tn=256, row0=t_prompt, rows=t - t_prompt).reshape(
        nseq - n_prompt, SEQ, D_MODEL)
    return (y_prompt, y_sample)
```

```python
import functools

import numpy as np
import jax
import jax.numpy as jnp
from jax import lax
from jax.experimental import pallas as pl
from jax.experimental.pallas import tpu as pltpu

D_MODEL = 2048
SEQ = 4096
DEPTH = 4
N_MIXERS = 3
BLOCK = 128
HEAD_DIM = 128
EPS = 1e-6
NEG_INF = -1e30
GMLP_HALF = 2 * D_MODEL
GMLP_GROUPS = 8
GMLP_GDIM = GMLP_HALF // GMLP_GROUPS
N_HEADS = D_MODEL // HEAD_DIM
N_KV_HEADS = 4
KV_GROUP = N_HEADS // N_KV_HEADS
WINDOW = 128
N_BUCKETS = 32
MAX_DISTANCE = 128
D_RNN = 2816
RNN_BLOCKS = 16
RNN_BDIM = D_RNN // RNN_BLOCKS
CONV_WIDTH = 4
CONV_LEFT = 2
LRU_C = 8.0
MEM_LEN = 256
MEM_HEADS = 4
MEM_WIDTH = MEM_HEADS * HEAD_DIM
D_FF = 5632

LANES = 128
SUBLANES = 8
VMEM_LIMIT = 56 * 1024 * 1024

F32 = jnp.float32
BF16 = jnp.bfloat16
LOG2E = float(np.log2(np.e))


def _params(sem):
    return pltpu.CompilerParams(dimension_semantics=sem, vmem_limit_bytes=VMEM_LIMIT)


def _rms(x, g):
    ms = jnp.mean(x * x, axis=-1, keepdims=True)
    return x * lax.rsqrt(ms + EPS) * g


def _gelu(x):
    k1 = float(-2.0 * np.sqrt(2.0 / np.pi) * LOG2E)
    return x / (1.0 + jnp.exp2(x * (k1 + (k1 * 0.044715) * (x * x))))


def _sigmoid(x):
    return 1.0 / (1.0 + jnp.exp(-x))


def _dot(a, b):
    return jnp.dot(a, b, preferred_element_type=F32)


def _norm_mm_kernel(x_ref, g_ref, w_ref, o_ref, h_scr, *, n_gelu):
    j = pl.program_id(1)

    def emit(act, first):
        if first:
            h_scr[...] = _rms(x_ref[...], g_ref[...]).astype(BF16)
        z = _dot(h_scr[...], w_ref[...])
        o_ref[...] = (_gelu(z) if act else z).astype(o_ref.dtype)

    pl.when(j == 0)(functools.partial(emit, n_gelu > 0, True))
    if n_gelu > 1:
        pl.when((j > 0) & (j < n_gelu))(functools.partial(emit, True, False))
    pl.when(j >= max(n_gelu, 1))(functools.partial(emit, False, False))


def _norm_matmul(x, g, w, layer, *, tm, tn, n_gelu=0):
    t, d = x.shape
    n = w.shape[2]
    return pl.pallas_call(
        functools.partial(_norm_mm_kernel, n_gelu=n_gelu),
        grid=(t // tm, n // tn),
        in_specs=[
            pl.BlockSpec((tm, d), lambda i, j: (i, 0)),
            pl.BlockSpec((1, d), lambda i, j: (0, 0)),
            pl.BlockSpec((None, d, tn), lambda i, j: (layer, 0, j)),
        ],
        out_specs=pl.BlockSpec((tm, tn), lambda i, j: (i, j)),
        out_shape=jax.ShapeDtypeStruct((t, n), BF16),
        scratch_shapes=[pltpu.VMEM((tm, d), BF16)],
        compiler_params=_params(("parallel", "arbitrary")),
        name="norm_matmul",
    )(x, g, w)


def _part_specs(parts, tm, tile_of):
    specs, ranges, lo = [], [], 0
    for p in parts:
        n = p.shape[0] // tm
        specs.append(pl.BlockSpec(
            (tm, p.shape[1]),
            lambda *idx, lo=lo, n=n: (jnp.clip(tile_of(*idx) - lo, 0, n - 1), 0)))
        ranges.append((lo, lo + n))
        lo += n
    return specs, tuple(ranges)


def _for_owner(i, ranges, refs, cond, fn):
    for (lo, hi), ref in zip(ranges, refs):
        pl.when(cond & (i >= lo) & (i < hi))(functools.partial(fn, ref))


MXU_COLS = 256


def _gmlp_in_kernel(*refs, ranges, n_col, tn):
    n_parts = len(ranges)
    x_refs = refs[:n_parts]
    g_ref, w_ref, z_ref, mu_ref, rs_ref, h_scr, s1_scr, s2_scr = refs[n_parts:]
    i, j = pl.program_id(0), pl.program_id(1)
    tm = h_scr.shape[0]

    def step(is_v, x_ref=None):
        if x_ref is not None:
            h_scr[...] = _rms(x_ref[...], g_ref[...]).astype(BF16)
            s1_scr[...] = jnp.zeros_like(s1_scr)
            s2_scr[...] = jnp.zeros_like(s2_scr)
        p1 = jnp.zeros((tm, LANES), F32)
        p2 = jnp.zeros((tm, LANES), F32)
        per_tile = GMLP_GDIM // MXU_COLS
        for c in range(tn // MXU_COLS):
            cols = slice(c * MXU_COLS, (c + 1) * MXU_COLS)
            zc = _gelu(_dot(h_scr[...], w_ref[:, cols]))
            lo = (c % per_tile) * MXU_COLS
            z_ref[c // per_tile, :, lo:lo + MXU_COLS] = zc.astype(z_ref.dtype)
            if is_v:
                for k in range(MXU_COLS // LANES):
                    zk = zc[:, k * LANES:(k + 1) * LANES]
                    p1 = p1 + zk
                    p2 = p2 + zk * zk
        if is_v:
            s1_scr[...] += p1
            s2_scr[...] += p2

    _for_owner(i, ranges, x_refs, j == 0, functools.partial(step, False))
    pl.when((j > 0) & (j < n_col // 2))(functools.partial(step, False))
    pl.when(j >= n_col // 2)(functools.partial(step, True))

    @pl.when(j == n_col - 1)
    def _():
        mu = jnp.sum(s1_scr[...], axis=-1, keepdims=True) * (1.0 / GMLP_HALF)
        msq = jnp.sum(s2_scr[...], axis=-1, keepdims=True) * (1.0 / GMLP_HALF)
        mu_ref[...] = mu
        rs_ref[...] = lax.rsqrt(jnp.maximum(msq - mu * mu, 0.0) + EPS)


def _gmlp_in(x_parts, g, w, layer, *, tm, tn):
    t = sum(p.shape[0] for p in x_parts)
    d = x_parts[0].shape[1]
    n_col = 2 * GMLP_HALF // tn
    sub = tn // GMLP_GDIM
    x_specs, ranges = _part_specs(x_parts, tm, lambda i, j: i)
    return pl.pallas_call(
        functools.partial(_gmlp_in_kernel, ranges=ranges, n_col=n_col, tn=tn),
        grid=(t // tm, n_col),
        in_specs=x_specs + [
            pl.BlockSpec((1, d), lambda i, j: (0, 0)),
            pl.BlockSpec((None, d, tn), lambda i, j: (layer, 0, j)),
        ],
        out_specs=[
            pl.BlockSpec((sub, tm, GMLP_GDIM), lambda i, j: (j, i, 0)),
            pl.BlockSpec((tm, 1), lambda i, j: (i, 0)),
            pl.BlockSpec((tm, 1), lambda i, j: (i, 0)),
        ],
        out_shape=[
            jax.ShapeDtypeStruct((2 * GMLP_GROUPS, t, GMLP_GDIM), BF16),
            jax.ShapeDtypeStruct((t, 1), F32),
            jax.ShapeDtypeStruct((t, 1), F32),
        ],
        scratch_shapes=[
            pltpu.VMEM((tm, d), BF16),
            pltpu.VMEM((tm, LANES), F32),
            pltpu.VMEM((tm, LANES), F32),
        ],
        compiler_params=_params(("parallel", "arbitrary")),
        name="gmlp_in",
    )(*x_parts, g, w)


def _mm_norm_res_kernel(a_ref, w_ref, x_ref, g_ref, o_ref):
    y = _dot(a_ref[...], w_ref[...])
    o_ref[...] = x_ref[...] + _rms(y, g_ref[...])


def _matmul_norm_res(a, w, layer, x, g, *, tm):
    t, k = a.shape
    d = w.shape[2]
    return pl.pallas_call(
        _mm_norm_res_kernel,
        grid=(t // tm,),
        in_specs=[
            pl.BlockSpec((tm, k), lambda i: (i, 0)),
            pl.BlockSpec((None, k, d), lambda i: (layer, 0, 0), pipeline_mode=pl.Buffered(1)),
            pl.BlockSpec((tm, d), lambda i: (i, 0)),
            pl.BlockSpec((1, d), lambda i: (0, 0)),
        ],
        out_specs=pl.BlockSpec((tm, d), lambda i: (i, 0)),
        out_shape=jax.ShapeDtypeStruct((t, d), F32),
        compiler_params=_params(("parallel",)),
        name="matmul_norm_res",
    )(a, w, x, g)


PAIRS = GMLP_GROUPS // 2


def _gmlp_out_kernel(*refs, ranges, tm):
    n_parts = len(ranges)
    (u1_ref, v1_ref, mu1_ref, rs1_ref, lng1_ref, lnb1_ref, ws1_ref, bs1_ref,
     u2_ref, v2_ref, mu2_ref, rs2_ref, lng2_ref, lnb2_ref, ws2_ref, bs2_ref,
     wo_ref, g_ref) = refs[:18]
    x_refs = refs[18:18 + n_parts]
    o_ref, a0_scr, a1_scr = refs[18 + n_parts:]
    s = pl.program_id(0)
    q = jnp.maximum(s - 1, 0)
    tile, pair = q // PAIRS, q % PAIRS

    @pl.when(s == 0)
    def _():
        a0_scr[...] = jnp.zeros_like(a0_scr)
        a1_scr[...] = jnp.zeros_like(a1_scr)

    def prep(u_ref, v_ref, mu_ref, rs_ref, lng_ref, lnb_ref, ws_ref, bs_ref):
        vn = ((v_ref[0].astype(F32) - mu_ref[...]) * rs_ref[...] * lng_ref[0]
              + lnb_ref[0]).astype(BF16)
        ws = ws_ref[0]
        bs = bs_ref[0]
        parts = []
        for n in range(tm // BLOCK):
            rows = slice(n * BLOCK, (n + 1) * BLOCK)
            mixed = _dot(ws, vn[rows, :]) + bs
            parts.append((u_ref[0, rows, :].astype(F32) * mixed).astype(BF16))
        return jnp.concatenate(parts, axis=0)

    def step(first, x_ref=None):
        row0 = pl.multiple_of(pair * (2 * GMLP_GDIM), 2 * GMLP_GDIM)
        proj = _dot(a0_scr[...], wo_ref[pl.ds(row0, GMLP_GDIM), :])
        if first:
            o_ref[...] = proj
        else:
            o_ref[...] += proj
        a1_scr[...] = prep(u1_ref, v1_ref, mu1_ref, rs1_ref, lng1_ref, lnb1_ref, ws1_ref, bs1_ref)
        o_ref[...] += _dot(a1_scr[...], wo_ref[pl.ds(row0 + GMLP_GDIM, GMLP_GDIM), :])
        a0_scr[...] = prep(u2_ref, v2_ref, mu2_ref, rs2_ref, lng2_ref, lnb2_ref, ws2_ref, bs2_ref)
        if x_ref is not None:
            o_ref[...] = x_ref[...] + _rms(o_ref[...], g_ref[...])

    pl.when(pair == 0)(functools.partial(step, True))
    pl.when((pair > 0) & (pair < PAIRS - 1))(functools.partial(step, False))
    _for_owner(tile, ranges, x_refs, pair == PAIRS - 1, functools.partial(step, False))


def _gmlp_out(z, mu, rstd, ln_g, ln_b, w_s, b_s, w_out, layer, x_parts, g, *, tm):
    t = sum(p.shape[0] for p in x_parts)
    d = x_parts[0].shape[1]
    n_tiles = t // tm
    last = GMLP_GROUPS * n_tiles - 1

    def proj_tile(s):
        return jnp.maximum(s - 1, 0) // PAIRS

    def prep_specs(pair_of):
        def at(f):
            return lambda s: f(*pair_of(s))
        return [
            pl.BlockSpec((1, tm, GMLP_GDIM), at(lambda i, c: (c, i, 0))),
            pl.BlockSpec((1, tm, GMLP_GDIM), at(lambda i, c: (GMLP_GROUPS + c, i, 0))),
            pl.BlockSpec((tm, 1), at(lambda i, c: (i, 0))),
            pl.BlockSpec((tm, 1), at(lambda i, c: (i, 0))),
            pl.BlockSpec((1, 1, GMLP_GDIM), at(lambda i, c: (c, 0, 0))),
            pl.BlockSpec((1, 1, GMLP_GDIM), at(lambda i, c: (c, 0, 0))),
            pl.BlockSpec((1, BLOCK, BLOCK), at(lambda i, c: (c, 0, 0))),
            pl.BlockSpec((1, BLOCK, 1), at(lambda i, c: (c, 0, 0))),
        ]

    def split(gidx):
        gidx = jnp.clip(gidx, 0, last)
        return gidx // GMLP_GROUPS, gidx % GMLP_GROUPS

    x_specs, ranges = _part_specs(x_parts, tm, proj_tile)
    prep_args = (z, z, mu, rstd, ln_g, ln_b, w_s, b_s)
    return pl.pallas_call(
        functools.partial(_gmlp_out_kernel, ranges=ranges, tm=tm),
        grid=(PAIRS * n_tiles + 1,),
        in_specs=prep_specs(lambda s: split(2 * s - 1)) + prep_specs(lambda s: split(2 * s)) + [
            pl.BlockSpec((None, GMLP_HALF, d), lambda s: (layer, 0, 0),
                         pipeline_mode=pl.Buffered(1)),
            pl.BlockSpec((1, d), lambda s: (0, 0)),
        ] + x_specs,
        out_specs=pl.BlockSpec((tm, d), lambda s: (proj_tile(s), 0)),
        out_shape=jax.ShapeDtypeStruct((t, d), F32),
        scratch_shapes=[
            pltpu.VMEM((tm, GMLP_GDIM), BF16),
            pltpu.VMEM((tm, GMLP_GDIM), BF16),
        ],
        compiler_params=_params(("arbitrary",)),
        name="gmlp_out",
    )(*prep_args, *prep_args, w_out, g, *x_parts)


def _win_attn_kernel(sink_ref, q_ref, kp_ref, kc_ref, kn_ref, vp_ref, vc_ref, vn_ref, bias_ref,
                     o_ref, *, r_blocks, n_steps):
    i = pl.program_id(1)
    kall = jnp.concatenate([kp_ref[0], kc_ref[0], kn_ref[0]], axis=0)
    vall = jnp.concatenate([vp_ref[0], vc_ref[0], vn_ref[0]], axis=0)
    col = lax.broadcasted_iota(jnp.int32, (1, 3 * BLOCK), 1)
    sk = jnp.concatenate(
        [jnp.full((BLOCK, 1), sink_ref[h], F32) for h in range(N_HEADS)], axis=0)
    hrows = KV_GROUP * BLOCK
    for r in range(r_blocks):
        rows = slice(r * BLOCK, (r + 1) * BLOCK)
        oob = None
        if r == 0:
            oob = (col < BLOCK) & (i == 0)
        if r == r_blocks - 1:
            hi = (col >= 2 * BLOCK) & (i == n_steps - 1)
            oob = hi if oob is None else (oob | hi)
        parts = []
        for kh in range(N_KV_HEADS):
            kb = kall[r * BLOCK:(r + 3) * BLOCK, kh * HEAD_DIM:(kh + 1) * HEAD_DIM]
            qs = jnp.concatenate(
                [q_ref[0, rows, h * HEAD_DIM:(h + 1) * HEAD_DIM]
                 for h in range(kh * KV_GROUP, (kh + 1) * KV_GROUP)], axis=0)
            parts.append(
                lax.dot_general(qs, kb, (((1,), (1,)), ((), ())), preferred_element_type=F32))
        s = jnp.concatenate(parts, axis=0) + bias_ref[...]
        if oob is not None:
            s = jnp.where(oob, NEG_INF, s)
        m = jnp.maximum(jnp.max(s, axis=-1, keepdims=True), sk)
        p = jnp.exp2(s - m)
        inv = 1.0 / (jnp.sum(p, axis=-1, keepdims=True) + jnp.exp2(sk - m))
        pb = p.astype(BF16)
        for kh in range(N_KV_HEADS):
            vb = vall[r * BLOCK:(r + 3) * BLOCK, kh * HEAD_DIM:(kh + 1) * HEAD_DIM]
            hsl = slice(kh * hrows, (kh + 1) * hrows)
            o = _dot(pb[hsl], vb) * inv[hsl]
            for gq in range(KV_GROUP):
                h = kh * KV_GROUP + gq
                o_ref[0, rows, h * HEAD_DIM:(h + 1) * HEAD_DIM] = (
                    o[gq * BLOCK:(gq + 1) * BLOCK].astype(o_ref.dtype))


def _win_attention(qkv, sink, bias, *, nseq, r_blocks):
    tq = r_blocks * BLOCK
    n_steps = SEQ // tq
    nb = SEQ // BLOCK
    qkv3 = qkv.reshape(nseq, SEQ, qkv.shape[-1])
    kcol = N_HEADS * HEAD_DIM // (N_KV_HEADS * HEAD_DIM)
    vcol = kcol + 1
    kvw = N_KV_HEADS * HEAD_DIM

    def prev_map(c):
        return lambda b, i: (b, jnp.maximum(i * r_blocks - 1, 0), c)

    def cur_map(c):
        return lambda b, i: (b, i, c)

    def next_map(c):
        return lambda b, i: (b, jnp.minimum((i + 1) * r_blocks, nb - 1), c)

    out = pl.pallas_call(
        functools.partial(_win_attn_kernel, r_blocks=r_blocks, n_steps=n_steps),
        grid=(nseq, n_steps),
        in_specs=[
            pl.BlockSpec(memory_space=pltpu.SMEM),
            pl.BlockSpec((1, tq, N_HEADS * HEAD_DIM), lambda b, i: (b, i, 0)),
            pl.BlockSpec((1, BLOCK, kvw), prev_map(kcol)),
            pl.BlockSpec((1, tq, kvw), cur_map(kcol)),
            pl.BlockSpec((1, BLOCK, kvw), next_map(kcol)),
            pl.BlockSpec((1, BLOCK, kvw), prev_map(vcol)),
            pl.BlockSpec((1, tq, kvw), cur_map(vcol)),
            pl.BlockSpec((1, BLOCK, kvw), next_map(vcol)),
            pl.BlockSpec((N_HEADS * BLOCK, 3 * BLOCK), lambda b, i: (0, 0)),
        ],
        out_specs=pl.BlockSpec((1, tq, N_HEADS * HEAD_DIM), lambda b, i: (b, i, 0)),
        out_shape=jax.ShapeDtypeStruct((nseq, SEQ, N_HEADS * HEAD_DIM), BF16),
        compiler_params=_params(("parallel", "parallel")),
        name="win_attention",
    )(sink, qkv3, qkv3, qkv3, qkv3, qkv3, qkv3, qkv3, bias)
    return out.reshape(nseq * SEQ, N_HEADS * HEAD_DIM)


def _t5_bucket(rel):
    half = N_BUCKETS // 2
    exact = half // 2
    n = np.abs(rel)
    large = exact + (np.log(np.maximum(n, 1) / exact) / np.log(MAX_DISTANCE / exact)
                     * (half - exact)).astype(np.int32)
    large = np.minimum(large, half - 1)
    return (np.where(rel > 0, half, 0) + np.where(n < exact, n, large)).astype(np.int32)


def _band_bias(rel_table):
    rel = np.arange(3 * BLOCK)[None, :] - BLOCK - np.arange(BLOCK)[:, None]
    onehot = (_t5_bucket(rel).reshape(1, -1) == np.arange(N_BUCKETS)[:, None]).astype(np.float32)
    bias = jnp.einsum("bh,bn->hn", rel_table.astype(F32) * LOG2E, onehot,
                      precision=lax.Precision.HIGHEST)
    bias = jnp.where((np.abs(rel) <= WINDOW).reshape(1, -1), bias, NEG_INF)
    return bias.reshape(N_HEADS * BLOCK, 3 * BLOCK)


def _kv_kernel(mem_ref, g_ref, w_ref, o_ref):
    h = _rms(mem_ref[0], g_ref[0]).astype(BF16)
    o_ref[0, 0] = _dot(h, w_ref[0]).astype(o_ref.dtype)


def _kv_proj(mem, g, w_kv):
    nseq = mem.shape[0]
    return pl.pallas_call(
        _kv_kernel,
        grid=(DEPTH, nseq),
        in_specs=[
            pl.BlockSpec((1, MEM_LEN, D_MODEL), lambda l, b: (b, 0, 0)),
            pl.BlockSpec((1, 1, D_MODEL), lambda l, b: (l, 0, 0)),
            pl.BlockSpec((1, D_MODEL, 2 * MEM_WIDTH), lambda l, b: (l, 0, 0)),
        ],
        out_specs=pl.BlockSpec((1, 1, MEM_LEN, 2 * MEM_WIDTH), lambda l, b: (l, b, 0, 0)),
        out_shape=jax.ShapeDtypeStruct((DEPTH, nseq, MEM_LEN, 2 * MEM_WIDTH), BF16),
        compiler_params=_params(("parallel", "parallel")),
        name="kv_proj",
    )(mem, g, w_kv)


def _mem_attn_kernel(x_ref, gpre_ref, gpost_ref, wq_ref, kv_ref, wo_ref, o_ref):
    x = x_ref[...]
    h = _rms(x, gpre_ref[...]).astype(BF16)
    q = (_dot(h, wq_ref[...]) * (HEAD_DIM ** -0.5)).astype(BF16)
    outs = []
    for hd in range(MEM_HEADS):
        sl = slice(hd * HEAD_DIM, (hd + 1) * HEAD_DIM)
        k = kv_ref[0, :, sl]
        v = kv_ref[0, :, MEM_WIDTH + hd * HEAD_DIM:MEM_WIDTH + (hd + 1) * HEAD_DIM]
        s = lax.dot_general(q[:, sl], k, (((1,), (1,)), ((), ())), preferred_element_type=F32)
        m = jnp.max(s, axis=-1, keepdims=True)
        p = jnp.exp(s - m)
        denom = jnp.sum(p, axis=-1, keepdims=True)
        outs.append((_dot(p.astype(BF16), v) / denom).astype(BF16))
    y = _dot(jnp.concatenate(outs, axis=-1), wo_ref[...])
    o_ref[...] = x + _rms(y, gpost_ref[...])


def _mem_attention(x, gpre, gpost, w_q, kv, w_o, layer, *, tm):
    t, d = x.shape
    per_seq = SEQ // tm
    return pl.pallas_call(
        _mem_attn_kernel,
        grid=(t // tm,),
        in_specs=[
            pl.BlockSpec((tm, d), lambda i: (i, 0)),
            pl.BlockSpec((1, d), lambda i: (0, 0)),
            pl.BlockSpec((1, d), lambda i: (0, 0)),
            pl.BlockSpec((None, d, MEM_WIDTH), lambda i: (layer, 0, 0)),
            pl.BlockSpec((None, 1, MEM_LEN, 2 * MEM_WIDTH), lambda i: (layer, i // per_seq, 0, 0)),
            pl.BlockSpec((None, MEM_WIDTH, d), lambda i: (layer, 0, 0)),
        ],
        out_specs=pl.BlockSpec((tm, d), lambda i: (i, 0)),
        out_shape=jax.ShapeDtypeStruct((t, d), F32),
        compiler_params=_params(("parallel",)),
        name="mem_attention",
    )(x, gpre, gpost, w_q, kv, w_o)


def _ffn_kernel(*refs, n_chunks, cast_next):
    if cast_next:
        (x_ref, gpre_ref, gpost_ref, wg_ref, wu_ref, wd_ref, ngu_ref, nd_ref,
         o_ref, ngu_out, nd_out, h_scr) = refs
    else:
        x_ref, gpre_ref, gpost_ref, wg_ref, wu_ref, wd_ref, o_ref, h_scr = refs
    j = pl.program_id(1)

    def chunk(first, last):
        if cast_next:
            ngu_out[...] = ngu_ref[...].astype(BF16)
            nd_out[...] = nd_ref[...].astype(BF16)
        if first:
            h_scr[...] = _rms(x_ref[...], gpre_ref[...]).astype(BF16)
        h = h_scr[...]
        gate = _dot(h, wg_ref[...])
        up = _dot(h, wu_ref[...])
        a = (gate * _sigmoid(gate) * up).astype(BF16)
        if first:
            o_ref[...] = _dot(a, wd_ref[...])
        else:
            o_ref[...] += _dot(a, wd_ref[...])
        if last:
            o_ref[...] = x_ref[...] + _rms(o_ref[...], gpost_ref[...])

    pl.when(j == 0)(functools.partial(chunk, True, False))
    pl.when((j > 0) & (j < n_chunks - 1))(functools.partial(chunk, False, False))
    pl.when(j == n_chunks - 1)(functools.partial(chunk, False, True))


def _ffn(x, gpre, gpost, w_gate_up, w_down, layer, *, tm, tn, row0=0, rows=None, next_f32=None):
    d = x.shape[1]
    t = x.shape[0] if rows is None else rows
    tile0 = row0 // tm
    n_chunks = D_FF // tn
    n_tiles = t // tm
    in_specs_next, out_specs_next, out_shape_next, args_next = [], [], [], []
    if next_f32 is not None:
        ngu, nd, nl = next_f32
        gu_blk = (d // n_tiles, 2 * D_FF // n_chunks)
        d_blk = (D_FF // (n_tiles * n_chunks), d)
        assert gu_blk[0] * n_tiles == d and gu_blk[1] * n_chunks == 2 * D_FF
        assert d_blk[0] * n_tiles * n_chunks == D_FF and d_blk[0] % (2 * SUBLANES) == 0
        in_specs_next = [
            pl.BlockSpec((None,) + gu_blk, lambda i, j: (nl, i, j)),
            pl.BlockSpec((None,) + d_blk, lambda i, j: (nl, i * n_chunks + j, 0)),
        ]
        out_specs_next = [
            pl.BlockSpec(gu_blk, lambda i, j: (i, j)),
            pl.BlockSpec(d_blk, lambda i, j: (i * n_chunks + j, 0)),
        ]
        out_shape_next = [
            jax.ShapeDtypeStruct((d, 2 * D_FF), BF16),
            jax.ShapeDtypeStruct((D_FF, d), BF16),
        ]
        args_next = [ngu, nd]
    x_bytes = tm * d * 4
    fixed = 2 * x_bytes + tm * d * 2 + 2 * 3 * d * tn * 2
    temps = 2 * tm * tn * 4 + tm * tn * 2 + tm * d * 4
    x_mode = None if fixed + 2 * x_bytes + temps <= VMEM_LIMIT else pl.Buffered(1)
    out = pl.pallas_call(
        functools.partial(_ffn_kernel, n_chunks=n_chunks, cast_next=next_f32 is not None),
        grid=(n_tiles, n_chunks),
        in_specs=[
            pl.BlockSpec((tm, d), lambda i, j: (i + tile0, 0), pipeline_mode=x_mode),
            pl.BlockSpec((1, d), lambda i, j: (0, 0)),
            pl.BlockSpec((1, d), lambda i, j: (0, 0)),
            pl.BlockSpec((None, d, tn), lambda i, j: (layer, 0, j)),
            pl.BlockSpec((None, d, tn), lambda i, j: (layer, 0, j + n_chunks)),
            pl.BlockSpec((None, tn, d), lambda i, j: (layer, j, 0)),
        ] + in_specs_next,
        out_specs=[pl.BlockSpec((tm, d), lambda i, j: (i, 0))] + out_specs_next,
        out_shape=[jax.ShapeDtypeStruct((t, d), F32)] + out_shape_next,
        scratch_shapes=[pltpu.VMEM((tm, d), BF16)],
        compiler_params=_params(("parallel", "arbitrary")),
        name="ffn",
    )(x, gpre, gpost, w_gate_up, w_gate_up, w_down, *args_next)
    if next_f32 is None:
        return out[0]
    return out[0], out[1][None], out[2][None]


N_CT = D_RNN // LANES
HALO = 2 * SUBLANES


def _gate_windows():
    lo_hi = []
    for j in range(N_CT):
        h0 = (LANES * j) // RNN_BDIM
        h1 = (LANES * j + LANES - 1) // RNN_BDIM
        lo = (RNN_BDIM * h0) // LANES * LANES
        hi = -(-(RNN_BDIM * (h1 + 1)) // LANES) * LANES
        lo_hi.append((lo, hi))
    width = max(hi - lo for lo, hi in lo_hi)
    starts = tuple(min(lo, D_RNN - width) for lo, _ in lo_hi)
    return starts, width


GATE_STARTS, GATE_WIDTH = _gate_windows()


def _gate_weights(w_a, w_x):
    def tile(w, j):
        st = GATE_STARTS[j]
        pieces = []
        for h in range((LANES * j) // RNN_BDIM, (LANES * j + LANES - 1) // RNN_BDIM + 1):
            c0 = max(LANES * j, RNN_BDIM * h) - RNN_BDIM * h
            c1 = min(LANES * (j + 1), RNN_BDIM * (h + 1)) - RNN_BDIM * h
            top = RNN_BDIM * h - st
            pieces.append(jnp.pad(w[h, :, c0:c1], ((top, GATE_WIDTH - RNN_BDIM - top), (0, 0))))
        return jnp.concatenate(pieces, axis=1)

    wa, wx = (w_a * 0.5).astype(BF16), (w_x * 0.5).astype(BF16)
    return jnp.stack([jnp.concatenate([tile(wa, j), tile(wx, j)], axis=1) for j in range(N_CT)])


def _scan_kernel(*refs, ts, n_tiles, nseq, reverse):
    if reverse:
        (xp_ref, xc_ref, xn_ref, cw_ref, cb_ref, wp_ref, ba_ref, bx_ref, lam_ref,
         hsf_ref, gate_ref, o_ref, conv_scr, convb_scr, a_scr, u_scr, h_scr) = refs
    else:
        (xp_ref, xc_ref, xn_ref, cw_ref, cb_ref, wp_ref, ba_ref, bx_ref, lam_ref,
         o_ref, conv_scr, convb_scr, a_scr, u_scr, h_scr) = refs
    i = pl.program_id(0)
    tile = (n_tiles - 1 - i) if reverse else i

    @pl.when(i == 0)
    def _():
        h_scr[...] = jnp.zeros_like(h_scr)
        a_scr[...] = jnp.zeros_like(a_scr)
        u_scr[...] = jnp.zeros_like(u_scr)

    cw = cw_ref[...]
    cb = cb_ref[...]
    shifts = [k - CONV_LEFT for k in range(CONV_WIDTH) if k != CONV_LEFT]
    rr = lax.broadcasted_iota(jnp.int32, (ts, ts + 2 * HALO), 0)
    cc = lax.broadcasted_iota(jnp.int32, (ts, ts + 2 * HALO), 1)
    sel = jnp.concatenate(
        [jnp.where(cc == rr + HALO + s, 1.0, 0.0).astype(BF16) for s in shifts], axis=0)
    def conv_columns(cols):
        for b in range(nseq):
            prev = xp_ref[b, :, cols]
            nxt = xn_ref[b, :, cols]
            prev = jnp.where(tile == 0, jnp.zeros_like(prev), prev)
            nxt = jnp.where(tile == n_tiles - 1, jnp.zeros_like(nxt), nxt)
            cur = xc_ref[b, :, cols]
            shifted = _dot(sel, jnp.concatenate([prev, cur, nxt], axis=0))
            acc = cb[:, cols] + cur.astype(F32) * cw[CONV_LEFT:CONV_LEFT + 1, cols]
            for n, s in enumerate(shifts):
                k = s + CONV_LEFT
                acc = acc + shifted[n * ts:(n + 1) * ts] * cw[k:k + 1, cols]
            conv_scr[b * ts:(b + 1) * ts, cols] = acc
            convb_scr[b * ts:(b + 1) * ts, cols] = acc.astype(BF16)

    def gate_tile(j):
        st = GATE_STARTS[j]
        sl = slice(LANES * j, LANES * (j + 1))
        pre = _dot(convb_scr[:, st:st + GATE_WIDTH], wp_ref[j])
        tr = jnp.tanh(pre[:, :LANES] + ba_ref[:, sl])
        ti = jnp.tanh(pre[:, LANES:] + bx_ref[:, sl])
        z = -lam_ref[:, sl]
        softplus = jnp.maximum(z, 0.0) + jnp.log(1.0 + jnp.exp(-jnp.abs(z)))
        half = (-0.5 * LRU_C * LOG2E) * softplus
        a = jnp.exp2(half + half * tr)
        y = 1.0 - a * a
        root = jnp.where(y > 0.0, y * lax.rsqrt(y), 0.0)
        hx = 0.5 * conv_scr[:, sl]
        u = root * (hx + hx * ti)
        for b in range(nseq):
            a_scr[j, pl.ds(b, ts, stride=SUBLANES), :] = a[b * ts:(b + 1) * ts]
            u_scr[j, pl.ds(b, ts, stride=SUBLANES), :] = u[b * ts:(b + 1) * ts]

    conv_columns(slice(None))
    for j in range(N_CT):
        gate_tile(j)

    def step(k, h):
        t = (ts - 1 - k) if reverse else k
        row = pl.multiple_of(t * SUBLANES, SUBLANES)
        h = a_scr[:, pl.ds(row, SUBLANES), :] * h + u_scr[:, pl.ds(row, SUBLANES), :]
        u_scr[:, pl.ds(row, SUBLANES), :] = h
        return h

    h_scr[...] = lax.fori_loop(0, ts, step, h_scr[...], unroll=8)

    for b in range(nseq):
        for j in range(N_CT):
            sl = slice(LANES * j, LANES * (j + 1))
            y = u_scr[j, pl.ds(b, ts, stride=SUBLANES), :]
            if reverse:
                y = gate_ref[b, :, sl].astype(F32) * (y + hsf_ref[b, :, sl].astype(F32))
            o_ref[b, :, sl] = y.astype(o_ref.dtype)


def _rglru_scan(y_in, conv_w, conv_b, wp, b_a, b_x, lam, hs_fwd, *, ts, nseq, reverse):
    n_tiles = SEQ // ts
    per_tile = ts // HALO

    def tmap(i):
        return (n_tiles - 1 - i) if reverse else i

    in_specs = [
        pl.BlockSpec((nseq, HALO, D_RNN), lambda i: (0, jnp.maximum(tmap(i) * per_tile - 1, 0), 1)),
        pl.BlockSpec((nseq, ts, D_RNN), lambda i: (0, tmap(i), 1)),
        pl.BlockSpec((nseq, HALO, D_RNN),
                     lambda i: (0, jnp.minimum((tmap(i) + 1) * per_tile, SEQ // HALO - 1), 1)),
        pl.BlockSpec((CONV_WIDTH, D_RNN), lambda i: (0, 0)),
        pl.BlockSpec((1, D_RNN), lambda i: (0, 0)),
        pl.BlockSpec((N_CT, GATE_WIDTH, 2 * LANES), lambda i: (0, 0, 0)),
        pl.BlockSpec((1, D_RNN), lambda i: (0, 0)),
        pl.BlockSpec((1, D_RNN), lambda i: (0, 0)),
        pl.BlockSpec((1, D_RNN), lambda i: (0, 0)),
    ]
    args = [y_in, y_in, y_in, conv_w, conv_b, wp, b_a, b_x, lam]
    if reverse:
        in_specs += [
            pl.BlockSpec((nseq, ts, D_RNN), lambda i: (0, tmap(i), 0)),
            pl.BlockSpec((nseq, ts, D_RNN), lambda i: (0, tmap(i), 0)),
        ]
        args += [hs_fwd, y_in]
    return pl.pallas_call(
        functools.partial(_scan_kernel, ts=ts, n_tiles=n_tiles, nseq=nseq, reverse=reverse),
        grid=(n_tiles,),
        in_specs=in_specs,
        out_specs=pl.BlockSpec((nseq, ts, D_RNN), lambda i: (0, tmap(i), 0)),
        out_shape=jax.ShapeDtypeStruct((nseq, SEQ, D_RNN), BF16),
        scratch_shapes=[
            pltpu.VMEM((nseq * ts, D_RNN), F32),
            pltpu.VMEM((nseq * ts, D_RNN), BF16),
            pltpu.VMEM((N_CT, ts * SUBLANES, LANES), F32),
            pltpu.VMEM((N_CT, ts * SUBLANES, LANES), F32),
            pltpu.VMEM((N_CT, SUBLANES, LANES), F32),
        ],
        compiler_params=_params(("arbitrary",)),
        name="rglru_scan_bwd" if reverse else "rglru_scan_fwd",
    )(*args)


def kernel(x_prompt, x_sample, mem_prompt, mem_sample, norms, rel_bias, a_w_in, a_ln_g, a_ln_b, a_w_s, a_b_s, a_w_out, b_w_qkv, b_sink, b_w_o, c_w_in, c_conv_w, c_conv_b, c_w_a, c_b_a, c_w_x, c_b_x, c_lam, c_w_out, m_w_q, m_w_kv, m_w_o, f_w_gate_up, f_w_down):
    n_prompt = x_prompt.shape[0]
    nseq = n_prompt + x_sample.shape[0]
    assert nseq <= SUBLANES
    t = nseq * SEQ
    t_prompt = n_prompt * SEQ
    x = None
    x_parts = [x_prompt.reshape(t_prompt, D_MODEL), x_sample.reshape(t - t_prompt, D_MODEL)]
    mem = jnp.concatenate([mem_prompt, mem_sample], axis=0)

    def gain(i, k):
        return norms[i, k].reshape(1, D_MODEL)

    a_w_in_b, a_w_out_b = a_w_in.astype(BF16), a_w_out.astype(BF16)
    q_cols = N_HEADS * HEAD_DIM
    b_w_qkv_b = jnp.concatenate(
        [b_w_qkv[:, :, :q_cols] * (HEAD_DIM ** -0.5 * LOG2E), b_w_qkv[:, :, q_cols:]],
        axis=-1).astype(BF16)
    b_w_o_b = b_w_o.astype(BF16)
    c_w_in_b, c_w_out_b = c_w_in.astype(BF16), c_w_out.astype(BF16)
    m_w_q_b, m_w_o_b = m_w_q.astype(BF16), m_w_o.astype(BF16)
    f_gu_b, f_d_b = f_w_gate_up[:1].astype(BF16), f_w_down[:1].astype(BF16)

    kv_all = _kv_proj(mem, norms[:, 4].reshape(DEPTH, 1, D_MODEL), m_w_kv.astype(BF16))

    for i in range(DEPTH):
        kind, j = i % N_MIXERS, i // N_MIXERS
        if kind == 0:
            parts = x_parts if x is None else [x]
            z, mu, rstd = _gmlp_in(parts, gain(i, 0), a_w_in_b, j, tm=1024,
                                   tn=(4 if len(parts) == 1 else 2) * GMLP_GDIM)
            x = _gmlp_out(z, mu, rstd, a_ln_g[j].reshape(GMLP_GROUPS, 1, GMLP_GDIM),
                          a_ln_b[j].reshape(GMLP_GROUPS, 1, GMLP_GDIM),
                          a_w_s[j].astype(BF16), a_b_s[j].reshape(GMLP_GROUPS, BLOCK, 1),
                          a_w_out_b, j, parts, gain(i, 1), tm=512)
        elif kind == 1:
            qkv = _norm_matmul(x, gain(i, 0), b_w_qkv_b, j, tm=1024, tn=1024)
            o = _win_attention(qkv, b_sink[j] * LOG2E, _band_bias(rel_bias), nseq=nseq, r_blocks=4)
            x = _matmul_norm_res(o, b_w_o_b, j, x, gain(i, 1), tm=512)
        else:
            y_in = _norm_matmul(x, gain(i, 0), c_w_in_b, j, tm=1024, tn=D_RNN // 2, n_gelu=2)
            y_in = y_in.reshape(nseq, SEQ, 2 * D_RNN)
            scan_args = (c_conv_w[j], c_conv_b[j].reshape(1, D_RNN))
            hs = None
            for d in range(2):
                hs = _rglru_scan(y_in, *scan_args, _gate_weights(c_w_a[j, d], c_w_x[j, d]),
                                 (c_b_a[j, d] * 0.5).reshape(1, D_RNN),
                                 (c_b_x[j, d] * 0.5).reshape(1, D_RNN),
                                 c_lam[j, d].reshape(1, D_RNN), hs, ts=64, nseq=nseq,
                                 reverse=(d == 1))
            x = _matmul_norm_res(hs.reshape(t, D_RNN), c_w_out_b, j, x, gain(i, 1), tm=512)
        x = _mem_attention(x, gain(i, 2), gain(i, 3), m_w_q_b, kv_all, m_w_o_b, i, tm=1024)
        ffn = functools.partial(_ffn, x, gain(i, 5), gain(i, 6), f_gu_b, f_d_b, 0)
        if i < DEPTH - 1:
            x, f_gu_b, f_d_b = ffn(tm=768, tn=512, next_f32=(f_w_gate_up, f_w_down, i + 1))

    y_prompt = ffn(tm=1024, tn=256, row0=0, rows=t_prompt).reshape(n_prompt, SEQ, D_MODEL)
    y_sample = ffn(tm=1024, tn=256, row0=t_prompt, rows=t - t_prompt).reshape(
        nseq - n_prompt, SEQ, D_MODEL)
    return (y_prompt, y_sample)
```

```python
import functools

import numpy as np
import jax
import jax.numpy as jnp
from jax import lax
from jax.experimental import pallas as pl
from jax.experimental.pallas import tpu as pltpu

D_MODEL = 2048
SEQ = 4096
DEPTH = 4
N_MIXERS = 3
BLOCK = 128
HEAD_DIM = 128
EPS = 1e-6
NEG_INF = -1e30
GMLP_HALF = 2 * D_MODEL
GMLP_GROUPS = 8
GMLP_GDIM = GMLP_HALF // GMLP_GROUPS
N_HEADS = D_MODEL // HEAD_DIM
N_KV_HEADS = 4
KV_GROUP = N_HEADS // N_KV_HEADS
WINDOW = 128
N_BUCKETS = 32
MAX_DISTANCE = 128
D_RNN = 2816
RNN_BLOCKS = 16
RNN_BDIM = D_RNN // RNN_BLOCKS
CONV_WIDTH = 4
CONV_LEFT = 2
LRU_C = 8.0
MEM_LEN = 256
MEM_HEADS = 4
MEM_WIDTH = MEM_HEADS * HEAD_DIM
D_FF = 5632

LANES = 128
SUBLANES = 8
VMEM_LIMIT = 56 * 1024 * 1024

F32 = jnp.float32
BF16 = jnp.bfloat16
LOG2E = float(np.log2(np.e))


def _params(sem):
    return pltpu.CompilerParams(dimension_semantics=sem, vmem_limit_bytes=VMEM_LIMIT)


def _rms(x, g):
    ms = jnp.mean(x * x, axis=-1, keepdims=True)
    return x * lax.rsqrt(ms + EPS) * g


def _gelu(x):
    k1 = float(-2.0 * np.sqrt(2.0 / np.pi) * LOG2E)
    return x / (1.0 + jnp.exp2(x * (k1 + (k1 * 0.044715) * (x * x))))


def _sigmoid(x):
    return 1.0 / (1.0 + jnp.exp(-x))


def _dot(a, b):
    return jnp.dot(a, b, preferred_element_type=F32)


def _norm_mm_kernel(x_ref, g_ref, w_ref, o_ref, h_scr, *, n_gelu):
    j = pl.program_id(1)

    def emit(act, first):
        if first:
            h_scr[...] = _rms(x_ref[...], g_ref[...]).astype(BF16)
        z = _dot(h_scr[...], w_ref[...])
        o_ref[...] = (_gelu(z) if act else z).astype(o_ref.dtype)

    pl.when(j == 0)(functools.partial(emit, n_gelu > 0, True))
    if n_gelu > 1:
        pl.when((j > 0) & (j < n_gelu))(functools.partial(emit, True, False))
    pl.when(j >= max(n_gelu, 1))(functools.partial(emit, False, False))


def _norm_matmul(x, g, w, layer, *, tm, tn, n_gelu=0):
    t, d = x.shape
    n = w.shape[2]
    return pl.pallas_call(
        functools.partial(_norm_mm_kernel, n_gelu=n_gelu),
        grid=(t // tm, n // tn),
        in_specs=[
            pl.BlockSpec((tm, d), lambda i, j: (i, 0)),
            pl.BlockSpec((1, d), lambda i, j: (0, 0)),
            pl.BlockSpec((None, d, tn), lambda i, j: (layer, 0, j)),
        ],
        out_specs=pl.BlockSpec((tm, tn), lambda i, j: (i, j)),
        out_shape=jax.ShapeDtypeStruct((t, n), BF16),
        scratch_shapes=[pltpu.VMEM((tm, d), BF16)],
        compiler_params=_params(("parallel", "arbitrary")),
        name="norm_matmul",
    )(x, g, w)


def _part_specs(parts, tm, tile_of):
    specs, ranges, lo = [], [], 0
    for p in parts:
        n = p.shape[0] // tm
        specs.append(pl.BlockSpec(
            (tm, p.shape[1]),
            lambda *idx, lo=lo, n=n: (jnp.clip(tile_of(*idx) - lo, 0, n - 1), 0)))
        ranges.append((lo, lo + n))
        lo += n
    return specs, tuple(ranges)


def _for_owner(i, ranges, refs, cond, fn):
    for (lo, hi), ref in zip(ranges, refs):
        pl.when(cond & (i >= lo) & (i < hi))(functools.partial(fn, ref))


MXU_COLS = 256


def _gmlp_in_kernel(*refs, ranges, n_col, tn):
    n_parts = len(ranges)
    x_refs = refs[:n_parts]
    g_ref, w_ref, z_ref, mu_ref, rs_ref, h_scr, s1_scr, s2_scr = refs[n_parts:]
    i, j = pl.program_id(0), pl.program_id(1)
    tm = h_scr.shape[0]

    def step(is_v, x_ref=None):
        if x_ref is not None:
            h_scr[...] = _rms(x_ref[...], g_ref[...]).astype(BF16)
            s1_scr[...] = jnp.zeros_like(s1_scr)
            s2_scr[...] = jnp.zeros_like(s2_scr)
        p1 = jnp.zeros((tm, LANES), F32)
        p2 = jnp.zeros((tm, LANES), F32)
        per_tile = GMLP_GDIM // MXU_COLS
        for c in range(tn // MXU_COLS):
            cols = slice(c * MXU_COLS, (c + 1) * MXU_COLS)
            zc = _gelu(_dot(h_scr[...], w_ref[:, cols]))
            lo = (c % per_tile) * MXU_COLS
            z_ref[c // per_tile, :, lo:lo + MXU_COLS] = zc.astype(z_ref.dtype)
            if is_v:
                for k in range(MXU_COLS // LANES):
                    zk = zc[:, k * LANES:(k + 1) * LANES]
                    p1 = p1 + zk
                    p2 = p2 + zk * zk
        if is_v:
            s1_scr[...] += p1
            s2_scr[...] += p2

    _for_owner(i, ranges, x_refs, j == 0, functools.partial(step, False))
    pl.when((j > 0) & (j < n_col // 2))(functools.partial(step, False))
    pl.when(j >= n_col // 2)(functools.partial(step, True))

    @pl.when(j == n_col - 1)
    def _():
        mu = jnp.sum(s1_scr[...], axis=-1, keepdims=True) * (1.0 / GMLP_HALF)
        msq = jnp.sum(s2_scr[...], axis=-1, keepdims=True) * (1.0 / GMLP_HALF)
        mu_ref[...] = mu
        rs_ref[...] = lax.rsqrt(jnp.maximum(msq - mu * mu, 0.0) + EPS)


def _gmlp_in(x_parts, g, w, layer, *, tm, tn):
    t = sum(p.shape[0] for p in x_parts)
    d = x_parts[0].shape[1]
    n_col = 2 * GMLP_HALF // tn
    sub = tn // GMLP_GDIM
    x_specs, ranges = _part_specs(x_parts, tm, lambda i, j: i)
    return pl.pallas_call(
        functools.partial(_gmlp_in_kernel, ranges=ranges, n_col=n_col, tn=tn),
        grid=(t // tm, n_col),
        in_specs=x_specs + [
            pl.BlockSpec((1, d), lambda i, j: (0, 0)),
            pl.BlockSpec((None, d, tn), lambda i, j: (layer, 0, j)),
        ],
        out_specs=[
            pl.BlockSpec((sub, tm, GMLP_GDIM), lambda i, j: (j, i, 0)),
            pl.BlockSpec((tm, 1), lambda i, j: (i, 0)),
            pl.BlockSpec((tm, 1), lambda i, j: (i, 0)),
        ],
        out_shape=[
            jax.ShapeDtypeStruct((2 * GMLP_GROUPS, t, GMLP_GDIM), BF16),
            jax.ShapeDtypeStruct((t, 1), F32),
            jax.ShapeDtypeStruct((t, 1), F32),
        ],
        scratch_shapes=[
            pltpu.VMEM((tm, d), BF16),
            pltpu.VMEM((tm, LANES), F32),
            pltpu.VMEM((tm, LANES), F32),
        ],
        compiler_params=_params(("parallel", "arbitrary")),
        name="gmlp_in",
    )(*x_parts, g, w)


def _mm_norm_res_kernel(a_ref, w_ref, x_ref, g_ref, o_ref):
    y = _dot(a_ref[...], w_ref[...])
    o_ref[...] = x_ref[...] + _rms(y, g_ref[...])


def _matmul_norm_res(a, w, layer, x, g, *, tm):
    t, k = a.shape
    d = w.shape[2]
    return pl.pallas_call(
        _mm_norm_res_kernel,
        grid=(t // tm,),
        in_specs=[
            pl.BlockSpec((tm, k), lambda i: (i, 0)),
            pl.BlockSpec((None, k, d), lambda i: (layer, 0, 0), pipeline_mode=pl.Buffered(1)),
            pl.BlockSpec((tm, d), lambda i: (i, 0)),
            pl.BlockSpec((1, d), lambda i: (0, 0)),
        ],
        out_specs=pl.BlockSpec((tm, d), lambda i: (i, 0)),
        out_shape=jax.ShapeDtypeStruct((t, d), F32),
        compiler_params=_params(("parallel",)),
        name="matmul_norm_res",
    )(a, w, x, g)


PAIRS = GMLP_GROUPS // 2


def _gmlp_out_kernel(*refs, ranges, tm):
    n_parts = len(ranges)
    (u1_ref, v1_ref, mu1_ref, rs1_ref, lng1_ref, lnb1_ref, ws1_ref, bs1_ref,
     u2_ref, v2_ref, mu2_ref, rs2_ref, lng2_ref, lnb2_ref, ws2_ref, bs2_ref,
     wo_ref, g_ref) = refs[:18]
    x_refs = refs[18:18 + n_parts]
    o_ref, a0_scr, a1_scr = refs[18 + n_parts:]
    s = pl.program_id(0)
    q = jnp.maximum(s - 1, 0)
    tile, pair = q // PAIRS, q % PAIRS

    @pl.when(s == 0)
    def _():
        a0_scr[...] = jnp.zeros_like(a0_scr)
        a1_scr[...] = jnp.zeros_like(a1_scr)

    def prep(u_ref, v_ref, mu_ref, rs_ref, lng_ref, lnb_ref, ws_ref, bs_ref):
        vn = ((v_ref[0].astype(F32) - mu_ref[...]) * rs_ref[...] * lng_ref[0]
              + lnb_ref[0]).astype(BF16)
        ws = ws_ref[0]
        bs = bs_ref[0]
        parts = []
        for n in range(tm // BLOCK):
            rows = slice(n * BLOCK, (n + 1) * BLOCK)
            mixed = _dot(ws, vn[rows, :]) + bs
            parts.append((u_ref[0, rows, :].astype(F32) * mixed).astype(BF16))
        return jnp.concatenate(parts, axis=0)

    def step(first, x_ref=None):
        row0 = pl.multiple_of(pair * (2 * GMLP_GDIM), 2 * GMLP_GDIM)
        proj = _dot(a0_scr[...], wo_ref[pl.ds(row0, GMLP_GDIM), :])
        if first:
            o_ref[...] = proj
        else:
            o_ref[...] += proj
        a1_scr[...] = prep(u1_ref, v1_ref, mu1_ref, rs1_ref, lng1_ref, lnb1_ref, ws1_ref, bs1_ref)
        o_ref[...] += _dot(a1_scr[...], wo_ref[pl.ds(row0 + GMLP_GDIM, GMLP_GDIM), :])
        a0_scr[...] = prep(u2_ref, v2_ref, mu2_ref, rs2_ref, lng2_ref, lnb2_ref, ws2_ref, bs2_ref)
        if x_ref is not None:
            o_ref[...] = x_ref[...] + _rms(o_ref[...], g_ref[...])

    pl.when(pair == 0)(functools.partial(step, True))
    pl.when((pair > 0) & (pair < PAIRS - 1))(functools.partial(step, False))
    _for_owner(tile, ranges, x_refs, pair == PAIRS - 1, functools.partial(step, False))


def _gmlp_out(z, mu, rstd, ln_g, ln_b, w_s, b_s, w_out, layer, x_parts, g, *, tm):
    t = sum(p.shape[0] for p in x_parts)
    d = x_parts[0].shape[1]
    n_tiles = t // tm
    last = GMLP_GROUPS * n_tiles - 1

    def proj_tile(s):
        return jnp.maximum(s - 1, 0) // PAIRS

    def prep_specs(pair_of):
        def at(f):
            return lambda s: f(*pair_of(s))
        return [
            pl.BlockSpec((1, tm, GMLP_GDIM), at(lambda i, c: (c, i, 0))),
            pl.BlockSpec((1, tm, GMLP_GDIM), at(lambda i, c: (GMLP_GROUPS + c, i, 0))),
            pl.BlockSpec((tm, 1), at(lambda i, c: (i, 0))),
            pl.BlockSpec((tm, 1), at(lambda i, c: (i, 0))),
            pl.BlockSpec((1, 1, GMLP_GDIM), at(lambda i, c: (c, 0, 0))),
            pl.BlockSpec((1, 1, GMLP_GDIM), at(lambda i, c: (c, 0, 0))),
            pl.BlockSpec((1, BLOCK, BLOCK), at(lambda i, c: (c, 0, 0))),
            pl.BlockSpec((1, BLOCK, 1), at(lambda i, c: (c, 0, 0))),
        ]

    def split(gidx):
        gidx = jnp.clip(gidx, 0, last)
        return gidx // GMLP_GROUPS, gidx % GMLP_GROUPS

    x_specs, ranges = _part_specs(x_parts, tm, proj_tile)
    prep_args = (z, z, mu, rstd, ln_g, ln_b, w_s, b_s)
    return pl.pallas_call(
        functools.partial(_gmlp_out_kernel, ranges=ranges, tm=tm),
        grid=(PAIRS * n_tiles + 1,),
        in_specs=prep_specs(lambda s: split(2 * s - 1)) + prep_specs(lambda s: split(2 * s)) + [
            pl.BlockSpec((None, GMLP_HALF, d), lambda s: (layer, 0, 0),
                         pipeline_mode=pl.Buffered(1)),
            pl.BlockSpec((1, d), lambda s: (0, 0)),
        ] + x_specs,
        out_specs=pl.BlockSpec((tm, d), lambda s: (proj_tile(s), 0)),
        out_shape=jax.ShapeDtypeStruct((t, d), F32),
        scratch_shapes=[
            pltpu.VMEM((tm, GMLP_GDIM), BF16),
            pltpu.VMEM((tm, GMLP_GDIM), BF16),
        ],
        compiler_params=_params(("arbitrary",)),
        name="gmlp_out",
    )(*prep_args, *prep_args, w_out, g, *x_parts)


def _win_attn_kernel(sink_ref, q_ref, kp_ref, kc_ref, kn_ref, vp_ref, vc_ref, vn_ref, bias_ref,
                     o_ref, *, r_blocks, n_steps):
    i = pl.program_id(1)
    kall = jnp.concatenate([kp_ref[0], kc_ref[0], kn_ref[0]], axis=0)
    vall = jnp.concatenate([vp_ref[0], vc_ref[0], vn_ref[0]], axis=0)
    col = lax.broadcasted_iota(jnp.int32, (1, 3 * BLOCK), 1)
    sk = jnp.concatenate(
        [jnp.full((BLOCK, 1), sink_ref[h], F32) for h in range(N_HEADS)], axis=0)
    hrows = KV_GROUP * BLOCK
    for r in range(r_blocks):
        rows = slice(r * BLOCK, (r + 1) * BLOCK)
        oob = None
        if r == 0:
            oob = (col < BLOCK) & (i == 0)
        if r == r_blocks - 1:
            hi = (col >= 2 * BLOCK) & (i == n_steps - 1)
            oob = hi if oob is None else (oob | hi)
        parts = []
        for kh in range(N_KV_HEADS):
            kb = kall[r * BLOCK:(r + 3) * BLOCK, kh * HEAD_DIM:(kh + 1) * HEAD_DIM]
            qs = jnp.concatenate(
                [q_ref[0, rows, h * HEAD_DIM:(h + 1) * HEAD_DIM]
                 for h in range(kh * KV_GROUP, (kh + 1) * KV_GROUP)], axis=0)
            parts.append(
                lax.dot_general(qs, kb, (((1,), (1,)), ((), ())), preferred_element_type=F32))
        s = jnp.concatenate(parts, axis=0) + bias_ref[...]
        if oob is not None:
            s = jnp.where(oob, NEG_INF, s)
        m = jnp.maximum(jnp.max(s, axis=-1, keepdims=True), sk)
        p = jnp.exp2(s - m)
        inv = 1.0 / (jnp.sum(p, axis=-1, keepdims=True) + jnp.exp2(sk - m))
        pb = p.astype(BF16)
        for kh in range(N_KV_HEADS):
            vb = vall[r * BLOCK:(r + 3) * BLOCK, kh * HEAD_DIM:(kh + 1) * HEAD_DIM]
            hsl = slice(kh * hrows, (kh + 1) * hrows)
            o = _dot(pb[hsl], vb) * inv[hsl]
            for gq in range(KV_GROUP):
                h = kh * KV_GROUP + gq
                o_ref[0, rows, h * HEAD_DIM:(h + 1) * HEAD_DIM] = (
                    o[gq * BLOCK:(gq + 1) * BLOCK].astype(o_ref.dtype))


def _win_attention(qkv, sink, bias, *, nseq, r_blocks):
    tq = r_blocks * BLOCK
    n_steps = SEQ // tq
    nb = SEQ // BLOCK
    qkv3 = qkv.reshape(nseq, SEQ, qkv.shape[-1])
    kcol = N_HEADS * HEAD_DIM // (N_KV_HEADS * HEAD_DIM)
    vcol = kcol + 1
    kvw = N_KV_HEADS * HEAD_DIM

    def prev_map(c):
        return lambda b, i: (b, jnp.maximum(i * r_blocks - 1, 0), c)

    def cur_map(c):
        return lambda b, i: (b, i, c)

    def next_map(c):
        return lambda b, i: (b, jnp.minimum((i + 1) * r_blocks, nb - 1), c)

    out = pl.pallas_call(
        functools.partial(_win_attn_kernel, r_blocks=r_blocks, n_steps=n_steps),
        grid=(nseq, n_steps),
        in_specs=[
            pl.BlockSpec(memory_space=pltpu.SMEM),
            pl.BlockSpec((1, tq, N_HEADS * HEAD_DIM), lambda b, i: (b, i, 0)),
            pl.BlockSpec((1, BLOCK, kvw), prev_map(kcol)),
            pl.BlockSpec((1, tq, kvw), cur_map(kcol)),
            pl.BlockSpec((1, BLOCK, kvw), next_map(kcol)),
            pl.BlockSpec((1, BLOCK, kvw), prev_map(vcol)),
            pl.BlockSpec((1, tq, kvw), cur_map(vcol)),
            pl.BlockSpec((1, BLOCK, kvw), next_map(vcol)),
            pl.BlockSpec((N_HEADS * BLOCK, 3 * BLOCK), lambda b, i: (0, 0)),
        ],
        out_specs=pl.BlockSpec((1, tq, N_HEADS * HEAD_DIM), lambda b, i: (b, i, 0)),
        out_shape=jax.ShapeDtypeStruct((nseq, SEQ, N_HEADS * HEAD_DIM), BF16),
        compiler_params=_params(("parallel", "parallel")),
        name="win_attention",
    )(sink, qkv3, qkv3, qkv3, qkv3, qkv3, qkv3, qkv3, bias)
    return out.reshape(nseq * SEQ, N_HEADS * HEAD_DIM)


def _t5_bucket(rel):
    half = N_BUCKETS // 2
    exact = half // 2
    n = np.abs(rel)
    large = exact + (np.log(np.maximum(n, 1) / exact) / np.log(MAX_DISTANCE / exact)
                     * (half - exact)).astype(np.int32)
    large = np.minimum(large, half - 1)
    return (np.where(rel > 0, half, 0) + np.where(n < exact, n, large)).astype(np.int32)


def _band_bias(rel_table):
    rel = np.arange(3 * BLOCK)[None, :] - BLOCK - np.arange(BLOCK)[:, None]
    onehot = (_t5_bucket(rel).reshape(1, -1) == np.arange(N_BUCKETS)[:, None]).astype(np.float32)
    bias = jnp.einsum("bh,bn->hn", rel_table.astype(F32) * LOG2E, onehot,
                      precision=lax.Precision.HIGHEST)
    bias = jnp.where((np.abs(rel) <= WINDOW).reshape(1, -1), bias, NEG_INF)
    return bias.reshape(N_HEADS * BLOCK, 3 * BLOCK)


def _kv_kernel(mem_ref, g_ref, w_ref, o_ref):
    h = _rms(mem_ref[0], g_ref[0]).astype(BF16)
    o_ref[0, 0] = _dot(h, w_ref[0]).astype(o_ref.dtype)


def _kv_proj(mem, g, w_kv):
    nseq = mem.shape[0]
    return pl.pallas_call(
        _kv_kernel,
        grid=(DEPTH, nseq),
        in_specs=[
            pl.BlockSpec((1, MEM_LEN, D_MODEL), lambda l, b: (b, 0, 0)),
            pl.BlockSpec((1, 1, D_MODEL), lambda l, b: (l, 0, 0)),
            pl.BlockSpec((1, D_MODEL, 2 * MEM_WIDTH), lambda l, b: (l, 0, 0)),
        ],
        out_specs=pl.BlockSpec((1, 1, MEM_LEN, 2 * MEM_WIDTH), lambda l, b: (l, b, 0, 0)),
        out_shape=jax.ShapeDtypeStruct((DEPTH, nseq, MEM_LEN, 2 * MEM_WIDTH), BF16),
        compiler_params=_params(("parallel", "parallel")),
        name="kv_proj",
    )(mem, g, w_kv)


def _mem_attn_kernel(x_ref, gpre_ref, gpost_ref, wq_ref, kv_ref, wo_ref, o_ref):
    x = x_ref[...]
    h = _rms(x, gpre_ref[...]).astype(BF16)
    q = (_dot(h, wq_ref[...]) * (HEAD_DIM ** -0.5)).astype(BF16)
    outs = []
    for hd in range(MEM_HEADS):
        sl = slice(hd * HEAD_DIM, (hd + 1) * HEAD_DIM)
        k = kv_ref[0, :, sl]
        v = kv_ref[0, :, MEM_WIDTH + hd * HEAD_DIM:MEM_WIDTH + (hd + 1) * HEAD_DIM]
        s = lax.dot_general(q[:, sl], k, (((1,), (1,)), ((), ())), preferred_element_type=F32)
        m = jnp.max(s, axis=-1, keepdims=True)
        p = jnp.exp(s - m)
        denom = jnp.sum(p, axis=-1, keepdims=True)
        outs.append((_dot(p.astype(BF16), v) / denom).astype(BF16))
    y = _dot(jnp.concatenate(outs, axis=-1), wo_ref[...])
    o_ref[...] = x + _rms(y, gpost_ref[...])


def _mem_attention(x, gpre, gpost, w_q, kv, w_o, layer, *, tm):
    t, d = x.shape
    per_seq = SEQ // tm
    return pl.pallas_call(
        _mem_attn_kernel,
        grid=(t // tm,),
        in_specs=[
            pl.BlockSpec((tm, d), lambda i: (i, 0)),
            pl.BlockSpec((1, d), lambda i: (0, 0)),
            pl.BlockSpec((1, d), lambda i: (0, 0)),
            pl.BlockSpec((None, d, MEM_WIDTH), lambda i: (layer, 0, 0)),
            pl.BlockSpec((None, 1, MEM_LEN, 2 * MEM_WIDTH), lambda i: (layer, i // per_seq, 0, 0)),
            pl.BlockSpec((None, MEM_WIDTH, d), lambda i: (layer, 0, 0)),
        ],
        out_specs=pl.BlockSpec((tm, d), lambda i: (i, 0)),
        out_shape=jax.ShapeDtypeStruct((t, d), F32),
        compiler_params=_params(("parallel",)),
        name="mem_attention",
    )(x, gpre, gpost, w_q, kv, w_o)


def _ffn_kernel(*refs, n_chunks, cast_next):
    if cast_next:
        (x_ref, gpre_ref, gpost_ref, wg_ref, wu_ref, wd_ref, ngu_ref, nd_ref,
         o_ref, ngu_out, nd_out, h_scr) = refs
    else:
        x_ref, gpre_ref, gpost_ref, wg_ref, wu_ref, wd_ref, o_ref, h_scr = refs
    j = pl.program_id(1)

    def chunk(first, last):
        if cast_next:
            ngu_out[...] = ngu_ref[...].astype(BF16)
            nd_out[...] = nd_ref[...].astype(BF16)
        if first:
            h_scr[...] = _rms(x_ref[...], gpre_ref[...]).astype(BF16)
        h = h_scr[...]
        gate = _dot(h, wg_ref[...])
        up = _dot(h, wu_ref[...])
        a = (gate * _sigmoid(gate) * up).astype(BF16)
        if first:
            o_ref[...] = _dot(a, wd_ref[...])
        else:
            o_ref[...] += _dot(a, wd_ref[...])
        if last:
            o_ref[...] = x_ref[...] + _rms(o_ref[...], gpost_ref[...])

    pl.when(j == 0)(functools.partial(chunk, True, False))
    pl.when((j > 0) & (j < n_chunks - 1))(functools.partial(chunk, False, False))
    pl.when(j == n_chunks - 1)(functools.partial(chunk, False, True))


def _ffn(x, gpre, gpost, w_gate_up, w_down, layer, *, tm, tn, row0=0, rows=None, next_f32=None):
    d = x.shape[1]
    t = x.shape[0] if rows is None else rows
    tile0 = row0 // tm
    n_chunks = D_FF // tn
    n_tiles = t // tm
    in_specs_next, out_specs_next, out_shape_next, args_next = [], [], [], []
    if next_f32 is not None:
        ngu, nd, nl = next_f32
        gu_blk = (d // n_tiles, 2 * D_FF // n_chunks)
        d_blk = (D_FF // (n_tiles * n_chunks), d)
        assert gu_blk[0] * n_tiles == d and gu_blk[1] * n_chunks == 2 * D_FF
        assert d_blk[0] * n_tiles * n_chunks == D_FF and d_blk[0] % (2 * SUBLANES) == 0
        in_specs_next = [
            pl.BlockSpec((None,) + gu_blk, lambda i, j: (nl, i, j)),
            pl.BlockSpec((None,) + d_blk, lambda i, j: (nl, i * n_chunks + j, 0)),
        ]
        out_specs_next = [
            pl.BlockSpec(gu_blk, lambda i, j: (i, j)),
            pl.BlockSpec(d_blk, lambda i, j: (i * n_chunks + j, 0)),
        ]
        out_shape_next = [
            jax.ShapeDtypeStruct((d, 2 * D_FF), BF16),
            jax.ShapeDtypeStruct((D_FF, d), BF16),
        ]
        args_next = [ngu, nd]
    x_bytes = tm * d * 4
    fixed = 2 * x_bytes + tm * d * 2 + 2 * 3 * d * tn * 2
    temps = 2 * tm * tn * 4 + tm * tn * 2 + tm * d * 4
    x_mode = None if fixed + 2 * x_bytes + temps <= VMEM_LIMIT else pl.Buffered(1)
    out = pl.pallas_call(
        functools.partial(_ffn_kernel, n_chunks=n_chunks, cast_next=next_f32 is not None),
        grid=(n_tiles, n_chunks),
        in_specs=[
            pl.BlockSpec((tm, d), lambda i, j: (i + tile0, 0), pipeline_mode=x_mode),
            pl.BlockSpec((1, d), lambda i, j: (0, 0)),
            pl.BlockSpec((1, d), lambda i, j: (0, 0)),
            pl.BlockSpec((None, d, tn), lambda i, j: (layer, 0, j)),
            pl.BlockSpec((None, d, tn), lambda i, j: (layer, 0, j + n_chunks)),
            pl.BlockSpec((None, tn, d), lambda i, j: (layer, j, 0)),
        ] + in_specs_next,
        out_specs=[pl.BlockSpec((tm, d), lambda i, j: (i, 0))] + out_specs_next,
        out_shape=[jax.ShapeDtypeStruct((t, d), F32)] + out_shape_next,
        scratch_shapes=[pltpu.VMEM((tm, d), BF16)],
        compiler_params=_params(("parallel", "arbitrary")),
        name="ffn",
    )(x, gpre, gpost, w_gate_up, w_gate_up, w_down, *args_next)
    if next_f32 is None:
        return out[0]
    return out[0], out[1][None], out[2][None]


def _cast_kernel(w_ref, o_ref):
    o_ref[...] = w_ref[...].astype(o_ref.dtype)


def _cast_layer(w, layer, *, row_blocks):
    _, rows, cols = w.shape
    br = rows // row_blocks
    out = pl.pallas_call(
        _cast_kernel,
        grid=(row_blocks,),
        in_specs=[pl.BlockSpec((None, br, cols), lambda i: (layer, i, 0))],
        out_specs=pl.BlockSpec((br, cols), lambda i: (i, 0)),
        out_shape=jax.ShapeDtypeStruct((rows, cols), BF16),
        compiler_params=_params(("parallel",)),
        name="cast_layer",
    )(w)
    return out[None]


N_CT = D_RNN // LANES
HALO = 2 * SUBLANES


def _gate_windows():
    lo_hi = []
    for j in range(N_CT):
        h0 = (LANES * j) // RNN_BDIM
        h1 = (LANES * j + LANES - 1) // RNN_BDIM
        lo = (RNN_BDIM * h0) // LANES * LANES
        hi = -(-(RNN_BDIM * (h1 + 1)) // LANES) * LANES
        lo_hi.append((lo, hi))
    width = max(hi - lo for lo, hi in lo_hi)
    starts = tuple(min(lo, D_RNN - width) for lo, _ in lo_hi)
    return starts, width


GATE_STARTS, GATE_WIDTH = _gate_windows()


def _gate_weights(w_a, w_x):
    def tile(w, j):
        st = GATE_STARTS[j]
        pieces = []
        for h in range((LANES * j) // RNN_BDIM, (LANES * j + LANES - 1) // RNN_BDIM + 1):
            c0 = max(LANES * j, RNN_BDIM * h) - RNN_BDIM * h
            c1 = min(LANES * (j + 1), RNN_BDIM * (h + 1)) - RNN_BDIM * h
            top = RNN_BDIM * h - st
            pieces.append(jnp.pad(w[h, :, c0:c1], ((top, GATE_WIDTH - RNN_BDIM - top), (0, 0))))
        return jnp.concatenate(pieces, axis=1)

    wa, wx = (w_a * 0.5).astype(BF16), (w_x * 0.5).astype(BF16)
    return jnp.stack([jnp.concatenate([tile(wa, j), tile(wx, j)], axis=1) for j in range(N_CT)])


def _scan_kernel(*refs, ts, n_tiles, nseq, reverse):
    if reverse:
        (xp_ref, xc_ref, xn_ref, cw_ref, cb_ref, wp_ref, ba_ref, bx_ref, lam_ref,
         hsf_ref, gate_ref, o_ref, conv_scr, convb_scr, a_scr, u_scr, h_scr) = refs
    else:
        (xp_ref, xc_ref, xn_ref, cw_ref, cb_ref, wp_ref, ba_ref, bx_ref, lam_ref,
         o_ref, conv_scr, convb_scr, a_scr, u_scr, h_scr) = refs
    i = pl.program_id(0)
    tile = (n_tiles - 1 - i) if reverse else i

    @pl.when(i == 0)
    def _():
        h_scr[...] = jnp.zeros_like(h_scr)
        a_scr[...] = jnp.zeros_like(a_scr)
        u_scr[...] = jnp.zeros_like(u_scr)

    cw = cw_ref[...]
    cb = cb_ref[...]
    shifts = [k - CONV_LEFT for k in range(CONV_WIDTH) if k != CONV_LEFT]
    rr = lax.broadcasted_iota(jnp.int32, (ts, ts + 2 * HALO), 0)
    cc = lax.broadcasted_iota(jnp.int32, (ts, ts + 2 * HALO), 1)
    sel = jnp.concatenate(
        [jnp.where(cc == rr + HALO + s, 1.0, 0.0).astype(BF16) for s in shifts], axis=0)
    def conv_columns(cols):
        for b in range(nseq):
            prev = xp_ref[b, :, cols]
            nxt = xn_ref[b, :, cols]
            prev = jnp.where(tile == 0, jnp.zeros_like(prev), prev)
            nxt = jnp.where(tile == n_tiles - 1, jnp.zeros_like(nxt), nxt)
            cur = xc_ref[b, :, cols]
            shifted = _dot(sel, jnp.concatenate([prev, cur, nxt], axis=0))
            acc = cb[:, cols] + cur.astype(F32) * cw[CONV_LEFT:CONV_LEFT + 1, cols]
            for n, s in enumerate(shifts):
                k = s + CONV_LEFT
                acc = acc + shifted[n * ts:(n + 1) * ts] * cw[k:k + 1, cols]
            conv_scr[b * ts:(b + 1) * ts, cols] = acc
            convb_scr[b * ts:(b + 1) * ts, cols] = acc.astype(BF16)

    def gate_tile(j):
        st = GATE_STARTS[j]
        sl = slice(LANES * j, LANES * (j + 1))
        pre = _dot(convb_scr[:, st:st + GATE_WIDTH], wp_ref[j])
        tr = jnp.tanh(pre[:, :LANES] + ba_ref[:, sl])
        ti = jnp.tanh(pre[:, LANES:] + bx_ref[:, sl])
        z = -lam_ref[:, sl]
        softplus = jnp.maximum(z, 0.0) + jnp.log(1.0 + jnp.exp(-jnp.abs(z)))
        half = (-0.5 * LRU_C * LOG2E) * softplus
        a = jnp.exp2(half + half * tr)
        y = 1.0 - a * a
        root = jnp.where(y > 0.0, y * lax.rsqrt(y), 0.0)
        hx = 0.5 * conv_scr[:, sl]
        u = root * (hx + hx * ti)
        for b in range(nseq):
            a_scr[j, pl.ds(b, ts, stride=SUBLANES), :] = a[b * ts:(b + 1) * ts]
            u_scr[j, pl.ds(b, ts, stride=SUBLANES), :] = u[b * ts:(b + 1) * ts]

    conv_columns(slice(None))
    for j in range(N_CT):
        gate_tile(j)

    def step(k, h):
        t = (ts - 1 - k) if reverse else k
        row = pl.multiple_of(t * SUBLANES, SUBLANES)
        h = a_scr[:, pl.ds(row, SUBLANES), :] * h + u_scr[:, pl.ds(row, SUBLANES), :]
        u_scr[:, pl.ds(row, SUBLANES), :] = h
        return h

    h_scr[...] = lax.fori_loop(0, ts, step, h_scr[...], unroll=8)

    for b in range(nseq):
        for j in range(N_CT):
            sl = slice(LANES * j, LANES * (j + 1))
            y = u_scr[j, pl.ds(b, ts, stride=SUBLANES), :]
            if reverse:
                y = gate_ref[b, :, sl].astype(F32) * (y + hsf_ref[b, :, sl].astype(F32))
            o_ref[b, :, sl] = y.astype(o_ref.dtype)


def _rglru_scan(y_in, conv_w, conv_b, wp, b_a, b_x, lam, hs_fwd, *, ts, nseq, reverse):
    n_tiles = SEQ // ts
    per_tile = ts // HALO

    def tmap(i):
        return (n_tiles - 1 - i) if reverse else i

    in_specs = [
        pl.BlockSpec((nseq, HALO, D_RNN), lambda i: (0, jnp.maximum(tmap(i) * per_tile - 1, 0), 1)),
        pl.BlockSpec((nseq, ts, D_RNN), lambda i: (0, tmap(i), 1)),
        pl.BlockSpec((nseq, HALO, D_RNN),
                     lambda i: (0, jnp.minimum((tmap(i) + 1) * per_tile, SEQ // HALO - 1), 1)),
        pl.BlockSpec((CONV_WIDTH, D_RNN), lambda i: (0, 0)),
        pl.BlockSpec((1, D_RNN), lambda i: (0, 0)),
        pl.BlockSpec((N_CT, GATE_WIDTH, 2 * LANES), lambda i: (0, 0, 0)),
        pl.BlockSpec((1, D_RNN), lambda i: (0, 0)),
        pl.BlockSpec((1, D_RNN), lambda i: (0, 0)),
        pl.BlockSpec((1, D_RNN), lambda i: (0, 0)),
    ]
    args = [y_in, y_in, y_in, conv_w, conv_b, wp, b_a, b_x, lam]
    if reverse:
        in_specs += [
            pl.BlockSpec((nseq, ts, D_RNN), lambda i: (0, tmap(i), 0)),
            pl.BlockSpec((nseq, ts, D_RNN), lambda i: (0, tmap(i), 0)),
        ]
        args += [hs_fwd, y_in]
    return pl.pallas_call(
        functools.partial(_scan_kernel, ts=ts, n_tiles=n_tiles, nseq=nseq, reverse=reverse),
        grid=(n_tiles,),
        in_specs=in_specs,
        out_specs=pl.BlockSpec((nseq, ts, D_RNN), lambda i: (0, tmap(i), 0)),
        out_shape=jax.ShapeDtypeStruct((nseq, SEQ, D_RNN), BF16),
        scratch_shapes=[
            pltpu.VMEM((nseq * ts, D_RNN), F32),
            pltpu.VMEM((nseq * ts, D_RNN), BF16),
            pltpu.VMEM((N_CT, ts * SUBLANES, LANES), F32),
            pltpu.VMEM((N_CT, ts * SUBLANES, LANES), F32),
            pltpu.VMEM((N_CT, SUBLANES, LANES), F32),
        ],
        compiler_params=_params(("arbitrary",)),
        name="rglru_scan_bwd" if reverse else "rglru_scan_fwd",
    )(*args)


def kernel(x_prompt, x_sample, mem_prompt, mem_sample, norms, rel_bias, a_w_in, a_ln_g, a_ln_b, a_w_s, a_b_s, a_w_out, b_w_qkv, b_sink, b_w_o, c_w_in, c_conv_w, c_conv_b, c_w_a, c_b_a, c_w_x, c_b_x, c_lam, c_w_out, m_w_q, m_w_kv, m_w_o, f_w_gate_up, f_w_down):
    n_prompt = x_prompt.shape[0]
    nseq = n_prompt + x_sample.shape[0]
    assert nseq <= SUBLANES
    t = nseq * SEQ
    t_prompt = n_prompt * SEQ
    x = None
    x_parts = [x_prompt.reshape(t_prompt, D_MODEL), x_sample.reshape(t - t_prompt, D_MODEL)]
    mem = jnp.concatenate([mem_prompt, mem_sample], axis=0)

    def gain(i, k):
        return norms[i, k].reshape(1, D_MODEL)

    a_w_in_b, a_w_out_b = a_w_in.astype(BF16), a_w_out.astype(BF16)
    q_cols = N_HEADS * HEAD_DIM
    b_w_qkv_b = jnp.concatenate(
        [b_w_qkv[:, :, :q_cols] * (HEAD_DIM ** -0.5 * LOG2E), b_w_qkv[:, :, q_cols:]],
        axis=-1).astype(BF16)
    b_w_o_b = b_w_o.astype(BF16)
    c_w_in_b, c_w_out_b = c_w_in.astype(BF16), c_w_out.astype(BF16)
    m_w_q_b, m_w_o_b = m_w_q.astype(BF16), m_w_o.astype(BF16)
    f_gu_b = _cast_layer(f_w_gate_up, 0, row_blocks=16)
    f_d_b = _cast_layer(f_w_down, 0, row_blocks=16)

    kv_all = _kv_proj(mem, norms[:, 4].reshape(DEPTH, 1, D_MODEL), m_w_kv.astype(BF16))

    for i in range(DEPTH):
        kind, j = i % N_MIXERS, i // N_MIXERS
        if kind == 0:
            parts = x_parts if x is None else [x]
            z, mu, rstd = _gmlp_in(parts, gain(i, 0), a_w_in_b, j, tm=1024,
                                   tn=(4 if len(parts) == 1 else 2) * GMLP_GDIM)
            x = _gmlp_out(z, mu, rstd, a_ln_g[j].reshape(GMLP_GROUPS, 1, GMLP_GDIM),
                          a_ln_b[j].reshape(GMLP_GROUPS, 1, GMLP_GDIM),
                          a_w_s[j].astype(BF16), a_b_s[j].reshape(GMLP_GROUPS, BLOCK, 1),
                          a_w_out_b, j, parts, gain(i, 1), tm=512)
        elif kind == 1:
            qkv = _norm_matmul(x, gain(i, 0), b_w_qkv_b, j, tm=1024, tn=1024)
            o = _win_attention(qkv, b_sink[j] * LOG2E, _band_bias(rel_bias), nseq=nseq, r_blocks=4)
            x = _matmul_norm_res(o, b_w_o_b, j, x, gain(i, 1), tm=512)
        else:
            y_in = _norm_matmul(x, gain(i, 0), c_w_in_b, j, tm=1024, tn=D_RNN // 2, n_gelu=2)
            y_in = y_in.reshape(nseq, SEQ, 2 * D_RNN)
            scan_args = (c_conv_w[j], c_conv_b[j].reshape(1, D_RNN))
            hs = None
            for d in range(2):
                hs = _rglru_scan(y_in, *scan_args, _gate_weights(c_w_a[j, d], c_w_x[j, d]),
                                 (c_b_a[j, d] * 0.5).reshape(1, D_RNN),
                                 (c_b_x[j, d] * 0.5).reshape(1, D_RNN),
                                 c_lam[j, d].reshape(1, D_RNN), hs, ts=64, nseq=nseq,
                                 reverse=(d == 1))
            x = _matmul_norm_res(hs.reshape(t, D_RNN), c_w_out_b, j, x, gain(i, 1), tm=512)
        x = _mem_attention(x, gain(i, 2), gain(i, 3), m_w_q_b, kv_all, m_w_o_b, i, tm=1024)
        ffn = functools.partial(_ffn, x, gain(i, 5), gain(i, 6), f_gu_b, f_d_b, 0)
        if i < DEPTH - 1:
            x, f_gu_b, f_d_b = ffn(tm=768, tn=512, next_f32=(f_w_gate_up, f_w_down, i + 1))

    y_prompt = ffn(tm=1024, tn=256, row0=0, rows=t_prompt).reshape(n_prompt, SEQ, D_MODEL)
    y_sample = ffn(tm=1024, tn=256, row0=t_prompt, rows=t - t_prompt).reshape(
        nseq - n_prompt, SEQ, D_MODEL)
    return (y_prompt, y_sample)
```

```python
import functools

import numpy as np
import jax
import jax.numpy as jnp
from jax import lax
from jax.experimental import pallas as pl
from jax.experimental.pallas import tpu as pltpu

D_MODEL = 2048
SEQ = 4096
DEPTH = 4
N_MIXERS = 3
BLOCK = 128
HEAD_DIM = 128
EPS = 1e-6
NEG_INF = -1e30
GMLP_HALF = 2 * D_MODEL
GMLP_GROUPS = 8
GMLP_GDIM = GMLP_HALF // GMLP_GROUPS
N_HEADS = D_MODEL // HEAD_DIM
N_KV_HEADS = 4
KV_GROUP = N_HEADS // N_KV_HEADS
WINDOW = 128
N_BUCKETS = 32
MAX_DISTANCE = 128
D_RNN = 2816
RNN_BLOCKS = 16
RNN_BDIM = D_RNN // RNN_BLOCKS
CONV_WIDTH = 4
CONV_LEFT = 2
LRU_C = 8.0
MEM_LEN = 256
MEM_HEADS = 4
MEM_WIDTH = MEM_HEADS * HEAD_DIM
D_FF = 5632

LANES = 128
SUBLANES = 8
VMEM_LIMIT = 56 * 1024 * 1024

F32 = jnp.float32
BF16 = jnp.bfloat16
LOG2E = float(np.log2(np.e))


def _params(sem):
    return pltpu.CompilerParams(dimension_semantics=sem, vmem_limit_bytes=VMEM_LIMIT)


def _rms(x, g):
    ms = jnp.mean(x * x, axis=-1, keepdims=True)
    return x * lax.rsqrt(ms + EPS) * g


def _gelu(x):
    k1 = float(-2.0 * np.sqrt(2.0 / np.pi) * LOG2E)
    return x / (1.0 + jnp.exp2(x * (k1 + (k1 * 0.044715) * (x * x))))


def _sigmoid(x):
    return 1.0 / (1.0 + jnp.exp(-x))


def _dot(a, b):
    return jnp.dot(a, b, preferred_element_type=F32)


def _norm_mm_kernel(x_ref, g_ref, w_ref, o_ref, h_scr, *, n_gelu):
    j = pl.program_id(1)

    def emit(act, first):
        if first:
            h_scr[...] = _rms(x_ref[...], g_ref[...]).astype(BF16)
        z = _dot(h_scr[...], w_ref[...])
        o_ref[...] = (_gelu(z) if act else z).astype(o_ref.dtype)

    pl.when(j == 0)(functools.partial(emit, n_gelu > 0, True))
    if n_gelu > 1:
        pl.when((j > 0) & (j < n_gelu))(functools.partial(emit, True, False))
    pl.when(j >= max(n_gelu, 1))(functools.partial(emit, False, False))


def _norm_matmul(x, g, w, layer, *, tm, tn, n_gelu=0):
    t, d = x.shape
    n = w.shape[2]
    return pl.pallas_call(
        functools.partial(_norm_mm_kernel, n_gelu=n_gelu),
        grid=(t // tm, n // tn),
        in_specs=[
            pl.BlockSpec((tm, d), lambda i, j: (i, 0)),
            pl.BlockSpec((1, d), lambda i, j: (0, 0)),
            pl.BlockSpec((None, d, tn), lambda i, j: (layer, 0, j)),
        ],
        out_specs=pl.BlockSpec((tm, tn), lambda i, j: (i, j)),
        out_shape=jax.ShapeDtypeStruct((t, n), BF16),
        scratch_shapes=[pltpu.VMEM((tm, d), BF16)],
        compiler_params=_params(("parallel", "arbitrary")),
        name="norm_matmul",
    )(x, g, w)


def _part_specs(parts, tm, tile_of):
    specs, ranges, lo = [], [], 0
    for p in parts:
        n = p.shape[0] // tm
        specs.append(pl.BlockSpec(
            (tm, p.shape[1]),
            lambda *idx, lo=lo, n=n: (jnp.clip(tile_of(*idx) - lo, 0, n - 1), 0)))
        ranges.append((lo, lo + n))
        lo += n
    return specs, tuple(ranges)


def _for_owner(i, ranges, refs, cond, fn):
    for (lo, hi), ref in zip(ranges, refs):
        pl.when(cond & (i >= lo) & (i < hi))(functools.partial(fn, ref))


MXU_COLS = 256


def _gmlp_in_kernel(*refs, ranges, n_col, tn):
    n_parts = len(ranges)
    x_refs = refs[:n_parts]
    g_ref, w_ref, z_ref, mu_ref, rs_ref, h_scr, s1_scr, s2_scr = refs[n_parts:]
    i, j = pl.program_id(0), pl.program_id(1)
    tm = h_scr.shape[0]

    def step(is_v, x_ref=None):
        if x_ref is not None:
            h_scr[...] = _rms(x_ref[...], g_ref[...]).astype(BF16)
            s1_scr[...] = jnp.zeros_like(s1_scr)
            s2_scr[...] = jnp.zeros_like(s2_scr)
        p1 = jnp.zeros((tm, LANES), F32)
        p2 = jnp.zeros((tm, LANES), F32)
        per_tile = GMLP_GDIM // MXU_COLS
        for c in range(tn // MXU_COLS):
            cols = slice(c * MXU_COLS, (c + 1) * MXU_COLS)
            zc = _gelu(_dot(h_scr[...], w_ref[:, cols]))
            lo = (c % per_tile) * MXU_COLS
            z_ref[c // per_tile, :, lo:lo + MXU_COLS] = zc.astype(z_ref.dtype)
            if is_v:
                for k in range(MXU_COLS // LANES):
                    zk = zc[:, k * LANES:(k + 1) * LANES]
                    p1 = p1 + zk
                    p2 = p2 + zk * zk
        if is_v:
            s1_scr[...] += p1
            s2_scr[...] += p2

    _for_owner(i, ranges, x_refs, j == 0, functools.partial(step, False))
    pl.when((j > 0) & (j < n_col // 2))(functools.partial(step, False))
    pl.when(j >= n_col // 2)(functools.partial(step, True))

    @pl.when(j == n_col - 1)
    def _():
        mu = jnp.sum(s1_scr[...], axis=-1, keepdims=True) * (1.0 / GMLP_HALF)
        msq = jnp.sum(s2_scr[...], axis=-1, keepdims=True) * (1.0 / GMLP_HALF)
        mu_ref[...] = mu
        rs_ref[...] = lax.rsqrt(jnp.maximum(msq - mu * mu, 0.0) + EPS)


def _gmlp_in(x_parts, g, w, layer, *, tm, tn):
    t = sum(p.shape[0] for p in x_parts)
    d = x_parts[0].shape[1]
    n_col = 2 * GMLP_HALF // tn
    sub = tn // GMLP_GDIM
    x_specs, ranges = _part_specs(x_parts, tm, lambda i, j: i)
    return pl.pallas_call(
        functools.partial(_gmlp_in_kernel, ranges=ranges, n_col=n_col, tn=tn),
        grid=(t // tm, n_col),
        in_specs=x_specs + [
            pl.BlockSpec((1, d), lambda i, j: (0, 0)),
            pl.BlockSpec((None, d, tn), lambda i, j: (layer, 0, j)),
        ],
        out_specs=[
            pl.BlockSpec((sub, tm, GMLP_GDIM), lambda i, j: (j, i, 0)),
            pl.BlockSpec((tm, 1), lambda i, j: (i, 0)),
            pl.BlockSpec((tm, 1), lambda i, j: (i, 0)),
        ],
        out_shape=[
            jax.ShapeDtypeStruct((2 * GMLP_GROUPS, t, GMLP_GDIM), BF16),
            jax.ShapeDtypeStruct((t, 1), F32),
            jax.ShapeDtypeStruct((t, 1), F32),
        ],
        scratch_shapes=[
            pltpu.VMEM((tm, d), BF16),
            pltpu.VMEM((tm, LANES), F32),
            pltpu.VMEM((tm, LANES), F32),
        ],
        compiler_params=_params(("parallel", "arbitrary")),
        name="gmlp_in",
    )(*x_parts, g, w)


def _mm_norm_res_kernel(a_ref, w_ref, x_ref, g_ref, o_ref):
    y = _dot(a_ref[...], w_ref[...])
    o_ref[...] = x_ref[...] + _rms(y, g_ref[...])


def _matmul_norm_res(a, w, layer, x, g, *, tm):
    t, k = a.shape
    d = w.shape[2]
    return pl.pallas_call(
        _mm_norm_res_kernel,
        grid=(t // tm,),
        in_specs=[
            pl.BlockSpec((tm, k), lambda i: (i, 0)),
            pl.BlockSpec((None, k, d), lambda i: (layer, 0, 0), pipeline_mode=pl.Buffered(1)),
            pl.BlockSpec((tm, d), lambda i: (i, 0)),
            pl.BlockSpec((1, d), lambda i: (0, 0)),
        ],
        out_specs=pl.BlockSpec((tm, d), lambda i: (i, 0)),
        out_shape=jax.ShapeDtypeStruct((t, d), F32),
        compiler_params=_params(("parallel",)),
        name="matmul_norm_res",
    )(a, w, x, g)


PAIRS = GMLP_GROUPS // 2


def _gmlp_out_kernel(*refs, ranges, tm):
    n_parts = len(ranges)
    (u1_ref, v1_ref, mu1_ref, rs1_ref, lng1_ref, lnb1_ref, ws1_ref, bs1_ref,
     u2_ref, v2_ref, mu2_ref, rs2_ref, lng2_ref, lnb2_ref, ws2_ref, bs2_ref,
     wo_ref, g_ref) = refs[:18]
    x_refs = refs[18:18 + n_parts]
    o_ref, a0_scr, a1_scr = refs[18 + n_parts:]
    s = pl.program_id(0)
    q = jnp.maximum(s - 1, 0)
    tile, pair = q // PAIRS, q % PAIRS

    @pl.when(s == 0)
    def _():
        a0_scr[...] = jnp.zeros_like(a0_scr)
        a1_scr[...] = jnp.zeros_like(a1_scr)

    def prep(u_ref, v_ref, mu_ref, rs_ref, lng_ref, lnb_ref, ws_ref, bs_ref):
        vn = ((v_ref[0].astype(F32) - mu_ref[...]) * rs_ref[...] * lng_ref[0]
              + lnb_ref[0]).astype(BF16)
        ws = ws_ref[0]
        bs = bs_ref[0]
        parts = []
        for n in range(tm // BLOCK):
            rows = slice(n * BLOCK, (n + 1) * BLOCK)
            mixed = _dot(ws, vn[rows, :]) + bs
            parts.append((u_ref[0, rows, :].astype(F32) * mixed).astype(BF16))
        return jnp.concatenate(parts, axis=0)

    def step(first, x_ref=None):
        row0 = pl.multiple_of(pair * (2 * GMLP_GDIM), 2 * GMLP_GDIM)
        proj = _dot(a0_scr[...], wo_ref[pl.ds(row0, GMLP_GDIM), :])
        if first:
            o_ref[...] = proj
        else:
            o_ref[...] += proj
        a1_scr[...] = prep(u1_ref, v1_ref, mu1_ref, rs1_ref, lng1_ref, lnb1_ref, ws1_ref, bs1_ref)
        o_ref[...] += _dot(a1_scr[...], wo_ref[pl.ds(row0 + GMLP_GDIM, GMLP_GDIM), :])
        a0_scr[...] = prep(u2_ref, v2_ref, mu2_ref, rs2_ref, lng2_ref, lnb2_ref, ws2_ref, bs2_ref)
        if x_ref is not None:
            o_ref[...] = x_ref[...] + _rms(o_ref[...], g_ref[...])

    pl.when(pair == 0)(functools.partial(step, True))
    pl.when((pair > 0) & (pair < PAIRS - 1))(functools.partial(step, False))
    _for_owner(tile, ranges, x_refs, pair == PAIRS - 1, functools.partial(step, False))


def _gmlp_out(z, mu, rstd, ln_g, ln_b, w_s, b_s, w_out, layer, x_parts, g, *, tm):
    t = sum(p.shape[0] for p in x_parts)
    d = x_parts[0].shape[1]
    n_tiles = t // tm
    last = GMLP_GROUPS * n_tiles - 1

    def proj_tile(s):
        return jnp.maximum(s - 1, 0) // PAIRS

    def prep_specs(pair_of):
        def at(f):
            return lambda s: f(*pair_of(s))
        return [
            pl.BlockSpec((1, tm, GMLP_GDIM), at(lambda i, c: (c, i, 0))),
            pl.BlockSpec((1, tm, GMLP_GDIM), at(lambda i, c: (GMLP_GROUPS + c, i, 0))),
            pl.BlockSpec((tm, 1), at(lambda i, c: (i, 0))),
            pl.BlockSpec((tm, 1), at(lambda i, c: (i, 0))),
            pl.BlockSpec((1, 1, GMLP_GDIM), at(lambda i, c: (c, 0, 0))),
            pl.BlockSpec((1, 1, GMLP_GDIM), at(lambda i, c: (c, 0, 0))),
            pl.BlockSpec((1, BLOCK, BLOCK), at(lambda i, c: (c, 0, 0))),
            pl.BlockSpec((1, BLOCK, 1), at(lambda i, c: (c, 0, 0))),
        ]

    def split(gidx):
        gidx = jnp.clip(gidx, 0, last)
        return gidx // GMLP_GROUPS, gidx % GMLP_GROUPS

    x_specs, ranges = _part_specs(x_parts, tm, proj_tile)
    prep_args = (z, z, mu, rstd, ln_g, ln_b, w_s, b_s)
    return pl.pallas_call(
        functools.partial(_gmlp_out_kernel, ranges=ranges, tm=tm),
        grid=(PAIRS * n_tiles + 1,),
        in_specs=prep_specs(lambda s: split(2 * s - 1)) + prep_specs(lambda s: split(2 * s)) + [
            pl.BlockSpec((None, GMLP_HALF, d), lambda s: (layer, 0, 0),
                         pipeline_mode=pl.Buffered(1)),
            pl.BlockSpec((1, d), lambda s: (0, 0)),
        ] + x_specs,
        out_specs=pl.BlockSpec((tm, d), lambda s: (proj_tile(s), 0)),
        out_shape=jax.ShapeDtypeStruct((t, d), F32),
        scratch_shapes=[
            pltpu.VMEM((tm, GMLP_GDIM), BF16),
            pltpu.VMEM((tm, GMLP_GDIM), BF16),
        ],
        compiler_params=_params(("arbitrary",)),
        name="gmlp_out",
    )(*prep_args, *prep_args, w_out, g, *x_parts)


def _win_attn_kernel(sink_ref, q_ref, kp_ref, kc_ref, kn_ref, vp_ref, vc_ref, vn_ref, bias_ref,
                     o_ref, *, r_blocks, n_steps):
    i = pl.program_id(1)
    kall = jnp.concatenate([kp_ref[0], kc_ref[0], kn_ref[0]], axis=0)
    vall = jnp.concatenate([vp_ref[0], vc_ref[0], vn_ref[0]], axis=0)
    col = lax.broadcasted_iota(jnp.int32, (1, 3 * BLOCK), 1)
    sk = jnp.concatenate(
        [jnp.full((BLOCK, 1), sink_ref[h], F32) for h in range(N_HEADS)], axis=0)
    hrows = KV_GROUP * BLOCK
    for r in range(r_blocks):
        rows = slice(r * BLOCK, (r + 1) * BLOCK)
        oob = None
        if r == 0:
            oob = (col < BLOCK) & (i == 0)
        if r == r_blocks - 1:
            hi = (col >= 2 * BLOCK) & (i == n_steps - 1)
            oob = hi if oob is None else (oob | hi)
        parts = []
        for kh in range(N_KV_HEADS):
            kb = kall[r * BLOCK:(r + 3) * BLOCK, kh * HEAD_DIM:(kh + 1) * HEAD_DIM]
            qs = jnp.concatenate(
                [q_ref[0, rows, h * HEAD_DIM:(h + 1) * HEAD_DIM]
                 for h in range(kh * KV_GROUP, (kh + 1) * KV_GROUP)], axis=0)
            parts.append(
                lax.dot_general(qs, kb, (((1,), (1,)), ((), ())), preferred_element_type=F32))
        s = jnp.concatenate(parts, axis=0) + bias_ref[...]
        if oob is not None:
            s = jnp.where(oob, NEG_INF, s)
        m = jnp.maximum(jnp.max(s, axis=-1, keepdims=True), sk)
        p = jnp.exp2(s - m)
        inv = 1.0 / (jnp.sum(p, axis=-1, keepdims=True) + jnp.exp2(sk - m))
        pb = p.astype(BF16)
        for kh in range(N_KV_HEADS):
            vb = vall[r * BLOCK:(r + 3) * BLOCK, kh * HEAD_DIM:(kh + 1) * HEAD_DIM]
            hsl = slice(kh * hrows, (kh + 1) * hrows)
            o = _dot(pb[hsl], vb) * inv[hsl]
            for gq in range(KV_GROUP):
                h = kh * KV_GROUP + gq
                o_ref[0, rows, h * HEAD_DIM:(h + 1) * HEAD_DIM] = (
                    o[gq * BLOCK:(gq + 1) * BLOCK].astype(o_ref.dtype))


def _win_attention(qkv, sink, bias, *, nseq, r_blocks):
    tq = r_blocks * BLOCK
    n_steps = SEQ // tq
    nb = SEQ // BLOCK
    qkv3 = qkv.reshape(nseq, SEQ, qkv.shape[-1])
    kcol = N_HEADS * HEAD_DIM // (N_KV_HEADS * HEAD_DIM)
    vcol = kcol + 1
    kvw = N_KV_HEADS * HEAD_DIM

    def prev_map(c):
        return lambda b, i: (b, jnp.maximum(i * r_blocks - 1, 0), c)

    def cur_map(c):
        return lambda b, i: (b, i, c)

    def next_map(c):
        return lambda b, i: (b, jnp.minimum((i + 1) * r_blocks, nb - 1), c)

    out = pl.pallas_call(
        functools.partial(_win_attn_kernel, r_blocks=r_blocks, n_steps=n_steps),
        grid=(nseq, n_steps),
        in_specs=[
            pl.BlockSpec(memory_space=pltpu.SMEM),
            pl.BlockSpec((1, tq, N_HEADS * HEAD_DIM), lambda b, i: (b, i, 0)),
            pl.BlockSpec((1, BLOCK, kvw), prev_map(kcol)),
            pl.BlockSpec((1, tq, kvw), cur_map(kcol)),
            pl.BlockSpec((1, BLOCK, kvw), next_map(kcol)),
            pl.BlockSpec((1, BLOCK, kvw), prev_map(vcol)),
            pl.BlockSpec((1, tq, kvw), cur_map(vcol)),
            pl.BlockSpec((1, BLOCK, kvw), next_map(vcol)),
            pl.BlockSpec((N_HEADS * BLOCK, 3 * BLOCK), lambda b, i: (0, 0)),
        ],
        out_specs=pl.BlockSpec((1, tq, N_HEADS * HEAD_DIM), lambda b, i: (b, i, 0)),
        out_shape=jax.ShapeDtypeStruct((nseq, SEQ, N_HEADS * HEAD_DIM), BF16),
        compiler_params=_params(("parallel", "parallel")),
        name="win_attention",
    )(sink, qkv3, qkv3, qkv3, qkv3, qkv3, qkv3, qkv3, bias)
    return out.reshape(nseq * SEQ, N_HEADS * HEAD_DIM)


def _t5_bucket(rel):
    half = N_BUCKETS // 2
    exact = half // 2
    n = np.abs(rel)
    large = exact + (np.log(np.maximum(n, 1) / exact) / np.log(MAX_DISTANCE / exact)
                     * (half - exact)).astype(np.int32)
    large = np.minimum(large, half - 1)
    return (np.where(rel > 0, half, 0) + np.where(n < exact, n, large)).astype(np.int32)


def _band_bias(rel_table):
    rel = np.arange(3 * BLOCK)[None, :] - BLOCK - np.arange(BLOCK)[:, None]
    onehot = (_t5_bucket(rel).reshape(1, -1) == np.arange(N_BUCKETS)[:, None]).astype(np.float32)
    bias = jnp.einsum("bh,bn->hn", rel_table.astype(F32) * LOG2E, onehot,
                      precision=lax.Precision.HIGHEST)
    bias = jnp.where((np.abs(rel) <= WINDOW).reshape(1, -1), bias, NEG_INF)
    return bias.reshape(N_HEADS * BLOCK, 3 * BLOCK)


def _kv_kernel(mem_ref, g_ref, w_ref, o_ref):
    h = _rms(mem_ref[0], g_ref[0]).astype(BF16)
    o_ref[0, 0] = _dot(h, w_ref[0]).astype(o_ref.dtype)


def _kv_proj(mem, g, w_kv):
    nseq = mem.shape[0]
    return pl.pallas_call(
        _kv_kernel,
        grid=(DEPTH, nseq),
        in_specs=[
            pl.BlockSpec((1, MEM_LEN, D_MODEL), lambda l, b: (b, 0, 0)),
            pl.BlockSpec((1, 1, D_MODEL), lambda l, b: (l, 0, 0)),
            pl.BlockSpec((1, D_MODEL, 2 * MEM_WIDTH), lambda l, b: (l, 0, 0)),
        ],
        out_specs=pl.BlockSpec((1, 1, MEM_LEN, 2 * MEM_WIDTH), lambda l, b: (l, b, 0, 0)),
        out_shape=jax.ShapeDtypeStruct((DEPTH, nseq, MEM_LEN, 2 * MEM_WIDTH), BF16),
        compiler_params=_params(("parallel", "parallel")),
        name="kv_proj",
    )(mem, g, w_kv)


def _mem_attn_kernel(x_ref, gpre_ref, gpost_ref, wq_ref, kv_ref, wo_ref, o_ref):
    x = x_ref[...]
    h = _rms(x, gpre_ref[...]).astype(BF16)
    q = (_dot(h, wq_ref[...]) * (HEAD_DIM ** -0.5)).astype(BF16)
    outs = []
    for hd in range(MEM_HEADS):
        sl = slice(hd * HEAD_DIM, (hd + 1) * HEAD_DIM)
        k = kv_ref[0, :, sl]
        v = kv_ref[0, :, MEM_WIDTH + hd * HEAD_DIM:MEM_WIDTH + (hd + 1) * HEAD_DIM]
        s = lax.dot_general(q[:, sl], k, (((1,), (1,)), ((), ())), preferred_element_type=F32)
        m = jnp.max(s, axis=-1, keepdims=True)
        p = jnp.exp(s - m)
        denom = jnp.sum(p, axis=-1, keepdims=True)
        outs.append((_dot(p.astype(BF16), v) / denom).astype(BF16))
    y = _dot(jnp.concatenate(outs, axis=-1), wo_ref[...])
    o_ref[...] = x + _rms(y, gpost_ref[...])


def _mem_attention(x, gpre, gpost, w_q, kv, w_o, layer, *, tm):
    t, d = x.shape
    per_seq = SEQ // tm
    return pl.pallas_call(
        _mem_attn_kernel,
        grid=(t // tm,),
        in_specs=[
            pl.BlockSpec((tm, d), lambda i: (i, 0)),
            pl.BlockSpec((1, d), lambda i: (0, 0)),
            pl.BlockSpec((1, d), lambda i: (0, 0)),
            pl.BlockSpec((None, d, MEM_WIDTH), lambda i: (layer, 0, 0)),
            pl.BlockSpec((None, 1, MEM_LEN, 2 * MEM_WIDTH), lambda i: (layer, i // per_seq, 0, 0)),
            pl.BlockSpec((None, MEM_WIDTH, d), lambda i: (layer, 0, 0)),
        ],
        out_specs=pl.BlockSpec((tm, d), lambda i: (i, 0)),
        out_shape=jax.ShapeDtypeStruct((t, d), F32),
        compiler_params=_params(("parallel",)),
        name="mem_attention",
    )(x, gpre, gpost, w_q, kv, w_o)


def _ffn_kernel(*refs, n_chunks, cast_next):
    if cast_next:
        (x_ref, gpre_ref, gpost_ref, wg_ref, wu_ref, wd_ref, ngu_ref, nd_ref,
         o_ref, ngu_out, nd_out, h_scr) = refs
    else:
        x_ref, gpre_ref, gpost_ref, wg_ref, wu_ref, wd_ref, o_ref, h_scr = refs
    j = pl.program_id(1)

    def chunk(first, last):
        if cast_next:
            ngu_out[...] = ngu_ref[...].astype(BF16)
            nd_out[...] = nd_ref[...].astype(BF16)
        if first:
            h_scr[...] = _rms(x_ref[...], gpre_ref[...]).astype(BF16)
        h = h_scr[...]
        gate = _dot(h, wg_ref[...])
        up = _dot(h, wu_ref[...])
        a = (gate * _sigmoid(gate) * up).astype(BF16)
        if first:
            o_ref[...] = _dot(a, wd_ref[...])
        else:
            o_ref[...] += _dot(a, wd_ref[...])
        if last:
            o_ref[...] = x_ref[...] + _rms(o_ref[...], gpost_ref[...])

    pl.when(j == 0)(functools.partial(chunk, True, False))
    pl.when((j > 0) & (j < n_chunks - 1))(functools.partial(chunk, False, False))
    pl.when(j == n_chunks - 1)(functools.partial(chunk, False, True))


def _ffn(x, gpre, gpost, w_gate_up, w_down, layer, *, tm, tn, row0=0, rows=None, next_f32=None):
    d = x.shape[1]
    t = x.shape[0] if rows is None else rows
    tile0 = row0 // tm
    n_chunks = D_FF // tn
    n_tiles = t // tm
    in_specs_next, out_specs_next, out_shape_next, args_next = [], [], [], []
    if next_f32 is not None:
        ngu, nd, nl = next_f32
        gu_blk = (d // n_tiles, 2 * D_FF // n_chunks)
        d_blk = (D_FF // (n_tiles * n_chunks), d)
        assert gu_blk[0] * n_tiles == d and gu_blk[1] * n_chunks == 2 * D_FF
        assert d_blk[0] * n_tiles * n_chunks == D_FF and d_blk[0] % (2 * SUBLANES) == 0
        in_specs_next = [
            pl.BlockSpec((None,) + gu_blk, lambda i, j: (nl, i, j)),
            pl.BlockSpec((None,) + d_blk, lambda i, j: (nl, i * n_chunks + j, 0)),
        ]
        out_specs_next = [
            pl.BlockSpec(gu_blk, lambda i, j: (i, j)),
            pl.BlockSpec(d_blk, lambda i, j: (i * n_chunks + j, 0)),
        ]
        out_shape_next = [
            jax.ShapeDtypeStruct((d, 2 * D_FF), BF16),
            jax.ShapeDtypeStruct((D_FF, d), BF16),
        ]
        args_next = [ngu, nd]
    x_bytes = tm * d * 4
    fixed = 2 * x_bytes + tm * d * 2 + 2 * 3 * d * tn * 2
    temps = 2 * tm * tn * 4 + tm * tn * 2 + tm * d * 4
    x_mode = None if fixed + 2 * x_bytes + temps <= VMEM_LIMIT else pl.Buffered(1)
    out = pl.pallas_call(
        functools.partial(_ffn_kernel, n_chunks=n_chunks, cast_next=next_f32 is not None),
        grid=(n_tiles, n_chunks),
        in_specs=[
            pl.BlockSpec((tm, d), lambda i, j: (i + tile0, 0), pipeline_mode=x_mode),
            pl.BlockSpec((1, d), lambda i, j: (0, 0)),
            pl.BlockSpec((1, d), lambda i, j: (0, 0)),
            pl.BlockSpec((None, d, tn), lambda i, j: (layer, 0, j)),
            pl.BlockSpec((None, d, tn), lambda i, j: (layer, 0, j + n_chunks)),
            pl.BlockSpec((None, tn, d), lambda i, j: (layer, j, 0)),
        ] + in_specs_next,
        out_specs=[pl.BlockSpec((tm, d), lambda i, j: (i, 0))] + out_specs_next,
        out_shape=[jax.ShapeDtypeStruct((t, d), F32)] + out_shape_next,
        scratch_shapes=[pltpu.VMEM((tm, d), BF16)],
        compiler_params=_params(("parallel", "arbitrary")),
        name="ffn",
    )(x, gpre, gpost, w_gate_up, w_gate_up, w_down, *args_next)
    if next_f32 is None:
        return out[0]
    return out[0], out[1][None], out[2][None]


N_CT = D_RNN // LANES
HALO = 2 * SUBLANES


def _gate_windows():
    lo_hi = []
    for j in range(N_CT):
        h0 = (LANES * j) // RNN_BDIM
        h1 = (LANES * j + LANES - 1) // RNN_BDIM
        lo = (RNN_BDIM * h0) // LANES * LANES
        hi = -(-(RNN_BDIM * (h1 + 1)) // LANES) * LANES
        lo_hi.append((lo, hi))
    width = max(hi - lo for lo, hi in lo_hi)
    starts = tuple(min(lo, D_RNN - width) for lo, _ in lo_hi)
    return starts, width


GATE_STARTS, GATE_WIDTH = _gate_windows()


def _gate_weights(w_a, w_x):
    def tile(w, j):
        st = GATE_STARTS[j]
        pieces = []
        for h in range((LANES * j) // RNN_BDIM, (LANES * j + LANES - 1) // RNN_BDIM + 1):
            c0 = max(LANES * j, RNN_BDIM * h) - RNN_BDIM * h
            c1 = min(LANES * (j + 1), RNN_BDIM * (h + 1)) - RNN_BDIM * h
            top = RNN_BDIM * h - st
            pieces.append(jnp.pad(w[h, :, c0:c1], ((top, GATE_WIDTH - RNN_BDIM - top), (0, 0))))
        return jnp.concatenate(pieces, axis=1)

    wa, wx = (w_a * 0.5).astype(BF16), (w_x * 0.5).astype(BF16)
    return jnp.stack([jnp.concatenate([tile(wa, j), tile(wx, j)], axis=1) for j in range(N_CT)])


def _scan_kernel(*refs, ts, n_tiles, nseq, reverse):
    if reverse:
        (conv_ref, convb_ref, wp_ref, ba_ref, bx_ref, lam_ref, hsf_ref, gate_ref,
         o_ref, a_scr, u_scr, h_scr) = refs
    else:
        (xp_ref, xc_ref, xn_ref, cw_ref, cb_ref, wp_ref, ba_ref, bx_ref, lam_ref,
         o_ref, conv_ref, convb_ref, a_scr, u_scr, h_scr) = refs
    i = pl.program_id(0)
    tile = (n_tiles - 1 - i) if reverse else i

    @pl.when(i == 0)
    def _():
        h_scr[...] = jnp.zeros_like(h_scr)
        a_scr[...] = jnp.zeros_like(a_scr)
        u_scr[...] = jnp.zeros_like(u_scr)

    def conv():
        cw = cw_ref[...]
        cb = cb_ref[...]
        shifts = [k - CONV_LEFT for k in range(CONV_WIDTH) if k != CONV_LEFT]
        rr = lax.broadcasted_iota(jnp.int32, (ts, ts + 2 * HALO), 0)
        cc = lax.broadcasted_iota(jnp.int32, (ts, ts + 2 * HALO), 1)
        sel = jnp.concatenate(
            [jnp.where(cc == rr + HALO + s, 1.0, 0.0).astype(BF16) for s in shifts], axis=0)
        for b in range(nseq):
            prev = xp_ref[b]
            nxt = xn_ref[b]
            prev = jnp.where(tile == 0, jnp.zeros_like(prev), prev)
            nxt = jnp.where(tile == n_tiles - 1, jnp.zeros_like(nxt), nxt)
            cur = xc_ref[b]
            shifted = _dot(sel, jnp.concatenate([prev, cur, nxt], axis=0))
            acc = cb + cur.astype(F32) * cw[CONV_LEFT:CONV_LEFT + 1]
            for n, s in enumerate(shifts):
                k = s + CONV_LEFT
                acc = acc + shifted[n * ts:(n + 1) * ts] * cw[k:k + 1]
            conv_ref[b * ts:(b + 1) * ts, :] = acc
            convb_ref[b * ts:(b + 1) * ts, :] = acc.astype(BF16)

    def gate_tile(j):
        st = GATE_STARTS[j]
        sl = slice(LANES * j, LANES * (j + 1))
        pre = _dot(convb_ref[:, st:st + GATE_WIDTH], wp_ref[j])
        tr = jnp.tanh(pre[:, :LANES] + ba_ref[:, sl])
        ti = jnp.tanh(pre[:, LANES:] + bx_ref[:, sl])
        z = -lam_ref[:, sl]
        softplus = jnp.maximum(z, 0.0) + jnp.log(1.0 + jnp.exp(-jnp.abs(z)))
        half = (-0.5 * LRU_C * LOG2E) * softplus
        a = jnp.exp2(half + half * tr)
        y = 1.0 - a * a
        root = jnp.where(y > 0.0, y * lax.rsqrt(y), 0.0)
        hx = 0.5 * conv_ref[:, sl]
        u = root * (hx + hx * ti)
        for b in range(nseq):
            a_scr[j, pl.ds(b, ts, stride=SUBLANES), :] = a[b * ts:(b + 1) * ts]
            u_scr[j, pl.ds(b, ts, stride=SUBLANES), :] = u[b * ts:(b + 1) * ts]

    if not reverse:
        conv()
    for j in range(N_CT):
        gate_tile(j)

    def step(k, h):
        t = (ts - 1 - k) if reverse else k
        row = pl.multiple_of(t * SUBLANES, SUBLANES)
        h = a_scr[:, pl.ds(row, SUBLANES), :] * h + u_scr[:, pl.ds(row, SUBLANES), :]
        u_scr[:, pl.ds(row, SUBLANES), :] = h
        return h

    h_scr[...] = lax.fori_loop(0, ts, step, h_scr[...], unroll=8)

    for b in range(nseq):
        for j in range(N_CT):
            sl = slice(LANES * j, LANES * (j + 1))
            y = u_scr[j, pl.ds(b, ts, stride=SUBLANES), :]
            if reverse:
                y = gate_ref[b, :, sl].astype(F32) * (y + hsf_ref[b, :, sl].astype(F32))
            o_ref[b, :, sl] = y.astype(o_ref.dtype)


def _rglru_scan(y_in, conv_w, conv_b, wp, b_a, b_x, lam, fwd_out, *, ts, nseq, reverse):
    n_tiles = SEQ // ts
    per_tile = ts // HALO

    def tmap(i):
        return (n_tiles - 1 - i) if reverse else i

    gate_specs = [
        pl.BlockSpec((N_CT, GATE_WIDTH, 2 * LANES), lambda i: (0, 0, 0)),
        pl.BlockSpec((1, D_RNN), lambda i: (0, 0)),
        pl.BlockSpec((1, D_RNN), lambda i: (0, 0)),
        pl.BlockSpec((1, D_RNN), lambda i: (0, 0)),
    ]
    gate_args = [wp, b_a, b_x, lam]
    conv_blocks = [pl.BlockSpec((None, nseq * ts, D_RNN), lambda i: (tmap(i), 0, 0))] * 2
    seq_block = pl.BlockSpec((nseq, ts, D_RNN), lambda i: (0, tmap(i), 0))
    if reverse:
        hs_fwd, conv_f, conv_b16 = fwd_out
        in_specs = conv_blocks + gate_specs + [seq_block, seq_block]
        args = [conv_f, conv_b16] + gate_args + [hs_fwd, y_in]
        out_specs = seq_block
        out_shape = jax.ShapeDtypeStruct((nseq, SEQ, D_RNN), BF16)
    else:
        in_specs = [
            pl.BlockSpec((nseq, HALO, D_RNN),
                         lambda i: (0, jnp.maximum(tmap(i) * per_tile - 1, 0), 1)),
            pl.BlockSpec((nseq, ts, D_RNN), lambda i: (0, tmap(i), 1)),
            pl.BlockSpec((nseq, HALO, D_RNN),
                         lambda i: (0, jnp.minimum((tmap(i) + 1) * per_tile, SEQ // HALO - 1), 1)),
            pl.BlockSpec((CONV_WIDTH, D_RNN), lambda i: (0, 0)),
            pl.BlockSpec((1, D_RNN), lambda i: (0, 0)),
        ] + gate_specs
        args = [y_in, y_in, y_in, conv_w, conv_b] + gate_args
        out_specs = [seq_block] + conv_blocks
        out_shape = [
            jax.ShapeDtypeStruct((nseq, SEQ, D_RNN), BF16),
            jax.ShapeDtypeStruct((n_tiles, nseq * ts, D_RNN), F32),
            jax.ShapeDtypeStruct((n_tiles, nseq * ts, D_RNN), BF16),
        ]
    return pl.pallas_call(
        functools.partial(_scan_kernel, ts=ts, n_tiles=n_tiles, nseq=nseq, reverse=reverse),
        grid=(n_tiles,),
        in_specs=in_specs,
        out_specs=out_specs,
        out_shape=out_shape,
        scratch_shapes=[
            pltpu.VMEM((N_CT, ts * SUBLANES, LANES), F32),
            pltpu.VMEM((N_CT, ts * SUBLANES, LANES), F32),
            pltpu.VMEM((N_CT, SUBLANES, LANES), F32),
        ],
        compiler_params=_params(("arbitrary",)),
        name="rglru_scan_bwd" if reverse else "rglru_scan_fwd",
    )(*args)


def kernel(x_prompt, x_sample, mem_prompt, mem_sample, norms, rel_bias, a_w_in, a_ln_g, a_ln_b, a_w_s, a_b_s, a_w_out, b_w_qkv, b_sink, b_w_o, c_w_in, c_conv_w, c_conv_b, c_w_a, c_b_a, c_w_x, c_b_x, c_lam, c_w_out, m_w_q, m_w_kv, m_w_o, f_w_gate_up, f_w_down):
    n_prompt = x_prompt.shape[0]
    nseq = n_prompt + x_sample.shape[0]
    assert nseq <= SUBLANES
    t = nseq * SEQ
    t_prompt = n_prompt * SEQ
    x = None
    x_parts = [x_prompt.reshape(t_prompt, D_MODEL), x_sample.reshape(t - t_prompt, D_MODEL)]
    mem = jnp.concatenate([mem_prompt, mem_sample], axis=0)

    def gain(i, k):
        return norms[i, k].reshape(1, D_MODEL)

    a_w_in_b, a_w_out_b = a_w_in.astype(BF16), a_w_out.astype(BF16)
    q_cols = N_HEADS * HEAD_DIM
    b_w_qkv_b = jnp.concatenate(
        [b_w_qkv[:, :, :q_cols] * (HEAD_DIM ** -0.5 * LOG2E), b_w_qkv[:, :, q_cols:]],
        axis=-1).astype(BF16)
    b_w_o_b = b_w_o.astype(BF16)
    c_w_in_b, c_w_out_b = c_w_in.astype(BF16), c_w_out.astype(BF16)
    m_w_q_b, m_w_o_b = m_w_q.astype(BF16), m_w_o.astype(BF16)
    f_gu_b, f_d_b = f_w_gate_up[:1].astype(BF16), f_w_down[:1].astype(BF16)

    kv_all = _kv_proj(mem, norms[:, 4].reshape(DEPTH, 1, D_MODEL), m_w_kv.astype(BF16))

    for i in range(DEPTH):
        kind, j = i % N_MIXERS, i // N_MIXERS
        if kind == 0:
            parts = x_parts if x is None else [x]
            z, mu, rstd = _gmlp_in(parts, gain(i, 0), a_w_in_b, j, tm=1024,
                                   tn=(4 if len(parts) == 1 else 2) * GMLP_GDIM)
            x = _gmlp_out(z, mu, rstd, a_ln_g[j].reshape(GMLP_GROUPS, 1, GMLP_GDIM),
                          a_ln_b[j].reshape(GMLP_GROUPS, 1, GMLP_GDIM),
                          a_w_s[j].astype(BF16), a_b_s[j].reshape(GMLP_GROUPS, BLOCK, 1),
                          a_w_out_b, j, parts, gain(i, 1), tm=512)
        elif kind == 1:
            qkv = _norm_matmul(x, gain(i, 0), b_w_qkv_b, j, tm=1024, tn=1024)
            o = _win_attention(qkv, b_sink[j] * LOG2E, _band_bias(rel_bias), nseq=nseq, r_blocks=4)
            x = _matmul_norm_res(o, b_w_o_b, j, x, gain(i, 1), tm=512)
        else:
            y_in = _norm_matmul(x, gain(i, 0), c_w_in_b, j, tm=1024, tn=D_RNN // 2, n_gelu=2)
            y_in = y_in.reshape(nseq, SEQ, 2 * D_RNN)
            scan_args = (c_conv_w[j], c_conv_b[j].reshape(1, D_RNN))
            hs = None
            for d in range(2):
                hs = _rglru_scan(y_in, *scan_args, _gate_weights(c_w_a[j, d], c_w_x[j, d]),
                                 (c_b_a[j, d] * 0.5).reshape(1, D_RNN),
                                 (c_b_x[j, d] * 0.5).reshape(1, D_RNN),
                                 c_lam[j, d].reshape(1, D_RNN), hs, ts=64, nseq=nseq,
                                 reverse=(d == 1))
            x = _matmul_norm_res(hs.reshape(t, D_RNN), c_w_out_b, j, x, gain(i, 1), tm=512)
        x = _mem_attention(x, gain(i, 2), gain(i, 3), m_w_q_b, kv_all, m_w_o_b, i, tm=1024)
        ffn = functools.partial(_ffn, x, gain(i, 5), gain(i, 6), f_gu_b, f_d_b, 0)
        if i < DEPTH - 1:
            x, f_gu_b, f_d_b = ffn(tm=768, tn=512, next_f32=(f_w_gate_up, f_w_down, i + 1))

    y_prompt = ffn(tm=1024, tn=256, row0=0, rows=t_prompt).reshape(n_prompt, SEQ, D_MODEL)
    y_sample = ffn(tm=1024, tn=256, row0=t_prompt, rows=t - t_prompt).reshape(
        nseq - n_prompt, SEQ, D_MODEL)
    return (y_prompt, y_sample)
```

```python
import functools

import numpy as np
import jax
import jax.numpy as jnp
from jax import lax
from jax.experimental import pallas as pl
from jax.experimental.pallas import tpu as pltpu

D_MODEL = 2048
SEQ = 4096
DEPTH = 4
N_MIXERS = 3
BLOCK = 128
HEAD_DIM = 128
EPS = 1e-6
NEG_INF = -1e30
GMLP_HALF = 2 * D_MODEL
GMLP_GROUPS = 8
GMLP_GDIM = GMLP_HALF // GMLP_GROUPS
N_HEADS = D_MODEL // HEAD_DIM
N_KV_HEADS = 4
KV_GROUP = N_HEADS // N_KV_HEADS
WINDOW = 128
N_BUCKETS = 32
MAX_DISTANCE = 128
D_RNN = 2816
RNN_BLOCKS = 16
RNN_BDIM = D_RNN // RNN_BLOCKS
CONV_WIDTH = 4
CONV_LEFT = 2
LRU_C = 8.0
MEM_LEN = 256
MEM_HEADS = 4
MEM_WIDTH = MEM_HEADS * HEAD_DIM
D_FF = 5632

LANES = 128
SUBLANES = 8
VMEM_LIMIT = 56 * 1024 * 1024

F32 = jnp.float32
BF16 = jnp.bfloat16
LOG2E = float(np.log2(np.e))


def _params(sem):
    return pltpu.CompilerParams(dimension_semantics=sem, vmem_limit_bytes=VMEM_LIMIT)


def _rms(x, g):
    ms = jnp.mean(x * x, axis=-1, keepdims=True)
    return x * lax.rsqrt(ms + EPS) * g


def _gelu(x):
    k1 = float(-2.0 * np.sqrt(2.0 / np.pi) * LOG2E)
    return x / (1.0 + jnp.exp2(x * (k1 + (k1 * 0.044715) * (x * x))))


def _sigmoid(x):
    return 1.0 / (1.0 + jnp.exp(-x))


def _dot(a, b):
    return jnp.dot(a, b, preferred_element_type=F32)


def _norm_mm_kernel(x_ref, g_ref, w_ref, o_ref, h_scr, *, n_gelu):
    j = pl.program_id(1)

    def emit(act, first):
        if first:
            h_scr[...] = _rms(x_ref[...], g_ref[...]).astype(BF16)
        z = _dot(h_scr[...], w_ref[...])
        o_ref[...] = (_gelu(z) if act else z).astype(o_ref.dtype)

    pl.when(j == 0)(functools.partial(emit, n_gelu > 0, True))
    if n_gelu > 1:
        pl.when((j > 0) & (j < n_gelu))(functools.partial(emit, True, False))
    pl.when(j >= max(n_gelu, 1))(functools.partial(emit, False, False))


def _norm_matmul(x, g, w, layer, *, tm, tn, n_gelu=0):
    t, d = x.shape
    n = w.shape[2]
    return pl.pallas_call(
        functools.partial(_norm_mm_kernel, n_gelu=n_gelu),
        grid=(t // tm, n // tn),
        in_specs=[
            pl.BlockSpec((tm, d), lambda i, j: (i, 0)),
            pl.BlockSpec((1, d), lambda i, j: (0, 0)),
            pl.BlockSpec((None, d, tn), lambda i, j: (layer, 0, j)),
        ],
        out_specs=pl.BlockSpec((tm, tn), lambda i, j: (i, j)),
        out_shape=jax.ShapeDtypeStruct((t, n), BF16),
        scratch_shapes=[pltpu.VMEM((tm, d), BF16)],
        compiler_params=_params(("parallel", "arbitrary")),
        name="norm_matmul",
    )(x, g, w)


def _part_specs(parts, tm, tile_of):
    specs, ranges, lo = [], [], 0
    for p in parts:
        n = p.shape[0] // tm
        specs.append(pl.BlockSpec(
            (tm, p.shape[1]),
            lambda *idx, lo=lo, n=n: (jnp.clip(tile_of(*idx) - lo, 0, n - 1), 0)))
        ranges.append((lo, lo + n))
        lo += n
    return specs, tuple(ranges)


def _for_owner(i, ranges, refs, cond, fn):
    for (lo, hi), ref in zip(ranges, refs):
        pl.when(cond & (i >= lo) & (i < hi))(functools.partial(fn, ref))


MXU_COLS = 256


def _gmlp_in_kernel(*refs, ranges, n_col, tn):
    n_parts = len(ranges)
    x_refs = refs[:n_parts]
    g_ref, w_ref, z_ref, mu_ref, rs_ref, h_scr, s1_scr, s2_scr = refs[n_parts:]
    i, j = pl.program_id(0), pl.program_id(1)
    tm = h_scr.shape[0]

    def step(is_v, x_ref=None, last=False):
        if x_ref is not None:
            h_scr[...] = _rms(x_ref[...], g_ref[...]).astype(BF16)
            s1_scr[...] = jnp.zeros_like(s1_scr)
            s2_scr[...] = jnp.zeros_like(s2_scr)
        p1 = jnp.zeros((tm, LANES), F32)
        p2 = jnp.zeros((tm, LANES), F32)
        per_tile = GMLP_GDIM // MXU_COLS
        for c in range(tn // MXU_COLS):
            cols = slice(c * MXU_COLS, (c + 1) * MXU_COLS)
            zc = _gelu(_dot(h_scr[...], w_ref[:, cols]))
            lo = (c % per_tile) * MXU_COLS
            z_ref[c // per_tile, :, lo:lo + MXU_COLS] = zc.astype(z_ref.dtype)
            if is_v:
                for k in range(MXU_COLS // LANES):
                    zk = zc[:, k * LANES:(k + 1) * LANES]
                    p1 = p1 + zk
                    p2 = p2 + zk * zk
        if is_v:
            s1 = s1_scr[...] + p1
            s2 = s2_scr[...] + p2
            s1_scr[...] = s1
            s2_scr[...] = s2
        if last:
            mu = jnp.sum(s1, axis=-1, keepdims=True) * (1.0 / GMLP_HALF)
            msq = jnp.sum(s2, axis=-1, keepdims=True) * (1.0 / GMLP_HALF)
            mu_ref[...] = mu
            rs_ref[...] = lax.rsqrt(jnp.maximum(msq - mu * mu, 0.0) + EPS)

    _for_owner(i, ranges, x_refs, j == 0, functools.partial(step, False))
    pl.when((j > 0) & (j < n_col // 2))(functools.partial(step, False))
    pl.when((j >= n_col // 2) & (j < n_col - 1))(functools.partial(step, True))
    pl.when(j == n_col - 1)(functools.partial(step, True, None, True))


def _gmlp_in(x_parts, g, w, layer, *, tm, tn):
    t = sum(p.shape[0] for p in x_parts)
    d = x_parts[0].shape[1]
    n_col = 2 * GMLP_HALF // tn
    sub = tn // GMLP_GDIM
    x_specs, ranges = _part_specs(x_parts, tm, lambda i, j: i)
    return pl.pallas_call(
        functools.partial(_gmlp_in_kernel, ranges=ranges, n_col=n_col, tn=tn),
        grid=(t // tm, n_col),
        in_specs=x_specs + [
            pl.BlockSpec((1, d), lambda i, j: (0, 0)),
            pl.BlockSpec((None, d, tn), lambda i, j: (layer, 0, j)),
        ],
        out_specs=[
            pl.BlockSpec((sub, tm, GMLP_GDIM), lambda i, j: (j, i, 0)),
            pl.BlockSpec((tm, 1), lambda i, j: (i, 0)),
            pl.BlockSpec((tm, 1), lambda i, j: (i, 0)),
        ],
        out_shape=[
            jax.ShapeDtypeStruct((2 * GMLP_GROUPS, t, GMLP_GDIM), BF16),
            jax.ShapeDtypeStruct((t, 1), F32),
            jax.ShapeDtypeStruct((t, 1), F32),
        ],
        scratch_shapes=[
            pltpu.VMEM((tm, d), BF16),
            pltpu.VMEM((tm, LANES), F32),
            pltpu.VMEM((tm, LANES), F32),
        ],
        compiler_params=_params(("parallel", "arbitrary")),
        name="gmlp_in",
    )(*x_parts, g, w)


def _mm_norm_res_kernel(a_ref, w_ref, x_ref, g_ref, o_ref):
    y = _dot(a_ref[...], w_ref[...])
    o_ref[...] = x_ref[...] + _rms(y, g_ref[...])


def _matmul_norm_res(a, w, layer, x, g, *, tm):
    t, k = a.shape
    d = w.shape[2]
    return pl.pallas_call(
        _mm_norm_res_kernel,
        grid=(t // tm,),
        in_specs=[
            pl.BlockSpec((tm, k), lambda i: (i, 0)),
            pl.BlockSpec((None, k, d), lambda i: (layer, 0, 0), pipeline_mode=pl.Buffered(1)),
            pl.BlockSpec((tm, d), lambda i: (i, 0)),
            pl.BlockSpec((1, d), lambda i: (0, 0)),
        ],
        out_specs=pl.BlockSpec((tm, d), lambda i: (i, 0)),
        out_shape=jax.ShapeDtypeStruct((t, d), F32),
        compiler_params=_params(("parallel",)),
        name="matmul_norm_res",
    )(a, w, x, g)


PAIRS = GMLP_GROUPS // 2


def _gmlp_out_kernel(*refs, ranges, tm):
    n_parts = len(ranges)
    (u1_ref, v1_ref, mu1_ref, rs1_ref, lng1_ref, lnb1_ref, ws1_ref, bs1_ref,
     u2_ref, v2_ref, mu2_ref, rs2_ref, lng2_ref, lnb2_ref, ws2_ref, bs2_ref,
     wo_ref, g_ref) = refs[:18]
    x_refs = refs[18:18 + n_parts]
    o_ref, a0_scr, a1_scr = refs[18 + n_parts:]
    s = pl.program_id(0)
    q = jnp.maximum(s - 1, 0)
    tile, pair = q // PAIRS, q % PAIRS

    @pl.when(s == 0)
    def _():
        a0_scr[...] = jnp.zeros_like(a0_scr)
        a1_scr[...] = jnp.zeros_like(a1_scr)

    def prep(u_ref, v_ref, mu_ref, rs_ref, lng_ref, lnb_ref, ws_ref, bs_ref):
        vn = ((v_ref[0].astype(F32) - mu_ref[...]) * rs_ref[...] * lng_ref[0]
              + lnb_ref[0]).astype(BF16)
        ws = ws_ref[0]
        bs = bs_ref[0]
        parts = []
        for n in range(tm // BLOCK):
            rows = slice(n * BLOCK, (n + 1) * BLOCK)
            mixed = _dot(ws, vn[rows, :]) + bs
            parts.append((u_ref[0, rows, :].astype(F32) * mixed).astype(BF16))
        return jnp.concatenate(parts, axis=0)

    def step(first, x_ref=None):
        row0 = pl.multiple_of(pair * (2 * GMLP_GDIM), 2 * GMLP_GDIM)
        proj = _dot(a0_scr[...], wo_ref[pl.ds(row0, GMLP_GDIM), :])
        if first:
            o_ref[...] = proj
        else:
            o_ref[...] += proj
        a1_scr[...] = prep(u1_ref, v1_ref, mu1_ref, rs1_ref, lng1_ref, lnb1_ref, ws1_ref, bs1_ref)
        o_ref[...] += _dot(a1_scr[...], wo_ref[pl.ds(row0 + GMLP_GDIM, GMLP_GDIM), :])
        a0_scr[...] = prep(u2_ref, v2_ref, mu2_ref, rs2_ref, lng2_ref, lnb2_ref, ws2_ref, bs2_ref)
        if x_ref is not None:
            o_ref[...] = x_ref[...] + _rms(o_ref[...], g_ref[...])

    pl.when(pair == 0)(functools.partial(step, True))
    pl.when((pair > 0) & (pair < PAIRS - 1))(functools.partial(step, False))
    _for_owner(tile, ranges, x_refs, pair == PAIRS - 1, functools.partial(step, False))


def _gmlp_out(z, mu, rstd, ln_g, ln_b, w_s, b_s, w_out, layer, x_parts, g, *, tm):
    t = sum(p.shape[0] for p in x_parts)
    d = x_parts[0].shape[1]
    n_tiles = t // tm
    last = GMLP_GROUPS * n_tiles - 1

    def proj_tile(s):
        return jnp.maximum(s - 1, 0) // PAIRS

    def prep_specs(pair_of):
        def at(f):
            return lambda s: f(*pair_of(s))
        return [
            pl.BlockSpec((1, tm, GMLP_GDIM), at(lambda i, c: (c, i, 0))),
            pl.BlockSpec((1, tm, GMLP_GDIM), at(lambda i, c: (GMLP_GROUPS + c, i, 0))),
            pl.BlockSpec((tm, 1), at(lambda i, c: (i, 0))),
            pl.BlockSpec((tm, 1), at(lambda i, c: (i, 0))),
            pl.BlockSpec((1, 1, GMLP_GDIM), at(lambda i, c: (c, 0, 0))),
            pl.BlockSpec((1, 1, GMLP_GDIM), at(lambda i, c: (c, 0, 0))),
            pl.BlockSpec((1, BLOCK, BLOCK), at(lambda i, c: (c, 0, 0))),
            pl.BlockSpec((1, BLOCK, 1), at(lambda i, c: (c, 0, 0))),
        ]

    def split(gidx):
        gidx = jnp.clip(gidx, 0, last)
        return gidx // GMLP_GROUPS, gidx % GMLP_GROUPS

    x_specs, ranges = _part_specs(x_parts, tm, proj_tile)
    prep_args = (z, z, mu, rstd, ln_g, ln_b, w_s, b_s)
    return pl.pallas_call(
        functools.partial(_gmlp_out_kernel, ranges=ranges, tm=tm),
        grid=(PAIRS * n_tiles + 1,),
        in_specs=prep_specs(lambda s: split(2 * s - 1)) + prep_specs(lambda s: split(2 * s)) + [
            pl.BlockSpec((None, GMLP_HALF, d), lambda s: (layer, 0, 0),
                         pipeline_mode=pl.Buffered(1)),
            pl.BlockSpec((1, d), lambda s: (0, 0)),
        ] + x_specs,
        out_specs=pl.BlockSpec((tm, d), lambda s: (proj_tile(s), 0)),
        out_shape=jax.ShapeDtypeStruct((t, d), F32),
        scratch_shapes=[
            pltpu.VMEM((tm, GMLP_GDIM), BF16),
            pltpu.VMEM((tm, GMLP_GDIM), BF16),
        ],
        compiler_params=_params(("arbitrary",)),
        name="gmlp_out",
    )(*prep_args, *prep_args, w_out, g, *x_parts)


def _win_attn_kernel(sink_ref, q_ref, kp_ref, kc_ref, kn_ref, vp_ref, vc_ref, vn_ref, bias_ref,
                     o_ref, *, r_blocks, n_steps):
    i = pl.program_id(1)
    kall = jnp.concatenate([kp_ref[0], kc_ref[0], kn_ref[0]], axis=0)
    vall = jnp.concatenate([vp_ref[0], vc_ref[0], vn_ref[0]], axis=0)
    col = lax.broadcasted_iota(jnp.int32, (1, 3 * BLOCK), 1)
    sk = jnp.concatenate(
        [jnp.full((BLOCK, 1), sink_ref[h], F32) for h in range(N_HEADS)], axis=0)
    hrows = KV_GROUP * BLOCK
    for r in range(r_blocks):
        rows = slice(r * BLOCK, (r + 1) * BLOCK)
        oob = None
        if r == 0:
            oob = (col < BLOCK) & (i == 0)
        if r == r_blocks - 1:
            hi = (col >= 2 * BLOCK) & (i == n_steps - 1)
            oob = hi if oob is None else (oob | hi)
        parts = []
        for kh in range(N_KV_HEADS):
            kb = kall[r * BLOCK:(r + 3) * BLOCK, kh * HEAD_DIM:(kh + 1) * HEAD_DIM]
            qs = jnp.concatenate(
                [q_ref[0, rows, h * HEAD_DIM:(h + 1) * HEAD_DIM]
                 for h in range(kh * KV_GROUP, (kh + 1) * KV_GROUP)], axis=0)
            parts.append(
                lax.dot_general(qs, kb, (((1,), (1,)), ((), ())), preferred_element_type=F32))
        s = jnp.concatenate(parts, axis=0) + bias_ref[...]
        if oob is not None:
            s = jnp.where(oob, NEG_INF, s)
        m = jnp.maximum(jnp.max(s, axis=-1, keepdims=True), sk)
        p = jnp.exp2(s - m)
        inv = 1.0 / (jnp.sum(p, axis=-1, keepdims=True) + jnp.exp2(sk - m))
        pb = p.astype(BF16)
        for kh in range(N_KV_HEADS):
            vb = vall[r * BLOCK:(r + 3) * BLOCK, kh * HEAD_DIM:(kh + 1) * HEAD_DIM]
            hsl = slice(kh * hrows, (kh + 1) * hrows)
            o = _dot(pb[hsl], vb) * inv[hsl]
            for gq in range(KV_GROUP):
                h = kh * KV_GROUP + gq
                o_ref[0, rows, h * HEAD_DIM:(h + 1) * HEAD_DIM] = (
                    o[gq * BLOCK:(gq + 1) * BLOCK].astype(o_ref.dtype))


def _win_attention(qkv, sink, bias, *, nseq, r_blocks):
    tq = r_blocks * BLOCK
    n_steps = SEQ // tq
    nb = SEQ // BLOCK
    qkv3 = qkv.reshape(nseq, SEQ, qkv.shape[-1])
    kcol = N_HEADS * HEAD_DIM // (N_KV_HEADS * HEAD_DIM)
    vcol = kcol + 1
    kvw = N_KV_HEADS * HEAD_DIM

    def prev_map(c):
        return lambda b, i: (b, jnp.maximum(i * r_blocks - 1, 0), c)

    def cur_map(c):
        return lambda b, i: (b, i, c)

    def next_map(c):
        return lambda b, i: (b, jnp.minimum((i + 1) * r_blocks, nb - 1), c)

    out = pl.pallas_call(
        functools.partial(_win_attn_kernel, r_blocks=r_blocks, n_steps=n_steps),
        grid=(nseq, n_steps),
        in_specs=[
            pl.BlockSpec(memory_space=pltpu.SMEM),
            pl.BlockSpec((1, tq, N_HEADS * HEAD_DIM), lambda b, i: (b, i, 0)),
            pl.BlockSpec((1, BLOCK, kvw), prev_map(kcol)),
            pl.BlockSpec((1, tq, kvw), cur_map(kcol)),
            pl.BlockSpec((1, BLOCK, kvw), next_map(kcol)),
            pl.BlockSpec((1, BLOCK, kvw), prev_map(vcol)),
            pl.BlockSpec((1, tq, kvw), cur_map(vcol)),
            pl.BlockSpec((1, BLOCK, kvw), next_map(vcol)),
            pl.BlockSpec((N_HEADS * BLOCK, 3 * BLOCK), lambda b, i: (0, 0)),
        ],
        out_specs=pl.BlockSpec((1, tq, N_HEADS * HEAD_DIM), lambda b, i: (b, i, 0)),
        out_shape=jax.ShapeDtypeStruct((nseq, SEQ, N_HEADS * HEAD_DIM), BF16),
        compiler_params=_params(("parallel", "parallel")),
        name="win_attention",
    )(sink, qkv3, qkv3, qkv3, qkv3, qkv3, qkv3, qkv3, bias)
    return out.reshape(nseq * SEQ, N_HEADS * HEAD_DIM)


def _t5_bucket(rel):
    half = N_BUCKETS // 2
    exact = half // 2
    n = np.abs(rel)
    large = exact + (np.log(np.maximum(n, 1) / exact) / np.log(MAX_DISTANCE / exact)
                     * (half - exact)).astype(np.int32)
    large = np.minimum(large, half - 1)
    return (np.where(rel > 0, half, 0) + np.where(n < exact, n, large)).astype(np.int32)


def _band_bias(rel_table):
    rel = np.arange(3 * BLOCK)[None, :] - BLOCK - np.arange(BLOCK)[:, None]
    onehot = (_t5_bucket(rel).reshape(1, -1) == np.arange(N_BUCKETS)[:, None]).astype(np.float32)
    bias = jnp.einsum("bh,bn->hn", rel_table.astype(F32) * LOG2E, onehot,
                      precision=lax.Precision.HIGHEST)
    bias = jnp.where((np.abs(rel) <= WINDOW).reshape(1, -1), bias, NEG_INF)
    return bias.reshape(N_HEADS * BLOCK, 3 * BLOCK)


def _kv_kernel(mem_ref, g_ref, w_ref, o_ref):
    h = _rms(mem_ref[0], g_ref[0]).astype(BF16)
    o_ref[0, 0] = _dot(h, w_ref[0]).astype(o_ref.dtype)


def _kv_proj(mem, g, w_kv):
    nseq = mem.shape[0]
    return pl.pallas_call(
        _kv_kernel,
        grid=(DEPTH, nseq),
        in_specs=[
            pl.BlockSpec((1, MEM_LEN, D_MODEL), lambda l, b: (b, 0, 0)),
            pl.BlockSpec((1, 1, D_MODEL), lambda l, b: (l, 0, 0)),
            pl.BlockSpec((1, D_MODEL, 2 * MEM_WIDTH), lambda l, b: (l, 0, 0)),
        ],
        out_specs=pl.BlockSpec((1, 1, MEM_LEN, 2 * MEM_WIDTH), lambda l, b: (l, b, 0, 0)),
        out_shape=jax.ShapeDtypeStruct((DEPTH, nseq, MEM_LEN, 2 * MEM_WIDTH), BF16),
        compiler_params=_params(("parallel", "parallel")),
        name="kv_proj",
    )(mem, g, w_kv)


def _mem_attn_kernel(x_ref, gpre_ref, gpost_ref, wq_ref, kv_ref, wo_ref, o_ref):
    x = x_ref[...]
    h = _rms(x, gpre_ref[...]).astype(BF16)
    q = (_dot(h, wq_ref[...]) * (HEAD_DIM ** -0.5)).astype(BF16)
    outs = []
    for hd in range(MEM_HEADS):
        sl = slice(hd * HEAD_DIM, (hd + 1) * HEAD_DIM)
        k = kv_ref[0, :, sl]
        v = kv_ref[0, :, MEM_WIDTH + hd * HEAD_DIM:MEM_WIDTH + (hd + 1) * HEAD_DIM]
        s = lax.dot_general(q[:, sl], k, (((1,), (1,)), ((), ())), preferred_element_type=F32)
        m = jnp.max(s, axis=-1, keepdims=True)
        p = jnp.exp(s - m)
        denom = jnp.sum(p, axis=-1, keepdims=True)
        outs.append((_dot(p.astype(BF16), v) / denom).astype(BF16))
    y = _dot(jnp.concatenate(outs, axis=-1), wo_ref[...])
    o_ref[...] = x + _rms(y, gpost_ref[...])


def _mem_attention(x, gpre, gpost, w_q, kv, w_o, layer, *, tm):
    t, d = x.shape
    per_seq = SEQ // tm
    return pl.pallas_call(
        _mem_attn_kernel,
        grid=(t // tm,),
        in_specs=[
            pl.BlockSpec((tm, d), lambda i: (i, 0)),
            pl.BlockSpec((1, d), lambda i: (0, 0)),
            pl.BlockSpec((1, d), lambda i: (0, 0)),
            pl.BlockSpec((None, d, MEM_WIDTH), lambda i: (layer, 0, 0)),
            pl.BlockSpec((None, 1, MEM_LEN, 2 * MEM_WIDTH), lambda i: (layer, i // per_seq, 0, 0)),
            pl.BlockSpec((None, MEM_WIDTH, d), lambda i: (layer, 0, 0)),
        ],
        out_specs=pl.BlockSpec((tm, d), lambda i: (i, 0)),
        out_shape=jax.ShapeDtypeStruct((t, d), F32),
        compiler_params=_params(("parallel",)),
        name="mem_attention",
    )(x, gpre, gpost, w_q, kv, w_o)


def _ffn_kernel(*refs, n_chunks, cast_next):
    if cast_next:
        (x_ref, gpre_ref, gpost_ref, wg_ref, wu_ref, wd_ref, ngu_ref, nd_ref,
         o_ref, ngu_out, nd_out, h_scr) = refs
    else:
        x_ref, gpre_ref, gpost_ref, wg_ref, wu_ref, wd_ref, o_ref, h_scr = refs
    j = pl.program_id(1)

    def chunk(first, last):
        if cast_next:
            ngu_out[...] = ngu_ref[...].astype(BF16)
            nd_out[...] = nd_ref[...].astype(BF16)
        if first:
            h_scr[...] = _rms(x_ref[...], gpre_ref[...]).astype(BF16)
        h = h_scr[...]
        gate = _dot(h, wg_ref[...])
        up = _dot(h, wu_ref[...])
        a = (gate * _sigmoid(gate) * up).astype(BF16)
        if first:
            o_ref[...] = _dot(a, wd_ref[...])
        else:
            o_ref[...] += _dot(a, wd_ref[...])
        if last:
            o_ref[...] = x_ref[...] + _rms(o_ref[...], gpost_ref[...])

    pl.when(j == 0)(functools.partial(chunk, True, False))
    pl.when((j > 0) & (j < n_chunks - 1))(functools.partial(chunk, False, False))
    pl.when(j == n_chunks - 1)(functools.partial(chunk, False, True))


def _ffn(x, gpre, gpost, w_gate_up, w_down, layer, *, tm, tn, row0=0, rows=None, next_f32=None):
    d = x.shape[1]
    t = x.shape[0] if rows is None else rows
    tile0 = row0 // tm
    n_chunks = D_FF // tn
    n_tiles = t // tm
    in_specs_next, out_specs_next, out_shape_next, args_next = [], [], [], []
    if next_f32 is not None:
        ngu, nd, nl = next_f32
        gu_blk = (d // n_tiles, 2 * D_FF // n_chunks)
        d_blk = (D_FF // (n_tiles * n_chunks), d)
        assert gu_blk[0] * n_tiles == d and gu_blk[1] * n_chunks == 2 * D_FF
        assert d_blk[0] * n_tiles * n_chunks == D_FF and d_blk[0] % (2 * SUBLANES) == 0
        in_specs_next = [
            pl.BlockSpec((None,) + gu_blk, lambda i, j: (nl, i, j)),
            pl.BlockSpec((None,) + d_blk, lambda i, j: (nl, i * n_chunks + j, 0)),
        ]
        out_specs_next = [
            pl.BlockSpec(gu_blk, lambda i, j: (i, j)),
            pl.BlockSpec(d_blk, lambda i, j: (i * n_chunks + j, 0)),
        ]
        out_shape_next = [
            jax.ShapeDtypeStruct((d, 2 * D_FF), BF16),
            jax.ShapeDtypeStruct((D_FF, d), BF16),
        ]
        args_next = [ngu, nd]
    x_bytes = tm * d * 4
    fixed = 2 * x_bytes + tm * d * 2 + 2 * 3 * d * tn * 2
    temps = 2 * tm * tn * 4 + tm * tn * 2 + tm * d * 4
    x_mode = None if fixed + 2 * x_bytes + temps <= VMEM_LIMIT else pl.Buffered(1)
    out = pl.pallas_call(
        functools.partial(_ffn_kernel, n_chunks=n_chunks, cast_next=next_f32 is not None),
        grid=(n_tiles, n_chunks),
        in_specs=[
            pl.BlockSpec((tm, d), lambda i, j: (i + tile0, 0), pipeline_mode=x_mode),
            pl.BlockSpec((1, d), lambda i, j: (0, 0)),
            pl.BlockSpec((1, d), lambda i, j: (0, 0)),
            pl.BlockSpec((None, d, tn), lambda i, j: (layer, 0, j)),
            pl.BlockSpec((None, d, tn), lambda i, j: (layer, 0, j + n_chunks)),
            pl.BlockSpec((None, tn, d), lambda i, j: (layer, j, 0)),
        ] + in_specs_next,
        out_specs=[pl.BlockSpec((tm, d), lambda i, j: (i, 0))] + out_specs_next,
        out_shape=[jax.ShapeDtypeStruct((t, d), F32)] + out_shape_next,
        scratch_shapes=[pltpu.VMEM((tm, d), BF16)],
        compiler_params=_params(("parallel", "arbitrary")),
        name="ffn",
    )(x, gpre, gpost, w_gate_up, w_gate_up, w_down, *args_next)
    if next_f32 is None:
        return out[0]
    return out[0], out[1][None], out[2][None]


N_CT = D_RNN // LANES
HALO = 2 * SUBLANES


def _gate_windows():
    lo_hi = []
    for j in range(N_CT):
        h0 = (LANES * j) // RNN_BDIM
        h1 = (LANES * j + LANES - 1) // RNN_BDIM
        lo = (RNN_BDIM * h0) // LANES * LANES
        hi = -(-(RNN_BDIM * (h1 + 1)) // LANES) * LANES
        lo_hi.append((lo, hi))
    width = max(hi - lo for lo, hi in lo_hi)
    starts = tuple(min(lo, D_RNN - width) for lo, _ in lo_hi)
    return starts, width


GATE_STARTS, GATE_WIDTH = _gate_windows()


def _gate_weights(w_a, w_x):
    def tile(w, j):
        st = GATE_STARTS[j]
        pieces = []
        for h in range((LANES * j) // RNN_BDIM, (LANES * j + LANES - 1) // RNN_BDIM + 1):
            c0 = max(LANES * j, RNN_BDIM * h) - RNN_BDIM * h
            c1 = min(LANES * (j + 1), RNN_BDIM * (h + 1)) - RNN_BDIM * h
            top = RNN_BDIM * h - st
            pieces.append(jnp.pad(w[h, :, c0:c1], ((top, GATE_WIDTH - RNN_BDIM - top), (0, 0))))
        return jnp.concatenate(pieces, axis=1)

    wa, wx = (w_a * 0.5).astype(BF16), (w_x * 0.5).astype(BF16)
    return jnp.stack([jnp.concatenate([tile(wa, j), tile(wx, j)], axis=1) for j in range(N_CT)])


def _scan_kernel(*refs, ts, n_tiles, nseq, reverse):
    if reverse:
        (conv_ref, convb_ref, wp_ref, ba_ref, bx_ref, lam_ref, hsf_ref, gate_ref,
         o_ref, a_scr, u_scr, h_scr) = refs
    else:
        (xp_ref, xc_ref, xn_ref, cw_ref, cb_ref, wp_ref, ba_ref, bx_ref, lam_ref,
         o_ref, conv_ref, convb_ref, a_scr, u_scr, h_scr) = refs
    i = pl.program_id(0)
    tile = (n_tiles - 1 - i) if reverse else i

    @pl.when(i == 0)
    def _():
        h_scr[...] = jnp.zeros_like(h_scr)
        a_scr[...] = jnp.zeros_like(a_scr)
        u_scr[...] = jnp.zeros_like(u_scr)

    def conv():
        cw = cw_ref[...]
        cb = cb_ref[...]
        shifts = [k - CONV_LEFT for k in range(CONV_WIDTH) if k != CONV_LEFT]
        rr = lax.broadcasted_iota(jnp.int32, (ts, ts + 2 * HALO), 0)
        cc = lax.broadcasted_iota(jnp.int32, (ts, ts + 2 * HALO), 1)
        sel = jnp.concatenate(
            [jnp.where(cc == rr + HALO + s, 1.0, 0.0).astype(BF16) for s in shifts], axis=0)
        for b in range(nseq):
            prev = xp_ref[b]
            nxt = xn_ref[b]
            prev = jnp.where(tile == 0, jnp.zeros_like(prev), prev)
            nxt = jnp.where(tile == n_tiles - 1, jnp.zeros_like(nxt), nxt)
            cur = xc_ref[b]
            shifted = _dot(sel, jnp.concatenate([prev, cur, nxt], axis=0))
            acc = cb + cur.astype(F32) * cw[CONV_LEFT:CONV_LEFT + 1]
            for n, s in enumerate(shifts):
                k = s + CONV_LEFT
                acc = acc + shifted[n * ts:(n + 1) * ts] * cw[k:k + 1]
            conv_ref[b * ts:(b + 1) * ts, :] = acc
            convb_ref[b * ts:(b + 1) * ts, :] = acc.astype(BF16)

    def gate_tile(j):
        st = GATE_STARTS[j]
        sl = slice(LANES * j, LANES * (j + 1))
        pre = _dot(convb_ref[:, st:st + GATE_WIDTH], wp_ref[j])
        tr = jnp.tanh(pre[:, :LANES] + ba_ref[:, sl])
        ti = jnp.tanh(pre[:, LANES:] + bx_ref[:, sl])
        z = -lam_ref[:, sl]
        softplus = jnp.maximum(z, 0.0) + jnp.log(1.0 + jnp.exp(-jnp.abs(z)))
        half = (-0.5 * LRU_C * LOG2E) * softplus
        a = jnp.exp2(half + half * tr)
        y = 1.0 - a * a
        root = jnp.where(y > 0.0, y * lax.rsqrt(y), 0.0)
        hx = 0.5 * conv_ref[:, sl]
        u = root * (hx + hx * ti)
        for b in range(nseq):
            a_scr[j, pl.ds(b, ts, stride=SUBLANES), :] = a[b * ts:(b + 1) * ts]
            u_scr[j, pl.ds(b, ts, stride=SUBLANES), :] = u[b * ts:(b + 1) * ts]

    if not reverse:
        conv()
    for j in range(N_CT):
        gate_tile(j)

    def step(k, h):
        t = (ts - 1 - k) if reverse else k
        row = pl.multiple_of(t * SUBLANES, SUBLANES)
        h = a_scr[:, pl.ds(row, SUBLANES), :] * h + u_scr[:, pl.ds(row, SUBLANES), :]
        u_scr[:, pl.ds(row, SUBLANES), :] = h
        return h

    h_scr[...] = lax.fori_loop(0, ts, step, h_scr[...], unroll=8)

    for b in range(nseq):
        for j in range(N_CT):
            sl = slice(LANES * j, LANES * (j + 1))
            y = u_scr[j, pl.ds(b, ts, stride=SUBLANES), :]
            if reverse:
                y = gate_ref[b, :, sl].astype(F32) * (y + hsf_ref[b, :, sl].astype(F32))
            o_ref[b, :, sl] = y.astype(o_ref.dtype)


def _rglru_scan(y_in, conv_w, conv_b, wp, b_a, b_x, lam, fwd_out, *, ts, nseq, reverse):
    n_tiles = SEQ // ts
    per_tile = ts // HALO

    def tmap(i):
        return (n_tiles - 1 - i) if reverse else i

    gate_specs = [
        pl.BlockSpec((N_CT, GATE_WIDTH, 2 * LANES), lambda i: (0, 0, 0)),
        pl.BlockSpec((1, D_RNN), lambda i: (0, 0)),
        pl.BlockSpec((1, D_RNN), lambda i: (0, 0)),
        pl.BlockSpec((1, D_RNN), lambda i: (0, 0)),
    ]
    gate_args = [wp, b_a, b_x, lam]
    conv_blocks = [pl.BlockSpec((None, nseq * ts, D_RNN), lambda i: (tmap(i), 0, 0))] * 2
    seq_block = pl.BlockSpec((nseq, ts, D_RNN), lambda i: (0, tmap(i), 0))
    if reverse:
        hs_fwd, conv_f, conv_b16 = fwd_out
        in_specs = conv_blocks + gate_specs + [seq_block, seq_block]
        args = [conv_f, conv_b16] + gate_args + [hs_fwd, y_in]
        out_specs = seq_block
        out_shape = jax.ShapeDtypeStruct((nseq, SEQ, D_RNN), BF16)
    else:
        in_specs = [
            pl.BlockSpec((nseq, HALO, D_RNN),
                         lambda i: (0, jnp.maximum(tmap(i) * per_tile - 1, 0), 1)),
            pl.BlockSpec((nseq, ts, D_RNN), lambda i: (0, tmap(i), 1)),
            pl.BlockSpec((nseq, HALO, D_RNN),
                         lambda i: (0, jnp.minimum((tmap(i) + 1) * per_tile, SEQ // HALO - 1), 1)),
            pl.BlockSpec((CONV_WIDTH, D_RNN), lambda i: (0, 0)),
            pl.BlockSpec((1, D_RNN), lambda i: (0, 0)),
        ] + gate_specs
        args = [y_in, y_in, y_in, conv_w, conv_b] + gate_args
        out_specs = [seq_block] + conv_blocks
        out_shape = [
            jax.ShapeDtypeStruct((nseq, SEQ, D_RNN), BF16),
            jax.ShapeDtypeStruct((n_tiles, nseq * ts, D_RNN), F32),
            jax.ShapeDtypeStruct((n_tiles, nseq * ts, D_RNN), BF16),
        ]
    return pl.pallas_call(
        functools.partial(_scan_kernel, ts=ts, n_tiles=n_tiles, nseq=nseq, reverse=reverse),
        grid=(n_tiles,),
        in_specs=in_specs,
        out_specs=out_specs,
        out_shape=out_shape,
        scratch_shapes=[
            pltpu.VMEM((N_CT, ts * SUBLANES, LANES), F32),
            pltpu.VMEM((N_CT, ts * SUBLANES, LANES), F32),
            pltpu.VMEM((N_CT, SUBLANES, LANES), F32),
        ],
        compiler_params=_params(("arbitrary",)),
        name="rglru_scan_bwd" if reverse else "rglru_scan_fwd",
    )(*args)


def kernel(x_prompt, x_sample, mem_prompt, mem_sample, norms, rel_bias, a_w_in, a_ln_g, a_ln_b, a_w_s, a_b_s, a_w_out, b_w_qkv, b_sink, b_w_o, c_w_in, c_conv_w, c_conv_b, c_w_a, c_b_a, c_w_x, c_b_x, c_lam, c_w_out, m_w_q, m_w_kv, m_w_o, f_w_gate_up, f_w_down):
    n_prompt = x_prompt.shape[0]
    nseq = n_prompt + x_sample.shape[0]
    assert nseq <= SUBLANES
    t = nseq * SEQ
    t_prompt = n_prompt * SEQ
    x = None
    x_parts = [x_prompt.reshape(t_prompt, D_MODEL), x_sample.reshape(t - t_prompt, D_MODEL)]
    mem = jnp.concatenate([mem_prompt, mem_sample], axis=0)

    def gain(i, k):
        return norms[i, k].reshape(1, D_MODEL)

    a_w_in_b, a_w_out_b = a_w_in.astype(BF16), a_w_out.astype(BF16)
    q_cols = N_HEADS * HEAD_DIM
    b_w_qkv_b = jnp.concatenate(
        [b_w_qkv[:, :, :q_cols] * (HEAD_DIM ** -0.5 * LOG2E), b_w_qkv[:, :, q_cols:]],
        axis=-1).astype(BF16)
    b_w_o_b = b_w_o.astype(BF16)
    c_w_in_b, c_w_out_b = c_w_in.astype(BF16), c_w_out.astype(BF16)
    m_w_q_b, m_w_o_b = m_w_q.astype(BF16), m_w_o.astype(BF16)
    f_gu_b, f_d_b = f_w_gate_up[:1].astype(BF16), f_w_down[:1].astype(BF16)

    kv_all = _kv_proj(mem, norms[:, 4].reshape(DEPTH, 1, D_MODEL), m_w_kv.astype(BF16))

    for i in range(DEPTH):
        kind, j = i % N_MIXERS, i // N_MIXERS
        if kind == 0:
            parts = x_parts if x is None else [x]
            z, mu, rstd = _gmlp_in(parts, gain(i, 0), a_w_in_b, j, tm=1024,
                                   tn=(4 if len(parts) == 1 else 2) * GMLP_GDIM)
            x = _gmlp_out(z, mu, rstd, a_ln_g[j].reshape(GMLP_GROUPS, 1, GMLP_GDIM),
                          a_ln_b[j].reshape(GMLP_GROUPS, 1, GMLP_GDIM),
                          a_w_s[j].astype(BF16), a_b_s[j].reshape(GMLP_GROUPS, BLOCK, 1),
                          a_w_out_b, j, parts, gain(i, 1), tm=512)
        elif kind == 1:
            qkv = _norm_matmul(x, gain(i, 0), b_w_qkv_b, j, tm=1024, tn=1024)
            o = _win_attention(qkv, b_sink[j] * LOG2E, _band_bias(rel_bias), nseq=nseq, r_blocks=8)
            x = _matmul_norm_res(o, b_w_o_b, j, x, gain(i, 1), tm=512)
        else:
            y_in = _norm_matmul(x, gain(i, 0), c_w_in_b, j, tm=1024, tn=D_RNN // 2, n_gelu=2)
            y_in = y_in.reshape(nseq, SEQ, 2 * D_RNN)
            scan_args = (c_conv_w[j], c_conv_b[j].reshape(1, D_RNN))
            hs = None
            for d in range(2):
                hs = _rglru_scan(y_in, *scan_args, _gate_weights(c_w_a[j, d], c_w_x[j, d]),
                                 (c_b_a[j, d] * 0.5).reshape(1, D_RNN),
                                 (c_b_x[j, d] * 0.5).reshape(1, D_RNN),
                                 c_lam[j, d].reshape(1, D_RNN), hs, ts=64, nseq=nseq,
                                 reverse=(d == 1))
            x = _matmul_norm_res(hs.reshape(t, D_RNN), c_w_out_b, j, x, gain(i, 1), tm=512)
        x = _mem_attention(x, gain(i, 2), gain(i, 3), m_w_q_b, kv_all, m_w_o_b, i, tm=1024)
        ffn = functools.partial(_ffn, x, gain(i, 5), gain(i, 6), f_gu_b, f_d_b, 0)
        if i < DEPTH - 1:
            x, f_gu_b, f_d_b = ffn(tm=768, tn=512, next_f32=(f_w_gate_up, f_w_down, i + 1))

    y_prompt = ffn(tm=1024, tn=256, row0=0, rows=t_prompt).reshape(n_prompt, SEQ, D_MODEL)
    y_sample = ffn(tm=1024, tn=256, row0=t_prompt, rows=t - t_prompt).reshape(
        nseq - n_prompt, SEQ, D_MODEL)
    return (y_prompt, y_sample)
```

```python
import functools

import numpy as np
import jax
import jax.numpy as jnp
from jax import lax
from jax.experimental import pallas as pl
from jax.experimental.pallas import tpu as pltpu

D_MODEL = 2048
SEQ = 4096
DEPTH = 4
N_MIXERS = 3
BLOCK = 128
HEAD_DIM = 128
EPS = 1e-6
NEG_INF = -1e30
GMLP_HALF = 2 * D_MODEL
GMLP_GROUPS = 8
GMLP_GDIM = GMLP_HALF // GMLP_GROUPS
N_HEADS = D_MODEL // HEAD_DIM
N_KV_HEADS = 4
KV_GROUP = N_HEADS // N_KV_HEADS
WINDOW = 128
N_BUCKETS = 32
MAX_DISTANCE = 128
D_RNN = 2816
RNN_BLOCKS = 16
RNN_BDIM = D_RNN // RNN_BLOCKS
CONV_WIDTH = 4
CONV_LEFT = 2
LRU_C = 8.0
MEM_LEN = 256
MEM_HEADS = 4
MEM_WIDTH = MEM_HEADS * HEAD_DIM
D_FF = 5632

LANES = 128
SUBLANES = 8
VMEM_LIMIT = 56 * 1024 * 1024

F32 = jnp.float32
BF16 = jnp.bfloat16
LOG2E = float(np.log2(np.e))


def _params(sem):
    return pltpu.CompilerParams(dimension_semantics=sem, vmem_limit_bytes=VMEM_LIMIT)


def _rms(x, g):
    ms = jnp.mean(x * x, axis=-1, keepdims=True)
    return x * lax.rsqrt(ms + EPS) * g


def _gelu(x):
    k1 = float(-2.0 * np.sqrt(2.0 / np.pi) * LOG2E)
    return x / (1.0 + jnp.exp2(x * (k1 + (k1 * 0.044715) * (x * x))))


def _sigmoid(x):
    return 1.0 / (1.0 + jnp.exp(-x))


def _dot(a, b):
    return jnp.dot(a, b, preferred_element_type=F32)


def _norm_mm_kernel(x_ref, g_ref, w_ref, o_ref, h_scr, *, n_gelu):
    j = pl.program_id(1)

    def emit(act, first):
        if first:
            h_scr[...] = _rms(x_ref[...], g_ref[...]).astype(BF16)
        z = _dot(h_scr[...], w_ref[...])
        o_ref[...] = (_gelu(z) if act else z).astype(o_ref.dtype)

    pl.when(j == 0)(functools.partial(emit, n_gelu > 0, True))
    if n_gelu > 1:
        pl.when((j > 0) & (j < n_gelu))(functools.partial(emit, True, False))
    pl.when(j >= max(n_gelu, 1))(functools.partial(emit, False, False))


def _norm_matmul(x, g, w, layer, *, tm, tn, n_gelu=0):
    t, d = x.shape
    n = w.shape[2]
    return pl.pallas_call(
        functools.partial(_norm_mm_kernel, n_gelu=n_gelu),
        grid=(t // tm, n // tn),
        in_specs=[
            pl.BlockSpec((tm, d), lambda i, j: (i, 0)),
            pl.BlockSpec((1, d), lambda i, j: (0, 0)),
            pl.BlockSpec((None, d, tn), lambda i, j: (layer, 0, j)),
        ],
        out_specs=pl.BlockSpec((tm, tn), lambda i, j: (i, j)),
        out_shape=jax.ShapeDtypeStruct((t, n), BF16),
        scratch_shapes=[pltpu.VMEM((tm, d), BF16)],
        compiler_params=_params(("parallel", "arbitrary")),
        name="norm_matmul",
    )(x, g, w)


def _part_specs(parts, tm, tile_of):
    specs, ranges, lo = [], [], 0
    for p in parts:
        n = p.shape[0] // tm
        specs.append(pl.BlockSpec(
            (tm, p.shape[1]),
            lambda *idx, lo=lo, n=n: (jnp.clip(tile_of(*idx) - lo, 0, n - 1), 0)))
        ranges.append((lo, lo + n))
        lo += n
    return specs, tuple(ranges)


def _for_owner(i, ranges, refs, cond, fn):
    for (lo, hi), ref in zip(ranges, refs):
        pl.when(cond & (i >= lo) & (i < hi))(functools.partial(fn, ref))


MXU_COLS = 256


def _gmlp_in_kernel(*refs, ranges, n_col, tn):
    n_parts = len(ranges)
    x_refs = refs[:n_parts]
    g_ref, w_ref, z_ref, mu_ref, rs_ref, h_scr, s1_scr, s2_scr = refs[n_parts:]
    i, j = pl.program_id(0), pl.program_id(1)
    tm = h_scr.shape[0]

    def step(is_v, x_ref=None):
        if x_ref is not None:
            h_scr[...] = _rms(x_ref[...], g_ref[...]).astype(BF16)
            s1_scr[...] = jnp.zeros_like(s1_scr)
            s2_scr[...] = jnp.zeros_like(s2_scr)
        p1 = jnp.zeros((tm, LANES), F32)
        p2 = jnp.zeros((tm, LANES), F32)
        per_tile = GMLP_GDIM // MXU_COLS
        for c in range(tn // MXU_COLS):
            cols = slice(c * MXU_COLS, (c + 1) * MXU_COLS)
            zc = _gelu(_dot(h_scr[...], w_ref[:, cols]))
            lo = (c % per_tile) * MXU_COLS
            z_ref[c // per_tile, :, lo:lo + MXU_COLS] = zc.astype(z_ref.dtype)
            if is_v:
                for k in range(MXU_COLS // LANES):
                    zk = zc[:, k * LANES:(k + 1) * LANES]
                    p1 = p1 + zk
                    p2 = p2 + zk * zk
        if is_v:
            s1_scr[...] += p1
            s2_scr[...] += p2

    _for_owner(i, ranges, x_refs, j == 0, functools.partial(step, False))
    pl.when((j > 0) & (j < n_col // 2))(functools.partial(step, False))
    pl.when(j >= n_col // 2)(functools.partial(step, True))

    @pl.when(j == n_col - 1)
    def _():
        mu = jnp.sum(s1_scr[...], axis=-1, keepdims=True) * (1.0 / GMLP_HALF)
        msq = jnp.sum(s2_scr[...], axis=-1, keepdims=True) * (1.0 / GMLP_HALF)
        mu_ref[...] = mu
        rs_ref[...] = lax.rsqrt(jnp.maximum(msq - mu * mu, 0.0) + EPS)


def _gmlp_in(x_parts, g, w, layer, *, tm, tn):
    t = sum(p.shape[0] for p in x_parts)
    d = x_parts[0].shape[1]
    n_col = 2 * GMLP_HALF // tn
    sub = tn // GMLP_GDIM
    x_specs, ranges = _part_specs(x_parts, tm, lambda i, j: i)
    return pl.pallas_call(
        functools.partial(_gmlp_in_kernel, ranges=ranges, n_col=n_col, tn=tn),
        grid=(t // tm, n_col),
        in_specs=x_specs + [
            pl.BlockSpec((1, d), lambda i, j: (0, 0)),
            pl.BlockSpec((None, d, tn), lambda i, j: (layer, 0, j)),
        ],
        out_specs=[
            pl.BlockSpec((sub, tm, GMLP_GDIM), lambda i, j: (j, i, 0)),
            pl.BlockSpec((tm, 1), lambda i, j: (i, 0)),
            pl.BlockSpec((tm, 1), lambda i, j: (i, 0)),
        ],
        out_shape=[
            jax.ShapeDtypeStruct((2 * GMLP_GROUPS, t, GMLP_GDIM), BF16),
            jax.ShapeDtypeStruct((t, 1), F32),
            jax.ShapeDtypeStruct((t, 1), F32),
        ],
        scratch_shapes=[
            pltpu.VMEM((tm, d), BF16),
            pltpu.VMEM((tm, LANES), F32),
            pltpu.VMEM((tm, LANES), F32),
        ],
        compiler_params=_params(("parallel", "arbitrary")),
        name="gmlp_in",
    )(*x_parts, g, w)


def _mm_norm_res_kernel(a_ref, w_ref, x_ref, g_ref, o_ref):
    y = _dot(a_ref[...], w_ref[...])
    o_ref[...] = x_ref[...] + _rms(y, g_ref[...])


def _matmul_norm_res(a, w, layer, x, g, *, tm):
    t, k = a.shape
    d = w.shape[2]
    return pl.pallas_call(
        _mm_norm_res_kernel,
        grid=(t // tm,),
        in_specs=[
            pl.BlockSpec((tm, k), lambda i: (i, 0)),
            pl.BlockSpec((None, k, d), lambda i: (layer, 0, 0), pipeline_mode=pl.Buffered(1)),
            pl.BlockSpec((tm, d), lambda i: (i, 0)),
            pl.BlockSpec((1, d), lambda i: (0, 0)),
        ],
        out_specs=pl.BlockSpec((tm, d), lambda i: (i, 0)),
        out_shape=jax.ShapeDtypeStruct((t, d), F32),
        compiler_params=_params(("parallel",)),
        name="matmul_norm_res",
    )(a, w, x, g)


PAIRS = GMLP_GROUPS // 2


def _gmlp_out_kernel(*refs, ranges, tm):
    n_parts = len(ranges)
    (u1_ref, v1_ref, mu1_ref, rs1_ref, lng1_ref, lnb1_ref, ws1_ref, bs1_ref,
     u2_ref, v2_ref, mu2_ref, rs2_ref, lng2_ref, lnb2_ref, ws2_ref, bs2_ref,
     wo_ref, g_ref) = refs[:18]
    x_refs = refs[18:18 + n_parts]
    o_ref, a0_scr, a1_scr = refs[18 + n_parts:]
    s = pl.program_id(0)
    q = jnp.maximum(s - 1, 0)
    tile, pair = q // PAIRS, q % PAIRS

    @pl.when(s == 0)
    def _():
        a0_scr[...] = jnp.zeros_like(a0_scr)
        a1_scr[...] = jnp.zeros_like(a1_scr)

    def prep(u_ref, v_ref, mu_ref, rs_ref, lng_ref, lnb_ref, ws_ref, bs_ref):
        vn = ((v_ref[0].astype(F32) - mu_ref[...]) * rs_ref[...] * lng_ref[0]
              + lnb_ref[0]).astype(BF16)
        ws = ws_ref[0]
        bs = bs_ref[0]
        parts = []
        for n in range(tm // BLOCK):
            rows = slice(n * BLOCK, (n + 1) * BLOCK)
            mixed = _dot(ws, vn[rows, :]) + bs
            parts.append((u_ref[0, rows, :].astype(F32) * mixed).astype(BF16))
        return jnp.concatenate(parts, axis=0)

    def step(first, x_ref=None):
        row0 = pl.multiple_of(pair * (2 * GMLP_GDIM), 2 * GMLP_GDIM)
        proj = _dot(a0_scr[...], wo_ref[pl.ds(row0, GMLP_GDIM), :])
        if first:
            o_ref[...] = proj
        else:
            o_ref[...] += proj
        a1_scr[...] = prep(u1_ref, v1_ref, mu1_ref, rs1_ref, lng1_ref, lnb1_ref, ws1_ref, bs1_ref)
        o_ref[...] += _dot(a1_scr[...], wo_ref[pl.ds(row0 + GMLP_GDIM, GMLP_GDIM), :])
        a0_scr[...] = prep(u2_ref, v2_ref, mu2_ref, rs2_ref, lng2_ref, lnb2_ref, ws2_ref, bs2_ref)
        if x_ref is not None:
            o_ref[...] = x_ref[...] + _rms(o_ref[...], g_ref[...])

    pl.when(pair == 0)(functools.partial(step, True))
    pl.when((pair > 0) & (pair < PAIRS - 1))(functools.partial(step, False))
    _for_owner(tile, ranges, x_refs, pair == PAIRS - 1, functools.partial(step, False))


def _gmlp_out(z, mu, rstd, ln_g, ln_b, w_s, b_s, w_out, layer, x_parts, g, *, tm):
    t = sum(p.shape[0] for p in x_parts)
    d = x_parts[0].shape[1]
    n_tiles = t // tm
    last = GMLP_GROUPS * n_tiles - 1

    def proj_tile(s):
        return jnp.maximum(s - 1, 0) // PAIRS

    def prep_specs(pair_of):
        def at(f):
            return lambda s: f(*pair_of(s))
        return [
            pl.BlockSpec((1, tm, GMLP_GDIM), at(lambda i, c: (c, i, 0))),
            pl.BlockSpec((1, tm, GMLP_GDIM), at(lambda i, c: (GMLP_GROUPS + c, i, 0))),
            pl.BlockSpec((tm, 1), at(lambda i, c: (i, 0))),
            pl.BlockSpec((tm, 1), at(lambda i, c: (i, 0))),
            pl.BlockSpec((1, 1, GMLP_GDIM), at(lambda i, c: (c, 0, 0))),
            pl.BlockSpec((1, 1, GMLP_GDIM), at(lambda i, c: (c, 0, 0))),
            pl.BlockSpec((1, BLOCK, BLOCK), at(lambda i, c: (c, 0, 0))),
            pl.BlockSpec((1, BLOCK, 1), at(lambda i, c: (c, 0, 0))),
        ]

    def split(gidx):
        gidx = jnp.clip(gidx, 0, last)
        return gidx // GMLP_GROUPS, gidx % GMLP_GROUPS

    x_specs, ranges = _part_specs(x_parts, tm, proj_tile)
    prep_args = (z, z, mu, rstd, ln_g, ln_b, w_s, b_s)
    return pl.pallas_call(
        functools.partial(_gmlp_out_kernel, ranges=ranges, tm=tm),
        grid=(PAIRS * n_tiles + 1,),
        in_specs=prep_specs(lambda s: split(2 * s - 1)) + prep_specs(lambda s: split(2 * s)) + [
            pl.BlockSpec((None, GMLP_HALF, d), lambda s: (layer, 0, 0),
                         pipeline_mode=pl.Buffered(1)),
            pl.BlockSpec((1, d), lambda s: (0, 0)),
        ] + x_specs,
        out_specs=pl.BlockSpec((tm, d), lambda s: (proj_tile(s), 0)),
        out_shape=jax.ShapeDtypeStruct((t, d), F32),
        scratch_shapes=[
            pltpu.VMEM((tm, GMLP_GDIM), BF16),
            pltpu.VMEM((tm, GMLP_GDIM), BF16),
        ],
        compiler_params=_params(("arbitrary",)),
        name="gmlp_out",
    )(*prep_args, *prep_args, w_out, g, *x_parts)


def _win_attn_kernel(sink_ref, q_ref, kp_ref, kc_ref, kn_ref, vp_ref, vc_ref, vn_ref, bias_ref,
                     o_ref, *, r_blocks, n_steps):
    i = pl.program_id(1)
    kall = jnp.concatenate([kp_ref[0], kc_ref[0], kn_ref[0]], axis=0)
    vall = jnp.concatenate([vp_ref[0], vc_ref[0], vn_ref[0]], axis=0)
    col = lax.broadcasted_iota(jnp.int32, (1, 3 * BLOCK), 1)
    sk = jnp.concatenate(
        [jnp.full((BLOCK, 1), sink_ref[h], F32) for h in range(N_HEADS)], axis=0)
    hrows = KV_GROUP * BLOCK
    for r in range(r_blocks):
        rows = slice(r * BLOCK, (r + 1) * BLOCK)
        oob = None
        if r == 0:
            oob = (col < BLOCK) & (i == 0)
        if r == r_blocks - 1:
            hi = (col >= 2 * BLOCK) & (i == n_steps - 1)
            oob = hi if oob is None else (oob | hi)
        parts = []
        for kh in range(N_KV_HEADS):
            kb = kall[r * BLOCK:(r + 3) * BLOCK, kh * HEAD_DIM:(kh + 1) * HEAD_DIM]
            qs = jnp.concatenate(
                [q_ref[0, rows, h * HEAD_DIM:(h + 1) * HEAD_DIM]
                 for h in range(kh * KV_GROUP, (kh + 1) * KV_GROUP)], axis=0)
            parts.append(
                lax.dot_general(qs, kb, (((1,), (1,)), ((), ())), preferred_element_type=F32))
        s = jnp.concatenate(parts, axis=0) + bias_ref[...]
        if oob is not None:
            s = jnp.where(oob, NEG_INF, s)
        m = jnp.maximum(jnp.max(s, axis=-1, keepdims=True), sk)
        p = jnp.exp2(s - m)
        inv = 1.0 / (jnp.sum(p, axis=-1, keepdims=True) + jnp.exp2(sk - m))
        pb = p.astype(BF16)
        for kh in range(N_KV_HEADS):
            vb = vall[r * BLOCK:(r + 3) * BLOCK, kh * HEAD_DIM:(kh + 1) * HEAD_DIM]
            hsl = slice(kh * hrows, (kh + 1) * hrows)
            o = _dot(pb[hsl], vb) * inv[hsl]
            for gq in range(KV_GROUP):
                h = kh * KV_GROUP + gq
                o_ref[0, rows, h * HEAD_DIM:(h + 1) * HEAD_DIM] = (
                    o[gq * BLOCK:(gq + 1) * BLOCK].astype(o_ref.dtype))


def _win_attention(qkv, sink, bias, *, nseq, r_blocks):
    tq = r_blocks * BLOCK
    n_steps = SEQ // tq
    nb = SEQ // BLOCK
    qkv3 = qkv.reshape(nseq, SEQ, qkv.shape[-1])
    kcol = N_HEADS * HEAD_DIM // (N_KV_HEADS * HEAD_DIM)
    vcol = kcol + 1
    kvw = N_KV_HEADS * HEAD_DIM

    def prev_map(c):
        return lambda b, i: (b, jnp.maximum(i * r_blocks - 1, 0), c)

    def cur_map(c):
        return lambda b, i: (b, i, c)

    def next_map(c):
        return lambda b, i: (b, jnp.minimum((i + 1) * r_blocks, nb - 1), c)

    out = pl.pallas_call(
        functools.partial(_win_attn_kernel, r_blocks=r_blocks, n_steps=n_steps),
        grid=(nseq, n_steps),
        in_specs=[
            pl.BlockSpec(memory_space=pltpu.SMEM),
            pl.BlockSpec((1, tq, N_HEADS * HEAD_DIM), lambda b, i: (b, i, 0)),
            pl.BlockSpec((1, BLOCK, kvw), prev_map(kcol)),
            pl.BlockSpec((1, tq, kvw), cur_map(kcol)),
            pl.BlockSpec((1, BLOCK, kvw), next_map(kcol)),
            pl.BlockSpec((1, BLOCK, kvw), prev_map(vcol)),
            pl.BlockSpec((1, tq, kvw), cur_map(vcol)),
            pl.BlockSpec((1, BLOCK, kvw), next_map(vcol)),
            pl.BlockSpec((N_HEADS * BLOCK, 3 * BLOCK), lambda b, i: (0, 0)),
        ],
        out_specs=pl.BlockSpec((1, tq, N_HEADS * HEAD_DIM), lambda b, i: (b, i, 0)),
        out_shape=jax.ShapeDtypeStruct((nseq, SEQ, N_HEADS * HEAD_DIM), BF16),
        compiler_params=_params(("parallel", "parallel")),
        name="win_attention",
    )(sink, qkv3, qkv3, qkv3, qkv3, qkv3, qkv3, qkv3, bias)
    return out.reshape(nseq * SEQ, N_HEADS * HEAD_DIM)


def _t5_bucket(rel):
    half = N_BUCKETS // 2
    exact = half // 2
    n = np.abs(rel)
    large = exact + (np.log(np.maximum(n, 1) / exact) / np.log(MAX_DISTANCE / exact)
                     * (half - exact)).astype(np.int32)
    large = np.minimum(large, half - 1)
    return (np.where(rel > 0, half, 0) + np.where(n < exact, n, large)).astype(np.int32)


def _band_bias(rel_table):
    rel = np.arange(3 * BLOCK)[None, :] - BLOCK - np.arange(BLOCK)[:, None]
    onehot = (_t5_bucket(rel).reshape(1, -1) == np.arange(N_BUCKETS)[:, None]).astype(np.float32)
    bias = jnp.einsum("bh,bn->hn", rel_table.astype(F32) * LOG2E, onehot,
                      precision=lax.Precision.HIGHEST)
    bias = jnp.where((np.abs(rel) <= WINDOW).reshape(1, -1), bias, NEG_INF)
    return bias.reshape(N_HEADS * BLOCK, 3 * BLOCK)


def _kv_kernel(mem_ref, g_ref, w_ref, o_ref):
    h = _rms(mem_ref[0], g_ref[0]).astype(BF16)
    o_ref[0, 0] = _dot(h, w_ref[0]).astype(o_ref.dtype)


def _kv_proj(mem, g, w_kv):
    nseq = mem.shape[0]
    return pl.pallas_call(
        _kv_kernel,
        grid=(DEPTH, nseq),
        in_specs=[
            pl.BlockSpec((1, MEM_LEN, D_MODEL), lambda l, b: (b, 0, 0)),
            pl.BlockSpec((1, 1, D_MODEL), lambda l, b: (l, 0, 0)),
            pl.BlockSpec((1, D_MODEL, 2 * MEM_WIDTH), lambda l, b: (l, 0, 0)),
        ],
        out_specs=pl.BlockSpec((1, 1, MEM_LEN, 2 * MEM_WIDTH), lambda l, b: (l, b, 0, 0)),
        out_shape=jax.ShapeDtypeStruct((DEPTH, nseq, MEM_LEN, 2 * MEM_WIDTH), BF16),
        compiler_params=_params(("parallel", "parallel")),
        name="kv_proj",
    )(mem, g, w_kv)


def _mem_attn_kernel(x_ref, gpre_ref, gpost_ref, wq_ref, kv_ref, wo_ref, o_ref):
    x = x_ref[...]
    h = _rms(x, gpre_ref[...]).astype(BF16)
    q = (_dot(h, wq_ref[...]) * (HEAD_DIM ** -0.5)).astype(BF16)
    outs = []
    for hd in range(MEM_HEADS):
        sl = slice(hd * HEAD_DIM, (hd + 1) * HEAD_DIM)
        k = kv_ref[0, :, sl]
        v = kv_ref[0, :, MEM_WIDTH + hd * HEAD_DIM:MEM_WIDTH + (hd + 1) * HEAD_DIM]
        s = lax.dot_general(q[:, sl], k, (((1,), (1,)), ((), ())), preferred_element_type=F32)
        m = jnp.max(s, axis=-1, keepdims=True)
        p = jnp.exp(s - m)
        denom = jnp.sum(p, axis=-1, keepdims=True)
        outs.append((_dot(p.astype(BF16), v) / denom).astype(BF16))
    y = _dot(jnp.concatenate(outs, axis=-1), wo_ref[...])
    o_ref[...] = x + _rms(y, gpost_ref[...])


def _mem_attention(x, gpre, gpost, w_q, kv, w_o, layer, *, tm):
    t, d = x.shape
    per_seq = SEQ // tm
    return pl.pallas_call(
        _mem_attn_kernel,
        grid=(t // tm,),
        in_specs=[
            pl.BlockSpec((tm, d), lambda i: (i, 0)),
            pl.BlockSpec((1, d), lambda i: (0, 0)),
            pl.BlockSpec((1, d), lambda i: (0, 0)),
            pl.BlockSpec((None, d, MEM_WIDTH), lambda i: (layer, 0, 0)),
            pl.BlockSpec((None, 1, MEM_LEN, 2 * MEM_WIDTH), lambda i: (layer, i // per_seq, 0, 0)),
            pl.BlockSpec((None, MEM_WIDTH, d), lambda i: (layer, 0, 0)),
        ],
        out_specs=pl.BlockSpec((tm, d), lambda i: (i, 0)),
        out_shape=jax.ShapeDtypeStruct((t, d), F32),
        compiler_params=_params(("parallel",)),
        name="mem_attention",
    )(x, gpre, gpost, w_q, kv, w_o)


def _ffn_kernel(*refs, n_chunks, cast_next):
    if cast_next:
        (x_ref, gpre_ref, gpost_ref, wg_ref, wu_ref, wd_ref, ngu_ref, nd_ref,
         o_ref, ngu_out, nd_out, h_scr) = refs
    else:
        x_ref, gpre_ref, gpost_ref, wg_ref, wu_ref, wd_ref, o_ref, h_scr = refs
    j = pl.program_id(1)

    def chunk(first, last):
        if cast_next:
            ngu_out[...] = ngu_ref[...].astype(BF16)
            nd_out[...] = nd_ref[...].astype(BF16)
        if first:
            h_scr[...] = _rms(x_ref[...], gpre_ref[...]).astype(BF16)
        h = h_scr[...]
        gate = _dot(h, wg_ref[...])
        up = _dot(h, wu_ref[...])
        a = (gate * _sigmoid(gate) * up).astype(BF16)
        if first:
            o_ref[...] = _dot(a, wd_ref[...])
        else:
            o_ref[...] += _dot(a, wd_ref[...])
        if last:
            o_ref[...] = x_ref[...] + _rms(o_ref[...], gpost_ref[...])

    pl.when(j == 0)(functools.partial(chunk, True, False))
    pl.when((j > 0) & (j < n_chunks - 1))(functools.partial(chunk, False, False))
    pl.when(j == n_chunks - 1)(functools.partial(chunk, False, True))


def _ffn(x, gpre, gpost, w_gate_up, w_down, layer, *, tm, tn, row0=0, rows=None, next_f32=None):
    d = x.shape[1]
    t = x.shape[0] if rows is None else rows
    tile0 = row0 // tm
    n_chunks = D_FF // tn
    n_tiles = t // tm
    in_specs_next, out_specs_next, out_shape_next, args_next = [], [], [], []
    if next_f32 is not None:
        ngu, nd, nl = next_f32
        gu_blk = (d // n_tiles, 2 * D_FF // n_chunks)
        d_blk = (D_FF // (n_tiles * n_chunks), d)
        assert gu_blk[0] * n_tiles == d and gu_blk[1] * n_chunks == 2 * D_FF
        assert d_blk[0] * n_tiles * n_chunks == D_FF and d_blk[0] % (2 * SUBLANES) == 0
        in_specs_next = [
            pl.BlockSpec((None,) + gu_blk, lambda i, j: (nl, i, j)),
            pl.BlockSpec((None,) + d_blk, lambda i, j: (nl, i * n_chunks + j, 0)),
        ]
        out_specs_next = [
            pl.BlockSpec(gu_blk, lambda i, j: (i, j)),
            pl.BlockSpec(d_blk, lambda i, j: (i * n_chunks + j, 0)),
        ]
        out_shape_next = [
            jax.ShapeDtypeStruct((d, 2 * D_FF), BF16),
            jax.ShapeDtypeStruct((D_FF, d), BF16),
        ]
        args_next = [ngu, nd]
    x_bytes = tm * d * 4
    fixed = 2 * x_bytes + tm * d * 2 + 2 * 3 * d * tn * 2
    temps = 2 * tm * tn * 4 + tm * tn * 2 + tm * d * 4
    x_mode = None if fixed + 2 * x_bytes + temps <= VMEM_LIMIT else pl.Buffered(1)
    out = pl.pallas_call(
        functools.partial(_ffn_kernel, n_chunks=n_chunks, cast_next=next_f32 is not None),
        grid=(n_tiles, n_chunks),
        in_specs=[
            pl.BlockSpec((tm, d), lambda i, j: (i + tile0, 0), pipeline_mode=x_mode),
            pl.BlockSpec((1, d), lambda i, j: (0, 0)),
            pl.BlockSpec((1, d), lambda i, j: (0, 0)),
            pl.BlockSpec((None, d, tn), lambda i, j: (layer, 0, j)),
            pl.BlockSpec((None, d, tn), lambda i, j: (layer, 0, j + n_chunks)),
            pl.BlockSpec((None, tn, d), lambda i, j: (layer, j, 0)),
        ] + in_specs_next,
        out_specs=[pl.BlockSpec((tm, d), lambda i, j: (i, 0))] + out_specs_next,
        out_shape=[jax.ShapeDtypeStruct((t, d), F32)] + out_shape_next,
        scratch_shapes=[pltpu.VMEM((tm, d), BF16)],
        compiler_params=_params(("parallel", "arbitrary")),
        name="ffn",
    )(x, gpre, gpost, w_gate_up, w_gate_up, w_down, *args_next)
    if next_f32 is None:
        return out[0]
    return out[0], out[1][None], out[2][None]


N_CT = D_RNN // LANES
HALO = 2 * SUBLANES


def _gate_windows():
    lo_hi = []
    for j in range(N_CT):
        h0 = (LANES * j) // RNN_BDIM
        h1 = (LANES * j + LANES - 1) // RNN_BDIM
        lo = (RNN_BDIM * h0) // LANES * LANES
        hi = -(-(RNN_BDIM * (h1 + 1)) // LANES) * LANES
        lo_hi.append((lo, hi))
    width = max(hi - lo for lo, hi in lo_hi)
    starts = tuple(min(lo, D_RNN - width) for lo, _ in lo_hi)
    return starts, width


GATE_STARTS, GATE_WIDTH = _gate_windows()


def _gate_weights(w_a, w_x):
    def tile(w, j):
        st = GATE_STARTS[j]
        pieces = []
        for h in range((LANES * j) // RNN_BDIM, (LANES * j + LANES - 1) // RNN_BDIM + 1):
            c0 = max(LANES * j, RNN_BDIM * h) - RNN_BDIM * h
            c1 = min(LANES * (j + 1), RNN_BDIM * (h + 1)) - RNN_BDIM * h
            top = RNN_BDIM * h - st
            pieces.append(jnp.pad(w[h, :, c0:c1], ((top, GATE_WIDTH - RNN_BDIM - top), (0, 0))))
        return jnp.concatenate(pieces, axis=1)

    wa, wx = (w_a * 0.5).astype(BF16), (w_x * 0.5).astype(BF16)
    return jnp.stack([jnp.concatenate([tile(wa, j), tile(wx, j)], axis=1) for j in range(N_CT)])


def _scan_kernel(*refs, ts, n_tiles, nseq, reverse):
    if reverse:
        (conv_ref, convb_ref, wp_ref, ba_ref, bx_ref, lam_ref, hsf_ref, gate_ref,
         o_ref, a_scr, u_scr, h_scr) = refs
    else:
        (xp_ref, xc_ref, xn_ref, cw_ref, cb_ref, wp_ref, ba_ref, bx_ref, lam_ref,
         o_ref, conv_ref, convb_ref, a_scr, u_scr, h_scr) = refs
    i = pl.program_id(0)
    tile = (n_tiles - 1 - i) if reverse else i

    @pl.when(i == 0)
    def _():
        h_scr[...] = jnp.zeros_like(h_scr)
        a_scr[...] = jnp.zeros_like(a_scr)
        u_scr[...] = jnp.zeros_like(u_scr)

    def conv():
        cw = cw_ref[...]
        cb = cb_ref[...]
        shifts = [k - CONV_LEFT for k in range(CONV_WIDTH) if k != CONV_LEFT]
        rr = lax.broadcasted_iota(jnp.int32, (ts, ts + 2 * HALO), 0)
        cc = lax.broadcasted_iota(jnp.int32, (ts, ts + 2 * HALO), 1)
        sel = jnp.concatenate(
            [jnp.where(cc == rr + HALO + s, 1.0, 0.0).astype(BF16) for s in shifts], axis=0)
        for b in range(nseq):
            prev = xp_ref[b]
            nxt = xn_ref[b]
            prev = jnp.where(tile == 0, jnp.zeros_like(prev), prev)
            nxt = jnp.where(tile == n_tiles - 1, jnp.zeros_like(nxt), nxt)
            cur = xc_ref[b]
            shifted = _dot(sel, jnp.concatenate([prev, cur, nxt], axis=0))
            acc = cb + cur.astype(F32) * cw[CONV_LEFT:CONV_LEFT + 1]
            for n, s in enumerate(shifts):
                k = s + CONV_LEFT
                acc = acc + shifted[n * ts:(n + 1) * ts] * cw[k:k + 1]
            conv_ref[b * ts:(b + 1) * ts, :] = acc
            convb_ref[b * ts:(b + 1) * ts, :] = acc.astype(BF16)

    def gate_tile(j):
        st = GATE_STARTS[j]
        sl = slice(LANES * j, LANES * (j + 1))
        pre = _dot(convb_ref[:, st:st + GATE_WIDTH], wp_ref[j])
        tr = jnp.tanh(pre[:, :LANES] + ba_ref[:, sl])
        ti = jnp.tanh(pre[:, LANES:] + bx_ref[:, sl])
        z = -lam_ref[:, sl]
        softplus = jnp.maximum(z, 0.0) + jnp.log(1.0 + jnp.exp(-jnp.abs(z)))
        half = (-0.5 * LRU_C * LOG2E) * softplus
        a = jnp.exp2(half + half * tr)
        y = 1.0 - a * a
        root = jnp.where(y > 0.0, y * lax.rsqrt(y), 0.0)
        hx = 0.5 * conv_ref[:, sl]
        u = root * (hx + hx * ti)
        for b in range(nseq):
            a_scr[j, pl.ds(b, ts, stride=SUBLANES), :] = a[b * ts:(b + 1) * ts]
            u_scr[j, pl.ds(b, ts, stride=SUBLANES), :] = u[b * ts:(b + 1) * ts]

    if not reverse:
        conv()
    for j in range(N_CT):
        gate_tile(j)

    def step(k, h):
        t = (ts - 1 - k) if reverse else k
        row = pl.multiple_of(t * SUBLANES, SUBLANES)
        h = a_scr[:, pl.ds(row, SUBLANES), :] * h + u_scr[:, pl.ds(row, SUBLANES), :]
        u_scr[:, pl.ds(row, SUBLANES), :] = h
        return h

    h_scr[...] = lax.fori_loop(0, ts, step, h_scr[...], unroll=8)

    for b in range(nseq):
        for j in range(N_CT):
            sl = slice(LANES * j, LANES * (j + 1))
            y = u_scr[j, pl.ds(b, ts, stride=SUBLANES), :]
            if reverse:
                y = gate_ref[b, :, sl].astype(F32) * (y + hsf_ref[b, :, sl].astype(F32))
            o_ref[b, :, sl] = y.astype(o_ref.dtype)


def _rglru_scan(y_in, conv_w, conv_b, wp, b_a, b_x, lam, fwd_out, *, ts, nseq, reverse):
    n_tiles = SEQ // ts
    per_tile = ts // HALO

    def tmap(i):
        return (n_tiles - 1 - i) if reverse else i

    gate_specs = [
        pl.BlockSpec((N_CT, GATE_WIDTH, 2 * LANES), lambda i: (0, 0, 0)),
        pl.BlockSpec((1, D_RNN), lambda i: (0, 0)),
        pl.BlockSpec((1, D_RNN), lambda i: (0, 0)),
        pl.BlockSpec((1, D_RNN), lambda i: (0, 0)),
    ]
    gate_args = [wp, b_a, b_x, lam]
    conv_blocks = [pl.BlockSpec((None, nseq * ts, D_RNN), lambda i: (tmap(i), 0, 0))] * 2
    seq_block = pl.BlockSpec((nseq, ts, D_RNN), lambda i: (0, tmap(i), 0))
    if reverse:
        hs_fwd, conv_f, conv_b16 = fwd_out
        in_specs = conv_blocks + gate_specs + [seq_block, seq_block]
        args = [conv_f, conv_b16] + gate_args + [hs_fwd, y_in]
        out_specs = seq_block
        out_shape = jax.ShapeDtypeStruct((nseq, SEQ, D_RNN), BF16)
    else:
        in_specs = [
            pl.BlockSpec((nseq, HALO, D_RNN),
                         lambda i: (0, jnp.maximum(tmap(i) * per_tile - 1, 0), 1)),
            pl.BlockSpec((nseq, ts, D_RNN), lambda i: (0, tmap(i), 1)),
            pl.BlockSpec((nseq, HALO, D_RNN),
                         lambda i: (0, jnp.minimum((tmap(i) + 1) * per_tile, SEQ // HALO - 1), 1)),
            pl.BlockSpec((CONV_WIDTH, D_RNN), lambda i: (0, 0)),
            pl.BlockSpec((1, D_RNN), lambda i: (0, 0)),
        ] + gate_specs
        args = [y_in, y_in, y_in, conv_w, conv_b] + gate_args
        out_specs = [seq_block] + conv_blocks
        out_shape = [
            jax.ShapeDtypeStruct((nseq, SEQ, D_RNN), BF16),
            jax.ShapeDtypeStruct((n_tiles, nseq * ts, D_RNN), F32),
            jax.ShapeDtypeStruct((n_tiles, nseq * ts, D_RNN), BF16),
        ]
    return pl.pallas_call(
        functools.partial(_scan_kernel, ts=ts, n_tiles=n_tiles, nseq=nseq, reverse=reverse),
        grid=(n_tiles,),
        in_specs=in_specs,
        out_specs=out_specs,
        out_shape=out_shape,
        scratch_shapes=[
            pltpu.VMEM((N_CT, ts * SUBLANES, LANES), F32),
            pltpu.VMEM((N_CT, ts * SUBLANES, LANES), F32),
            pltpu.VMEM((N_CT, SUBLANES, LANES), F32),
        ],
        compiler_params=_params(("arbitrary",)),
        name="rglru_scan_bwd" if reverse else "rglru_scan_fwd",
    )(*args)


def kernel(x_prompt, x_sample, mem_prompt, mem_sample, norms, rel_bias, a_w_in, a_ln_g, a_ln_b, a_w_s, a_b_s, a_w_out, b_w_qkv, b_sink, b_w_o, c_w_in, c_conv_w, c_conv_b, c_w_a, c_b_a, c_w_x, c_b_x, c_lam, c_w_out, m_w_q, m_w_kv, m_w_o, f_w_gate_up, f_w_down):
    n_prompt = x_prompt.shape[0]
    nseq = n_prompt + x_sample.shape[0]
    assert nseq <= SUBLANES
    t = nseq * SEQ
    t_prompt = n_prompt * SEQ
    x = None
    x_parts = [x_prompt.reshape(t_prompt, D_MODEL), x_sample.reshape(t - t_prompt, D_MODEL)]
    mem = jnp.concatenate([mem_prompt, mem_sample], axis=0)

    def gain(i, k):
        return norms[i, k].reshape(1, D_MODEL)

    a_w_in_b, a_w_out_b = a_w_in.astype(BF16), a_w_out.astype(BF16)
    q_cols = N_HEADS * HEAD_DIM
    b_w_qkv_b = jnp.concatenate(
        [b_w_qkv[:, :, :q_cols] * (HEAD_DIM ** -0.5 * LOG2E), b_w_qkv[:, :, q_cols:]],
        axis=-1).astype(BF16)
    b_w_o_b = b_w_o.astype(BF16)
    c_w_in_b, c_w_out_b = c_w_in.astype(BF16), c_w_out.astype(BF16)
    m_w_q_b, m_w_o_b = m_w_q.astype(BF16), m_w_o.astype(BF16)
    f_gu_b, f_d_b = f_w_gate_up[:1].astype(BF16), f_w_down[:1].astype(BF16)

    kv_all = _kv_proj(mem, norms[:, 4].reshape(DEPTH, 1, D_MODEL), m_w_kv.astype(BF16))

    for i in range(DEPTH):
        kind, j = i % N_MIXERS, i // N_MIXERS
        if kind == 0:
            parts = x_parts if x is None else [x]
            z, mu, rstd = _gmlp_in(parts, gain(i, 0), a_w_in_b, j, tm=1024,
                                   tn=(4 if len(parts) == 1 else 2) * GMLP_GDIM)
            x = _gmlp_out(z, mu, rstd, a_ln_g[j].reshape(GMLP_GROUPS, 1, GMLP_GDIM),
                          a_ln_b[j].reshape(GMLP_GROUPS, 1, GMLP_GDIM),
                          a_w_s[j].astype(BF16), a_b_s[j].reshape(GMLP_GROUPS, BLOCK, 1),
                          a_w_out_b, j, parts, gain(i, 1), tm=512)
        elif kind == 1:
            qkv = _norm_matmul(x, gain(i, 0), b_w_qkv_b, j, tm=1024, tn=1024)
            o = _win_attention(qkv, b_sink[j] * LOG2E, _band_bias(rel_bias), nseq=nseq, r_blocks=16)
            x = _matmul_norm_res(o, b_w_o_b, j, x, gain(i, 1), tm=512)
        else:
            y_in = _norm_matmul(x, gain(i, 0), c_w_in_b, j, tm=1024, tn=D_RNN // 2, n_gelu=2)
            y_in = y_in.reshape(nseq, SEQ, 2 * D_RNN)
            scan_args = (c_conv_w[j], c_conv_b[j].reshape(1, D_RNN))
            hs = None
            for d in range(2):
                hs = _rglru_scan(y_in, *scan_args, _gate_weights(c_w_a[j, d], c_w_x[j, d]),
                                 (c_b_a[j, d] * 0.5).reshape(1, D_RNN),
                                 (c_b_x[j, d] * 0.5).reshape(1, D_RNN),
                                 c_lam[j, d].reshape(1, D_RNN), hs, ts=64, nseq=nseq,
                                 reverse=(d == 1))
            x = _matmul_norm_res(hs.reshape(t, D_RNN), c_w_out_b, j, x, gain(i, 1), tm=512)
        x = _mem_attention(x, gain(i, 2), gain(i, 3), m_w_q_b, kv_all, m_w_o_b, i, tm=1024)
        ffn = functools.partial(_ffn, x, gain(i, 5), gain(i, 6), f_gu_b, f_d_b, 0)
        if i < DEPTH - 1:
            x, f_gu_b, f_d_b = ffn(tm=768, tn=512, next_f32=(f_w_gate_up, f_w_down, i + 1))

    y_prompt = ffn(tm=1024, tn=256, row0=0, rows=t_prompt).reshape(n_prompt, SEQ, D_MODEL)
    y_sample = ffn(tm=1024, tn=256, row0=t_prompt, rows=t - t_prompt).reshape(
        nseq - n_prompt, SEQ, D_MODEL)
    return (y_prompt, y_sample)
```
